```python
import math
import jax
import jax.numpy as jnp
from jax import lax
import numpy as np

D_MODEL = 2048
BATCH = 8
SEQ = 2048
DEPTH = 2

GRID_W = 64
CTX_LEN = 256
EXPAND = 2
MIX_WIDTH = EXPAND * D_MODEL

HEAD_DIM = 128
ATTN_WIDTH = MIX_WIDTH // 2
N_Q_HEADS = ATTN_WIDTH // HEAD_DIM
N_KV_HEADS = N_Q_HEADS // 4
Q_PER_KV = N_Q_HEADS // N_KV_HEADS
KV_WIDTH = N_KV_HEADS * HEAD_DIM
Q_BLOCK = 128
ROPE_THETA = 10000.0
ROPE_AXIS_DIM = HEAD_DIM // 2

SSD_WIDTH = MIX_WIDTH - ATTN_WIDTH
SSD_HEAD_DIM = 64
N_SSD_HEADS = SSD_WIDTH // SSD_HEAD_DIM
N_SSD_GROUPS = 4
HEADS_PER_GROUP = N_SSD_HEADS // N_SSD_GROUPS
D_STATE = 128
D_CONV = 5
SSD_CHUNK = 128
CONV_CH = SSD_WIDTH + 2 * N_SSD_GROUPS * D_STATE

CTX_COL0 = 2 * ATTN_WIDTH + SSD_WIDTH
EV_SPLITS = [ATTN_WIDTH, 2 * ATTN_WIDTH, CTX_COL0, CTX_COL0 + KV_WIDTH, CTX_COL0 + 2 * KV_WIDTH, CTX_COL0 + 2 * KV_WIDTH + CONV_CH]
CTX_SPLITS = [KV_WIDTH, 2 * KV_WIDTH, 2 * KV_WIDTH + CONV_CH]
EV_IN_COLS = CTX_COL0 + 2 * KV_WIDTH + CONV_CH + 2 * N_SSD_HEADS

FOURIER_WIDTH = MIX_WIDTH
N_FOURIER_GROUPS = 8
FOURIER_GROUP = FOURIER_WIDTH // N_FOURIER_GROUPS

N_EVEN = (DEPTH + 1) // 2
N_ODD = DEPTH // 2
RMS_EPS = 1e-6

kernel_name = 'hybrid_dit_gqa_ssd_fourier'


def rms_norm(x, w):
    xf = x.astype(jnp.float32)
    y = xf * lax.rsqrt(jnp.mean(xf * xf, axis=-1, keepdims=True) + RMS_EPS)
    return (y * w.astype(jnp.float32)).astype(x.dtype)


def ada_mod(cond, ada_w, ada_b):
    return jax.nn.silu(cond) @ ada_w + ada_b


def modulate(x, norm_w, mod):
    shift, scale, gate = jnp.split(mod, 3, axis=-1)
    h = rms_norm(x, norm_w) * (1 + scale[:, None]) + shift[:, None]
    return h, gate[:, None]


def heads(t, n, d):
    return t.reshape(t.shape[0], t.shape[1], n, d)


def axial_rope_angles(rows):
    pos = jnp.arange(rows * GRID_W)
    row = (pos // GRID_W).astype(jnp.float32)
    col = (pos % GRID_W).astype(jnp.float32)
    inv_freq = ROPE_THETA ** (-jnp.arange(0, ROPE_AXIS_DIM, 2, dtype=jnp.float32) / ROPE_AXIS_DIM)
    return row[:, None] * inv_freq, col[:, None] * inv_freq


def rotate_axis(x, ang):
    cos = jnp.cos(ang)[None, :, None, :].astype(x.dtype)
    sin = jnp.sin(ang)[None, :, None, :].astype(x.dtype)
    x1, x2 = jnp.split(x, 2, axis=-1)
    return jnp.concatenate([x1 * cos - x2 * sin, x1 * sin + x2 * cos], axis=-1)


def apply_axial_rope(x, ang_row, ang_col):
    return jnp.concatenate([rotate_axis(x[..., :ROPE_AXIS_DIM], ang_row),
                            rotate_axis(x[..., ROPE_AXIS_DIM:], ang_col)], axis=-1)


def attend(q, k, v):
    s = jnp.einsum('bqgrd,bkgd->bgrqk', q.astype(jnp.float32), k.astype(jnp.float32)) * (HEAD_DIM ** -0.5)
    p = jax.nn.softmax(s, axis=-1)
    return jnp.einsum('bgrqk,bkgd->bqgrd', p, v.astype(jnp.float32)).astype(v.dtype)


def latent_attention(q, k_all, v_all):
    b, s = q.shape[:2]
    nblk = s // Q_BLOCK
    qb = q.reshape(b, nblk, Q_BLOCK, N_KV_HEADS, Q_PER_KV, HEAD_DIM).swapaxes(0, 1)
    out = lax.map(lambda qi: attend(qi, k_all, v_all), qb)
    return out.swapaxes(0, 1).reshape(b, s, ATTN_WIDTH)


def depthwise_conv(x, w, bias):
    y = lax.conv_general_dilated(x, w[:, None, :].astype(x.dtype), window_strides=(1,),
                                 padding=[(D_CONV // 2, D_CONV // 2)],
                                 dimension_numbers=('NWC', 'WIO', 'NWC'),
                                 feature_group_count=x.shape[-1])
    return jax.nn.silu(y + bias)


def segsum(a):
    t = a.shape[-1]
    a_rep = jnp.broadcast_to(a[..., :, None], a.shape + (t,))
    strict = jnp.tril(jnp.ones((t, t), dtype=bool), -1)
    s = jnp.cumsum(jnp.where(strict, a_rep, 0.0), axis=-2)
    return jnp.where(jnp.tril(jnp.ones((t, t), dtype=bool)), s, -jnp.inf)


def _to_chunks(X, dA, Bm):
    b, L = X.shape[:2]
    nc = L // SSD_CHUNK
    Xc = X.reshape(b, nc, SSD_CHUNK, N_SSD_GROUPS, HEADS_PER_GROUP, SSD_HEAD_DIM)
    Ac = dA.reshape(b, nc, SSD_CHUNK, N_SSD_GROUPS, HEADS_PER_GROUP).transpose(0, 3, 4, 1, 2)
    Bc = Bm.reshape(b, nc, SSD_CHUNK, N_SSD_GROUPS, D_STATE)
    return Xc, Ac, Bc


def _chunk_boundary_states(Xc, A_cs, Bc, h0):
    decay_to_end = jnp.exp(A_cs[..., -1:] - A_cs)
    local = jnp.einsum('bclgn,bgrcl,bclgrp->bcgrpn', Bc, decay_to_end, Xc)
    local = jnp.concatenate([h0[:, None], local], axis=1)
    totals = jnp.pad(A_cs[..., -1], ((0, 0), (0, 0), (0, 0), (1, 0)))
    decay_chunk = jnp.exp(segsum(totals))
    states = jnp.einsum('bgrzc,bcgrpn->bzgrpn', decay_chunk, local)
    return states[:, :-1], states[:, -1]


def ssd_scan(X, dA, Bm, Cm, h0):
    b, L = X.shape[:2]
    Xc, Ac, Bc = _to_chunks(X, dA, Bm)
    Cc = Cm.reshape(Bc.shape)
    A_cs = jnp.cumsum(Ac, axis=-1)
    cb = jnp.einsum('bclgn,bcsgn->bgcls', Cc, Bc)
    y_diag = jnp.einsum('bgrcls,bcsgrp->bclgrp', cb[:, :, None] * jnp.exp(segsum(Ac)), Xc)
    states_in, h_final = _chunk_boundary_states(Xc, A_cs, Bc, h0)
    y_off = jnp.einsum('bclgn,bcgrpn,bgrcl->bclgrp', Cc, states_in, jnp.exp(A_cs))
    return (y_diag + y_off).reshape(b, L, N_SSD_HEADS, SSD_HEAD_DIM), h_final


def ssd_final_state(X, dA, Bm, h0):
    Xc, Ac, Bc = _to_chunks(X, dA, Bm)
    return _chunk_boundary_states(Xc, jnp.cumsum(Ac, axis=-1), Bc, h0)[1]


def ssd_prep(xbc, dt, a_log, dt_bias):
    b, L = xbc.shape[:2]
    xs, bm, cm = jnp.split(xbc.astype(jnp.float32), [SSD_WIDTH, SSD_WIDTH + N_SSD_GROUPS * D_STATE], axis=-1)
    xs = heads(xs, N_SSD_HEADS, SSD_HEAD_DIM)
    bm = heads(bm, N_SSD_GROUPS, D_STATE)
    cm = heads(cm, N_SSD_GROUPS, D_STATE)
    dtp = jax.nn.softplus(dt.astype(jnp.float32).reshape(b, L, 2, N_SSD_HEADS) + dt_bias.astype(jnp.float32))
    dA = dtp * (-jnp.exp(a_log.astype(jnp.float32)))
    return xs, bm, cm, dtp, dA


def _direction(xs, bm, cm, dtp, dA, d):
    X, A = xs * dtp[:, :, d, :, None], dA[:, :, d]
    if d == 1:
        return jnp.flip(X, 1), jnp.flip(A, 1), jnp.flip(bm, 1), jnp.flip(cm, 1)
    return X, A, bm, cm


def _even_out(attn, g, y, xs, z, d_skip, ssd_norm, w_out):
    b, L = g.shape[:2]
    a = attn.reshape(b, L, ATTN_WIDTH) * jax.nn.silu(g)
    y = (y + xs * d_skip.astype(jnp.float32)[:, None]).reshape(b, L, SSD_WIDTH) * jax.nn.silu(z.astype(jnp.float32))
    yg = y.reshape(b, L, N_SSD_GROUPS, SSD_WIDTH // N_SSD_GROUPS)
    yg = yg * lax.rsqrt(jnp.mean(yg * yg, axis=-1, keepdims=True) + RMS_EPS)
    y = (yg.reshape(b, L, SSD_WIDTH) * ssd_norm.astype(jnp.float32)).astype(a.dtype)
    return jnp.concatenate([a, y], axis=-1) @ w_out


def even_layer(x, ctx, c, c_ctx, ang_row, ang_col, norm_w, ada_w, ada_b, w_in, q_norm, k_norm,
               conv_w, conv_b, a_log, dt_bias, d_skip, ssd_norm, w_out, update_ctx):
    b = x.shape[0]
    hx, gate_x = modulate(x, norm_w, ada_mod(c, ada_w, ada_b))
    hc, gate_c = modulate(ctx, norm_w, ada_mod(c_ctx[None], ada_w, ada_b))
    zero_state = jnp.zeros((b, N_SSD_GROUPS, HEADS_PER_GROUP, SSD_HEAD_DIM, D_STATE), jnp.float32)

    if update_ctx:
        qc, gc, zc, kc, vc, xbc_c, dt_c = jnp.split(hc @ w_in, EV_SPLITS, axis=-1)
    else:
        kc, vc, xbc_c, dt_c = jnp.split(hc @ w_in[:, CTX_COL0:], CTX_SPLITS, axis=-1)
    kc = rms_norm(heads(kc, N_KV_HEADS, HEAD_DIM), k_norm)
    vc = heads(vc, N_KV_HEADS, HEAD_DIM)
    xs_c, bm_c, cm_c, dtp_c, dA_c = ssd_prep(depthwise_conv(xbc_c, conv_w, conv_b), dt_c, a_log, dt_bias)
    dirs_c = [_direction(xs_c, bm_c, cm_c, dtp_c, dA_c, d) for d in (0, 1)]
    if update_ctx:
        yf_c, hf_c = ssd_scan(*dirs_c[0], zero_state)
        yb_c, hb_c = ssd_scan(*dirs_c[1], zero_state)
        qc = rms_norm(heads(qc, N_Q_HEADS, HEAD_DIM), q_norm)
        attn_c = attend(qc.reshape(b, qc.shape[1], N_KV_HEADS, Q_PER_KV, HEAD_DIM), kc, vc)
        out_c = _even_out(attn_c, gc, yf_c + jnp.flip(yb_c, 1), xs_c, zc, d_skip, ssd_norm, w_out)
        ctx = ctx + gate_c * out_c
    else:
        hf_c = ssd_final_state(dirs_c[0][0], dirs_c[0][1], dirs_c[0][2], zero_state)
        hb_c = ssd_final_state(dirs_c[1][0], dirs_c[1][1], dirs_c[1][2], zero_state)

    n_lat = x.shape[1]
    q, g, z, k, v, xbc, dt = jnp.split(hx @ w_in, EV_SPLITS, axis=-1)
    q = apply_axial_rope(rms_norm(heads(q, N_Q_HEADS, HEAD_DIM), q_norm), ang_row, ang_col)
    k = apply_axial_rope(rms_norm(heads(k, N_KV_HEADS, HEAD_DIM), k_norm), ang_row, ang_col)
    v = heads(v, N_KV_HEADS, HEAD_DIM)
    attn = latent_attention(q.reshape(b, n_lat, N_KV_HEADS, Q_PER_KV, HEAD_DIM),
                            jnp.concatenate([kc, k], axis=1), jnp.concatenate([vc, v], axis=1))
    xs, bm, cm, dtp, dA = ssd_prep(depthwise_conv(xbc, conv_w, conv_b), dt, a_log, dt_bias)
    y_f, _ = ssd_scan(*_direction(xs, bm, cm, dtp, dA, 0), hf_c)
    y_b, _ = ssd_scan(*_direction(xs, bm, cm, dtp, dA, 1), hb_c)
    out = _even_out(attn, g, y_f + jnp.flip(y_b, 1), xs, z, d_skip, ssd_norm, w_out)
    return x + gate_x * out, ctx


def fourier_branch(h, w_in, w_out):
    u, z = jnp.split(h @ w_in, 2, axis=-1)
    b, L = u.shape[:2]
    ug = u.astype(jnp.float32).reshape(b, L, N_FOURIER_GROUPS, FOURIER_GROUP)
    f = jnp.fft.fft2(ug, axes=(1, 3), norm='ortho').real.reshape(b, L, FOURIER_WIDTH)
    return (f.astype(h.dtype) * jax.nn.silu(z)) @ w_out


def odd_layer(x, ctx, c, c_ctx, norm_w, ada_w, ada_b, w_in, w_out, update_ctx):
    hx, gate_x = modulate(x, norm_w, ada_mod(c, ada_w, ada_b))
    x = x + gate_x * fourier_branch(hx, w_in, w_out)
    if update_ctx:
        hc, gate_c = modulate(ctx, norm_w, ada_mod(c_ctx[None], ada_w, ada_b))
        ctx = ctx + gate_c * fourier_branch(hc, w_in, w_out)
    return x, ctx


def setup_inputs(seed: int = 0) -> dict:
    key = jax.random.key(seed)
    ks = jax.random.split(key, 24)
    f32 = jnp.float32

    def nrm(k, shape, fan_in):
        return jax.random.normal(k, shape, f32) * fan_in ** -0.5

    def gain(k, shape):
        return 1.0 + 0.1 * jax.random.normal(k, shape, f32)

    dt0 = jnp.exp(jax.random.uniform(ks[12], (N_EVEN, 2, N_SSD_HEADS), f32) * (math.log(0.1) - math.log(0.001)) + math.log(0.001))
    return {
        'x': jax.random.normal(ks[0], (BATCH, SEQ, D_MODEL), f32),
        'c': jax.random.normal(ks[1], (BATCH, D_MODEL), f32),
        'ctx': jax.random.normal(ks[2], (BATCH, CTX_LEN, D_MODEL), f32),
        'c_ctx': jax.random.normal(ks[3], (D_MODEL,), f32),
        'ev_norm_w': gain(ks[4], (N_EVEN, D_MODEL)),
        'ev_ada_w': nrm(ks[5], (N_EVEN, D_MODEL, 3 * D_MODEL), D_MODEL),
        'ev_ada_b': 0.02 * jax.random.normal(ks[6], (N_EVEN, 3 * D_MODEL), f32),
        'ev_w_in': nrm(ks[7], (N_EVEN, D_MODEL, EV_IN_COLS), D_MODEL),
        'ev_q_norm': gain(ks[8], (N_EVEN, HEAD_DIM)),
        'ev_k_norm': gain(ks[9], (N_EVEN, HEAD_DIM)),
        'ev_conv_w': nrm(ks[10], (N_EVEN, D_CONV, CONV_CH), D_CONV),
        'ev_conv_b': 0.02 * jax.random.normal(ks[11], (N_EVEN, CONV_CH), f32),
        'ev_a_log': jnp.log(jax.random.uniform(ks[13], (N_EVEN, 2, N_SSD_HEADS), f32, 1.0, 16.0)),
        'ev_dt_bias': dt0 + jnp.log(-jnp.expm1(-dt0)),
        'ev_d_skip': gain(ks[14], (N_EVEN, N_SSD_HEADS)),
        'ev_ssd_norm': gain(ks[15], (N_EVEN, SSD_WIDTH)),
        'ev_w_out': nrm(ks[16], (N_EVEN, MIX_WIDTH, D_MODEL), MIX_WIDTH),
        'od_norm_w': gain(ks[17], (N_ODD, D_MODEL)),
        'od_ada_w': nrm(ks[18], (N_ODD, D_MODEL, 3 * D_MODEL), D_MODEL),
        'od_ada_b': 0.02 * jax.random.normal(ks[19], (N_ODD, 3 * D_MODEL), f32),
        'od_w_in': nrm(ks[20], (N_ODD, D_MODEL, 2 * FOURIER_WIDTH), D_MODEL),
        'od_w_out': nrm(ks[21], (N_ODD, FOURIER_WIDTH, D_MODEL), FOURIER_WIDTH),
    }


def reference(x, c, ctx, c_ctx, ev_norm_w, ev_ada_w, ev_ada_b, ev_w_in, ev_q_norm, ev_k_norm,
              ev_conv_w, ev_conv_b, ev_a_log, ev_dt_bias, ev_d_skip, ev_ssd_norm, ev_w_out,
              od_norm_w, od_ada_w, od_ada_b, od_w_in, od_w_out):
    rows = x.shape[1] // GRID_W
    ang_row, ang_col = axial_rope_angles(rows)
    for i in range(DEPTH):
        j = i // 2
        ctx_read_later = any(l % 2 == 0 for l in range(i + 1, DEPTH))
        if i % 2 == 0:
            x, ctx = even_layer(x, ctx, c, c_ctx, ang_row, ang_col, ev_norm_w[j], ev_ada_w[j], ev_ada_b[j],
                                ev_w_in[j], ev_q_norm[j], ev_k_norm[j], ev_conv_w[j], ev_conv_b[j],
                                ev_a_log[j], ev_dt_bias[j], ev_d_skip[j], ev_ssd_norm[j], ev_w_out[j],
                                ctx_read_later)
        else:
            x, ctx = odd_layer(x, ctx, c, c_ctx, od_norm_w[j], od_ada_w[j], od_ada_b[j],
                               od_w_in[j], od_w_out[j], ctx_read_later)
    return x
```

```python
import functools
import math

import numpy as np
import jax
import jax.numpy as jnp
from jax import lax
from jax.experimental import pallas as pl
from jax.experimental.pallas import tpu as pltpu

F32 = jnp.float32
BF16 = jnp.bfloat16

GRID_W = 64
HEAD_DIM = 128
Q_PER_KV = 4
ROPE_THETA = 10000.0
ROPE_AXIS_DIM = HEAD_DIM // 2
SSD_HEAD_DIM = 64
N_SSD_GROUPS = 4
D_STATE = 128
D_CONV = 5
SSD_CHUNK = 128
N_FOURIER_GROUPS = 8
RMS_EPS = 1e-6

LANES = 128
SUBLANES = 8
VMEM_LIMIT_BYTES = 56 * 1024 * 1024

CONV_HALO = SUBLANES
LOG2E = 1.4426950408889634


def _cparams(*sem):
    return pltpu.CompilerParams(dimension_semantics=sem, vmem_limit_bytes=VMEM_LIMIT_BYTES)


def _tile(n, pref):
    t = min(n, pref)
    while n % t:
        t //= 2
    return t


def _silu(v):
    return v * jax.nn.sigmoid(v)


def _ada_kernel(c_ref, w_ref, b_ref, o_ref):
    c = c_ref[...]
    s = _silu(c).astype(BF16)
    o_ref[...] = jnp.dot(s, w_ref[...].astype(BF16), preferred_element_type=F32) + b_ref[...]


def _ada(cond, w, b):
    r, d = cond.shape
    n = w.shape[1]
    tn = _tile(n, 1024)
    return pl.pallas_call(
        _ada_kernel,
        grid=(n // tn,),
        in_specs=[pl.BlockSpec((r, d), lambda j: (0, 0)),
                  pl.BlockSpec((d, tn), lambda j: (0, j)),
                  pl.BlockSpec((1, tn), lambda j: (0, j))],
        out_specs=pl.BlockSpec((r, tn), lambda j: (0, j)),
        out_shape=jax.ShapeDtypeStruct((r, n), F32),
        compiler_params=_cparams("parallel"),
        name="ada_mod",
    )(cond, w, b.reshape(1, n))


def _inproj_kernel(x_ref, sh_ref, sc_ref, nw_ref, w_ref, *rest, has_dt, sub):
    if has_dt:
        wdt_ref, o_ref, dt_ref, h_ref = rest
    else:
        o_ref, h_ref = rest
    tm = x_ref.shape[0]

    @pl.when(pl.program_id(1) == 0)
    def _():
        nw = nw_ref[...]
        sc = 1.0 + sc_ref[...]
        sh = sh_ref[...]

        def body(s, carry):
            r = pl.multiple_of(s * sub, sub)
            x = x_ref[pl.ds(r, sub), :]
            y = x * lax.rsqrt(jnp.mean(x * x, axis=-1, keepdims=True) + RMS_EPS) * nw
            h_ref[pl.ds(r, sub), :] = (y * sc + sh).astype(BF16)
            return carry

        lax.fori_loop(0, tm // sub, body, 0)
        if has_dt:
            dt_ref[...] = jnp.dot(h_ref[...], wdt_ref[...], preferred_element_type=F32)

    o_ref[...] = jnp.dot(h_ref[...], w_ref[...], preferred_element_type=F32).astype(o_ref.dtype)


def _inproj(x2d, mod, norm_w, w, wdt, *, rows_per_mod, mod_row0, col0, ncols):
    m, d = x2d.shape
    tm = _tile(rows_per_mod, 1024)
    tn = _tile(math.gcd(ncols, col0) if col0 else ncols, 1024)
    cb0 = col0 // tn
    tiles_per_mod = rows_per_mod // tm
    has_dt = wdt is not None

    def modrow(i):
        return mod_row0 + i // tiles_per_mod

    in_specs = [pl.BlockSpec((tm, d), lambda i, j: (i, 0)),
                pl.BlockSpec((None, 1, d), lambda i, j: (3 * modrow(i), 0, 0)),
                pl.BlockSpec((None, 1, d), lambda i, j: (3 * modrow(i) + 1, 0, 0)),
                pl.BlockSpec((1, d), lambda i, j: (0, 0)),
                pl.BlockSpec((d, tn), lambda i, j: (0, cb0 + j))]
    args = [x2d, mod, mod, norm_w.reshape(1, d), w]
    out_specs = [pl.BlockSpec((tm, tn), lambda i, j: (i, j))]
    out_shape = [jax.ShapeDtypeStruct((m, ncols), BF16)]
    if has_dt:
        ndt = wdt.shape[1]
        in_specs.append(pl.BlockSpec((d, ndt), lambda i, j: (0, 0)))
        args.append(wdt)
        out_specs.append(pl.BlockSpec((tm, ndt), lambda i, j: (i, 0)))
        out_shape.append(jax.ShapeDtypeStruct((m, ndt), F32))
    res = pl.pallas_call(
        functools.partial(_inproj_kernel, has_dt=has_dt, sub=min(32, tm)),
        grid=(m // tm, ncols // tn),
        in_specs=in_specs,
        out_specs=out_specs,
        out_shape=out_shape,
        scratch_shapes=[pltpu.VMEM((tm, d), BF16)],
        compiler_params=_cparams("parallel", "arbitrary"),
        name="mod_inproj",
    )(*args)
    return (res[0], res[1]) if has_dt else (res[0], None)


def _rms_rows(v, gain):
    return v * lax.rsqrt(jnp.mean(v * v, axis=-1, keepdims=True) + RMS_EPS) * gain


def _rope(v, cos, sin, first_half):
    h = ROPE_AXIS_DIM // 2
    partner = jnp.where(first_half, pltpu.roll(v, HEAD_DIM - h, 1), pltpu.roll(v, h, 1))
    return v * cos + partner * sin


def _attn_kernel(q_ref, g_ref, kl_ref, vl_ref, kc_ref, vc_ref, qn_ref, kn_ref,
                 cq_ref, sq_ref, ck_ref, sk_ref, o_ref,
                 k_scr, v_scr, s_scr, m_scr, l_scr, acc_scr, *, nrep, tk, n_ctx_blk):
    tq = q_ref.shape[0]
    nkb = k_scr.shape[0]
    lane = lax.broadcasted_iota(jnp.int32, (1, HEAD_DIM), 1)
    first_half = (lane % ROPE_AXIS_DIM) < (ROPE_AXIS_DIM // 2)

    @pl.when(pl.program_id(2) == 0)
    def _prep_keys():
        kn = kn_ref[...]
        for j in range(nkb):
            if j < n_ctx_blk:
                rows = pl.ds(j * tk, tk)
                k_scr[j] = _rms_rows(kc_ref[rows, :].astype(F32), kn).astype(BF16)
                v_scr[j] = vc_ref[rows, :]
            else:
                rows = pl.ds((j - n_ctx_blk) * tk, tk)
                k = _rms_rows(kl_ref[rows, :].astype(F32), kn)
                k_scr[j] = _rope(k, ck_ref[rows, :], sk_ref[rows, :], first_half).astype(BF16)
                v_scr[j] = vl_ref[rows, :]

    qscale = LOG2E * HEAD_DIM ** -0.5
    qn = qn_ref[...]
    cq = cq_ref[...]
    sq = sq_ref[...]
    parts = []
    for r in range(nrep):
        q = _rms_rows(q_ref[:, r * HEAD_DIM:(r + 1) * HEAD_DIM].astype(F32), qn)
        parts.append((_rope(q, cq, sq, first_half) * qscale).astype(BF16))
    qs = jnp.concatenate(parts, axis=0)

    m_scr[...] = jnp.full(m_scr.shape, -jnp.inf, F32)

    def pass1(j, carry):
        s = lax.dot_general(qs, k_scr[j], (((1,), (1,)), ((), ())), preferred_element_type=F32)
        s_scr[j] = s
        mj = s[:, :LANES]
        for c in range(1, tk // LANES):
            mj = jnp.maximum(mj, s[:, c * LANES:(c + 1) * LANES])
        m_scr[...] = jnp.maximum(m_scr[...], mj)
        return carry

    lax.fori_loop(0, nkb, pass1, 0)
    m = jnp.max(m_scr[...], axis=-1, keepdims=True)

    l_scr[...] = jnp.zeros(l_scr.shape, F32)
    acc_scr[...] = jnp.zeros(acc_scr.shape, F32)

    def pass2(j, carry):
        p = jnp.exp2(s_scr[j] - m)
        lj = p[:, :LANES]
        for c in range(1, tk // LANES):
            lj = lj + p[:, c * LANES:(c + 1) * LANES]
        l_scr[...] += lj
        acc_scr[...] += jnp.dot(p.astype(BF16), v_scr[j], preferred_element_type=F32)
        return carry

    lax.fori_loop(0, nkb, pass2, 0)
    l = jnp.sum(l_scr[...], axis=-1, keepdims=True)
    o = acc_scr[...] / l
    for r in range(nrep):
        g = g_ref[:, r * HEAD_DIM:(r + 1) * HEAD_DIM].astype(F32)
        o_ref[:, r * HEAD_DIM:(r + 1) * HEAD_DIM] = (o[r * tq:(r + 1) * tq] * _silu(g)).astype(o_ref.dtype)


def _attention(proj, projc, q_norm, k_norm, cos_t, sin_t, *, batch, seq, ctx_len, attn_width):
    n_kv = attn_width // HEAD_DIM // Q_PER_KV
    kv_width = n_kv * HEAD_DIM
    gw = Q_PER_KV * HEAD_DIM
    tq = _tile(seq, 256)
    tk = 256
    assert ctx_len % tk == 0 and seq % tk == 0
    nkb = (ctx_len + seq) // tk
    tpb = seq // tq
    k_cb = (2 * attn_width + _ssd_width(attn_width)) // HEAD_DIM
    v_cb = k_cb + n_kv
    return pl.pallas_call(
        functools.partial(_attn_kernel, nrep=Q_PER_KV, tk=tk, n_ctx_blk=ctx_len // tk),
        grid=(batch, n_kv, tpb),
        in_specs=[pl.BlockSpec((tq, gw), lambda b, h, t: (b * tpb + t, h)),
                  pl.BlockSpec((tq, gw), lambda b, h, t: (b * tpb + t, attn_width // gw + h)),
                  pl.BlockSpec((seq, HEAD_DIM), lambda b, h, t: (b, k_cb + h)),
                  pl.BlockSpec((seq, HEAD_DIM), lambda b, h, t: (b, v_cb + h)),
                  pl.BlockSpec((ctx_len, HEAD_DIM), lambda b, h, t: (b, h)),
                  pl.BlockSpec((ctx_len, HEAD_DIM), lambda b, h, t: (b, n_kv + h)),
                  pl.BlockSpec((1, HEAD_DIM), lambda b, h, t: (0, 0)),
                  pl.BlockSpec((1, HEAD_DIM), lambda b, h, t: (0, 0)),
                  pl.BlockSpec((tq, HEAD_DIM), lambda b, h, t: (t, 0)),
                  pl.BlockSpec((tq, HEAD_DIM), lambda b, h, t: (t, 0)),
                  pl.BlockSpec((seq, HEAD_DIM), lambda b, h, t: (0, 0)),
                  pl.BlockSpec((seq, HEAD_DIM), lambda b, h, t: (0, 0))],
        out_specs=pl.BlockSpec((tq, gw), lambda b, h, t: (b * tpb + t, h)),
        out_shape=jax.ShapeDtypeStruct((batch * seq, attn_width), BF16),
        scratch_shapes=[pltpu.VMEM((nkb, tk, HEAD_DIM), BF16),
                        pltpu.VMEM((nkb, tk, HEAD_DIM), BF16),
                        pltpu.VMEM((nkb, Q_PER_KV * tq, tk), F32),
                        pltpu.VMEM((Q_PER_KV * tq, LANES), F32),
                        pltpu.VMEM((Q_PER_KV * tq, LANES), F32),
                        pltpu.VMEM((Q_PER_KV * tq, HEAD_DIM), F32)],
        compiler_params=_cparams("parallel", "parallel", "arbitrary"),
        name="gqa_attention",
    )(proj, proj, proj, proj, projc, projc, q_norm.reshape(1, HEAD_DIM), k_norm.reshape(1, HEAD_DIM),
      cos_t, sin_t, cos_t, sin_t)


def _ssd_width(attn_width):
    return attn_width


def _conv_silu(src_ref, pad_ref, w_ref, b_ref, dst_ref, n_rows):
    ch = SSD_CHUNK
    cw = src_ref.shape[1]
    win_rows = ch + 2 * CONV_HALO
    pad_ref[pl.ds(0, CONV_HALO), :] = jnp.zeros((CONV_HALO, cw), F32)
    pad_ref[pl.ds(CONV_HALO + n_rows, CONV_HALO), :] = jnp.zeros((CONV_HALO, cw), F32)

    def copy(i, carry):
        r = pl.multiple_of(i * ch, ch)
        pad_ref[pl.ds(r + CONV_HALO, ch), :] = src_ref[pl.ds(r, ch), :].astype(F32)
        return carry

    lax.fori_loop(0, n_rows // ch, copy, 0)
    w = w_ref[...]
    bias = b_ref[...]

    def body(i, carry):
        r = pl.multiple_of(i * ch, ch)
        win = pad_ref[pl.ds(r, win_rows), :]
        acc = bias + win[CONV_HALO:CONV_HALO + ch] * w[D_CONV // 2:D_CONV // 2 + 1]
        for k in range(D_CONV):
            off = k - D_CONV // 2
            if off == 0:
                continue
            shifted = pltpu.roll(win, (-off) % win_rows, 0)
            acc = acc + shifted[CONV_HALO:CONV_HALO + ch] * w[k:k + 1]
        dst_ref[pl.ds(r, ch), :] = _silu(acc).astype(dst_ref.dtype)
        return carry

    lax.fori_loop(0, n_rows // ch, body, 0)


def _ssd_kernel(xl_ref, bl_ref, cl_ref, zl_ref, dtl_ref, xc_ref, bc_ref, dtc_ref,
                wx_ref, wb_ref, wc_ref, bx_ref, bb_ref, bcv_ref, alog_ref, dtb_ref, dsk_ref, nrm_ref,
                o_ref,
                padx, padb, xs_s, b_s, c_s, xsc_s, bcc_s, y_s, stf, stb, *, hpg):
    ch = SSD_CHUNK
    seq = xl_ref.shape[0]
    ctx_len = xc_ref.shape[0]
    gw = xl_ref.shape[1]
    npair = gw // LANES
    assert 2 * SSD_HEAD_DIM == LANES and hpg == 2 * npair

    _conv_silu(xl_ref, padx, wx_ref, bx_ref, xs_s, seq)
    _conv_silu(bl_ref, padb, wb_ref, bb_ref, b_s, seq)
    _conv_silu(cl_ref, padb, wc_ref, bcv_ref, c_s, seq)
    _conv_silu(xc_ref, padx, wx_ref, bx_ref, xsc_s, ctx_len)
    _conv_silu(bc_ref, padb, wb_ref, bb_ref, bcc_s, ctx_len)

    li = lax.broadcasted_iota(jnp.int32, (ch, ch), 0)
    si = lax.broadcasted_iota(jnp.int32, (ch, ch), 1)
    tri = (si <= li).astype(F32)
    lower = li >= si
    upper = si >= li
    lane = lax.broadcasted_iota(jnp.int32, (1, LANES), 1)
    head0 = lane < SSD_HEAD_DIM
    neg_a = -jnp.exp(alog_ref[...])
    dt_bias = dtb_ref[...]

    def chunk_terms(dt_ref, r):
        x = dt_ref[pl.ds(r, ch), :] + dt_bias
        dtp = jnp.maximum(x, 0.0) + jnp.log1p(jnp.exp(-jnp.abs(x)))
        da = dtp * neg_a
        p = jnp.dot(tri, da, preferred_element_type=F32, precision=lax.Precision.HIGHEST)
        e = p - da
        return dtp.T, p, p.T, e, e.T

    def pair_sel(a, b):
        return jnp.where(head0, a, b)

    def block_diag_rhs(x_pair):
        zero = jnp.zeros_like(x_pair)
        return jnp.concatenate([jnp.where(head0, x_pair, zero), jnp.where(head0, zero, x_pair)], axis=0)

    def state_update(st_ref, bt, x_ref, r, w_rows, tot):
        for q in range(npair):
            j0, j1 = 2 * q, 2 * q + 1
            lhs = jnp.concatenate([(bt * w_rows[j0]).astype(BF16), (bt * w_rows[j1]).astype(BF16)], axis=1)
            rhs = block_diag_rhs(x_ref[pl.ds(r, ch), q * LANES:(q + 1) * LANES])
            new = jnp.dot(lhs, rhs, preferred_element_type=F32)
            dec = pair_sel(jnp.exp(tot[j0]), jnp.exp(tot[j1]))
            cols = slice(q * LANES, (q + 1) * LANES)
            st_ref[:, cols] = st_ref[:, cols] * dec + new

    def fwd_state_terms(dtp_t, p, p_t):
        w_rows, tot = [], []
        for j in range(hpg):
            t = p[ch - 1:ch, j:j + 1]
            tot.append(t)
            w_rows.append(jnp.exp(t - p_t[j:j + 1, :]) * dtp_t[j:j + 1, :])
        return w_rows, tot

    def bwd_state_terms(dtp_t, p, e_t):
        w_rows, tot = [], []
        for j in range(hpg):
            tot.append(p[ch - 1:ch, hpg + j:hpg + j + 1])
            w_rows.append(jnp.exp(e_t[hpg + j:hpg + j + 1, :]) * dtp_t[hpg + j:hpg + j + 1, :])
        return w_rows, tot

    stf[...] = jnp.zeros(stf.shape, F32)
    stb[...] = jnp.zeros(stb.shape, F32)
    n_cc = ctx_len // ch

    def ctx_fwd(c, carry):
        r = pl.multiple_of(c * ch, ch)
        dtp_t, p, p_t, e, e_t = chunk_terms(dtc_ref, r)
        bt = bcc_s[pl.ds(r, ch), :].astype(F32).T
        w_rows, tot = fwd_state_terms(dtp_t, p, p_t)
        state_update(stf, bt, xsc_s, r, w_rows, tot)
        return carry

    def ctx_bwd(i, carry):
        r = pl.multiple_of((n_cc - 1 - i) * ch, ch)
        dtp_t, p, p_t, e, e_t = chunk_terms(dtc_ref, r)
        bt = bcc_s[pl.ds(r, ch), :].astype(F32).T
        w_rows, tot = bwd_state_terms(dtp_t, p, e_t)
        state_update(stb, bt, xsc_s, r, w_rows, tot)
        return carry

    lax.fori_loop(0, n_cc, ctx_fwd, 0)
    lax.fori_loop(0, n_cc, ctx_bwd, 0)

    n_lc = seq // ch

    def lat_fwd(c, carry):
        r = pl.multiple_of(c * ch, ch)
        dtp_t, p, p_t, e, e_t = chunk_terms(dtl_ref, r)
        bm = b_s[pl.ds(r, ch), :]
        cm = c_s[pl.ds(r, ch), :]
        cb = lax.dot_general(cm, bm, (((1,), (1,)), ((), ())), preferred_element_type=F32)
        y_off = jnp.dot(cm, stf[...].astype(BF16), preferred_element_type=F32)
        for q in range(npair):
            ms, ecols = [], []
            for j in (2 * q, 2 * q + 1):
                jb = hpg + j
                lf = jnp.where(lower, jnp.exp(p[:, j:j + 1] - p_t[j:j + 1, :]), 0.0)
                lb = jnp.where(upper, jnp.exp(e_t[jb:jb + 1, :] - e[:, jb:jb + 1]), 0.0)
                ms.append((cb * (lf * dtp_t[j:j + 1, :] + lb * dtp_t[jb:jb + 1, :])).astype(BF16))
                ecols.append(jnp.exp(p[:, j:j + 1]))
            cols = slice(q * LANES, (q + 1) * LANES)
            rhs = block_diag_rhs(xs_s[pl.ds(r, ch), cols])
            y = jnp.dot(jnp.concatenate(ms, axis=1), rhs, preferred_element_type=F32)
            y_s[pl.ds(r, ch), cols] = y + y_off[:, cols] * pair_sel(ecols[0], ecols[1])
        w_rows, tot = fwd_state_terms(dtp_t, p, p_t)
        state_update(stf, bm.astype(F32).T, xs_s, r, w_rows, tot)
        return carry

    lax.fori_loop(0, n_lc, lat_fwd, 0)

    dskip = dsk_ref[...]
    nrm = nrm_ref[...]

    def lat_bwd(i, carry):
        r = pl.multiple_of((n_lc - 1 - i) * ch, ch)
        dtp_t, p, p_t, e, e_t = chunk_terms(dtl_ref, r)
        bm = b_s[pl.ds(r, ch), :]
        cm = c_s[pl.ds(r, ch), :]
        y_off = jnp.dot(cm, stb[...].astype(BF16), preferred_element_type=F32)
        parts = []
        for q in range(npair):
            ecols = []
            for j in (2 * q, 2 * q + 1):
                jb = hpg + j
                ecols.append(jnp.exp(p[ch - 1:ch, jb:jb + 1] - e[:, jb:jb + 1]))
            cols = slice(q * LANES, (q + 1) * LANES)
            parts.append(y_s[pl.ds(r, ch), cols] + y_off[:, cols] * pair_sel(ecols[0], ecols[1]))
        w_rows, tot = bwd_state_terms(dtp_t, p, e_t)
        state_update(stb, bm.astype(F32).T, xs_s, r, w_rows, tot)
        y = jnp.concatenate(parts, axis=1) if npair > 1 else parts[0]
        y = (y + xs_s[pl.ds(r, ch), :].astype(F32) * dskip) * _silu(zl_ref[pl.ds(r, ch), :].astype(F32))
        y = y * lax.rsqrt(jnp.mean(y * y, axis=-1, keepdims=True) + RMS_EPS) * nrm
        o_ref[pl.ds(r, ch), :] = y.astype(o_ref.dtype)
        return carry

    lax.fori_loop(0, n_lc, lat_bwd, 0)


def _ssd(proj, dt, projc, dtc, conv_w, conv_b, alog_g, dtb_g, dskip_c, ssd_norm, *,
         batch, seq, ctx_len, attn_width):
    ssd_width = _ssd_width(attn_width)
    n_kv = attn_width // HEAD_DIM // Q_PER_KV
    kv_width = n_kv * HEAD_DIM
    gw = ssd_width // N_SSD_GROUPS
    hpg = gw // SSD_HEAD_DIM
    xs_off = 3 * attn_width + 2 * kv_width
    xs_off_c = 2 * kv_width
    assert xs_off % gw == 0 and xs_off_c % gw == 0 and D_STATE == LANES
    nb = N_SSD_GROUPS
    b_cb = (xs_off + ssd_width) // LANES
    c_cb = b_cb + nb
    b_cb_c = (xs_off_c + ssd_width) // LANES
    cw_b = ssd_width // LANES
    m3 = lambda f: (lambda b, g: f(b, g))
    in_specs = [
        pl.BlockSpec((seq, gw), lambda b, g: (b, xs_off // gw + g)),
        pl.BlockSpec((seq, LANES), lambda b, g: (b, b_cb + g)),
        pl.BlockSpec((seq, LANES), lambda b, g: (b, c_cb + g)),
        pl.BlockSpec((seq, gw), lambda b, g: (b, 2 * attn_width // gw + g)),
        pl.BlockSpec((seq, LANES), lambda b, g: (b, g)),
        pl.BlockSpec((ctx_len, gw), lambda b, g: (b, xs_off_c // gw + g)),
        pl.BlockSpec((ctx_len, LANES), lambda b, g: (b, b_cb_c + g)),
        pl.BlockSpec((ctx_len, LANES), lambda b, g: (b, g)),
        pl.BlockSpec((D_CONV, gw), lambda b, g: (0, g)),
        pl.BlockSpec((D_CONV, LANES), lambda b, g: (0, cw_b + g)),
        pl.BlockSpec((D_CONV, LANES), lambda b, g: (0, cw_b + nb + g)),
        pl.BlockSpec((1, gw), lambda b, g: (0, g)),
        pl.BlockSpec((1, LANES), lambda b, g: (0, cw_b + g)),
        pl.BlockSpec((1, LANES), lambda b, g: (0, cw_b + nb + g)),
        pl.BlockSpec((None, 1, LANES), lambda b, g: (g, 0, 0)),
        pl.BlockSpec((None, 1, LANES), lambda b, g: (g, 0, 0)),
        pl.BlockSpec((1, gw), lambda b, g: (0, g)),
        pl.BlockSpec((1, gw), lambda b, g: (0, g)),
    ]
    del m3
    pad_rows = max(seq, ctx_len) + 2 * CONV_HALO
    return pl.pallas_call(
        functools.partial(_ssd_kernel, hpg=hpg),
        grid=(batch, N_SSD_GROUPS),
        in_specs=in_specs,
        out_specs=pl.BlockSpec((seq, gw), lambda b, g: (b, g)),
        out_shape=jax.ShapeDtypeStruct((batch * seq, ssd_width), BF16),
        scratch_shapes=[pltpu.VMEM((pad_rows, gw), F32),
                        pltpu.VMEM((pad_rows, LANES), F32),
                        pltpu.VMEM((seq, gw), BF16),
                        pltpu.VMEM((seq, LANES), BF16),
                        pltpu.VMEM((seq, LANES), BF16),
                        pltpu.VMEM((ctx_len, gw), BF16),
                        pltpu.VMEM((ctx_len, LANES), BF16),
                        pltpu.VMEM((seq, gw), F32),
                        pltpu.VMEM((D_STATE, gw), F32),
                        pltpu.VMEM((D_STATE, gw), F32)],
        compiler_params=_cparams("parallel", "parallel"),
        name="ssd_bidir",
    )(proj, proj, proj, proj, dt, projc, projc, dtc,
      conv_w, conv_w, conv_w, conv_b, conv_b, conv_b, alog_g, dtb_g, dskip_c, ssd_norm)


def _outproj_kernel(a_ref, y_ref, w0_ref, w1_ref, x_ref, gate_ref, o_ref):
    acc = jnp.dot(a_ref[...], w0_ref[...], preferred_element_type=F32)
    acc = acc + jnp.dot(y_ref[...], w1_ref[...], preferred_element_type=F32)
    o_ref[...] = x_ref[...] + gate_ref[...] * acc


def _outproj(lhs0, cb0, lhs1, cb1, w, x2d, mod, *, rows_per_mod):
    m, d = x2d.shape
    kh = w.shape[0] // 2
    tm = _tile(rows_per_mod, 1024)
    tn = _tile(d, 512)
    tiles_per_mod = rows_per_mod // tm
    return pl.pallas_call(
        _outproj_kernel,
        grid=(m // tm, d // tn),
        in_specs=[pl.BlockSpec((tm, kh), lambda i, j: (i, cb0)),
                  pl.BlockSpec((tm, kh), lambda i, j: (i, cb1)),
                  pl.BlockSpec((kh, tn), lambda i, j: (0, j)),
                  pl.BlockSpec((kh, tn), lambda i, j: (1, j)),
                  pl.BlockSpec((tm, tn), lambda i, j: (i, j)),
                  pl.BlockSpec((None, 1, tn), lambda i, j: (3 * (i // tiles_per_mod) + 2, 0, j))],
        out_specs=pl.BlockSpec((tm, tn), lambda i, j: (i, j)),
        out_shape=jax.ShapeDtypeStruct((m, d), F32),
        compiler_params=_cparams("parallel", "arbitrary"),
        name="outproj_residual",
    )(lhs0, lhs1, w, w, x2d, mod)


def _posdft_kernel(w_ref, u_ref, o_ref):
    o_ref[...] = jnp.dot(w_ref[...], u_ref[...], preferred_element_type=F32).astype(o_ref.dtype)


def _posdft(wl, uz, *, batch, seq, width):
    tm = _tile(2 * seq, 1024)
    tn = _tile(width, 1024)
    nt = 2 * seq // tm
    return pl.pallas_call(
        _posdft_kernel,
        grid=(batch, width // tn, nt),
        in_specs=[pl.BlockSpec((tm, seq), lambda b, j, i: (i, 0)),
                  pl.BlockSpec((seq, tn), lambda b, j, i: (b, j))],
        out_specs=pl.BlockSpec((tm, tn), lambda b, j, i: (b * nt + i, j)),
        out_shape=jax.ShapeDtypeStruct((batch * 2 * seq, width), BF16),
        compiler_params=_cparams("parallel", "parallel", "arbitrary"),
        name="fourier_positions",
    )(wl, uz)


def _chdft_kernel(a_ref, b_ref, z_ref, wc_ref, o_ref):
    ab = jnp.concatenate([a_ref[...], b_ref[...]], axis=1)
    f = jnp.dot(ab, wc_ref[...], preferred_element_type=F32)
    o_ref[...] = (f * _silu(z_ref[...].astype(F32))).astype(o_ref.dtype)


def _chdft_gate(ab, uz, wc, *, batch, seq, width):
    gw = width // N_FOURIER_GROUPS
    tm = _tile(seq, 1024)
    tpb = seq // tm
    return pl.pallas_call(
        _chdft_kernel,
        grid=(batch * tpb, N_FOURIER_GROUPS),
        in_specs=[pl.BlockSpec((tm, gw), lambda i, g: ((i // tpb) * 2 * tpb + i % tpb, g)),
                  pl.BlockSpec((tm, gw), lambda i, g: ((i // tpb) * 2 * tpb + tpb + i % tpb, g)),
                  pl.BlockSpec((tm, gw), lambda i, g: (i, N_FOURIER_GROUPS + g)),
                  pl.BlockSpec((2 * gw, gw), lambda i, g: (0, 0))],
        out_specs=pl.BlockSpec((tm, gw), lambda i, g: (i, g)),
        out_shape=jax.ShapeDtypeStruct((batch * seq, width), BF16),
        compiler_params=_cparams("parallel", "parallel"),
        name="fourier_channels_gate",
    )(ab, ab, uz, wc)


def _dft_tables(seq, gw):
    def cs(n):
        k = np.arange(n, dtype=np.int64)
        ang = 2.0 * np.pi * ((k[:, None] * k[None, :]) % n).astype(np.float64) / n
        return np.cos(ang) / np.sqrt(n), np.sin(ang) / np.sqrt(n)
    cl, sl = cs(seq)
    cc, sc = cs(gw)
    wl = np.concatenate([cl, sl], axis=0).astype(np.float32)
    wc = np.concatenate([cc, -sc], axis=0).astype(np.float32)
    return jnp.asarray(wl).astype(BF16), jnp.asarray(wc).astype(BF16)


def _rope_tables(seq):
    pos = jnp.arange(seq)
    row = (pos // GRID_W).astype(F32)
    col = (pos % GRID_W).astype(F32)
    inv_freq = ROPE_THETA ** (-jnp.arange(0, ROPE_AXIS_DIM, 2, dtype=F32) / ROPE_AXIS_DIM)
    ar, ac = row[:, None] * inv_freq, col[:, None] * inv_freq
    cos_t = jnp.concatenate([jnp.cos(ar), jnp.cos(ar), jnp.cos(ac), jnp.cos(ac)], axis=-1)
    sin_t = jnp.concatenate([-jnp.sin(ar), jnp.sin(ar), -jnp.sin(ac), jnp.sin(ac)], axis=-1)
    return cos_t, sin_t


def _per_group_lanes(v, hpg):
    g = v.shape[1] // hpg
    t = v.reshape(2, g, hpg).transpose(1, 0, 2).reshape(g, 2 * hpg)
    return jnp.pad(t, ((0, 0), (0, LANES - 2 * hpg))).reshape(g, 1, LANES)


def kernel(x, c, ctx, c_ctx, ev_norm_w, ev_ada_w, ev_ada_b, ev_w_in, ev_q_norm, ev_k_norm, ev_conv_w,
           ev_conv_b, ev_a_log, ev_dt_bias, ev_d_skip, ev_ssd_norm, ev_w_out, od_norm_w, od_ada_w,
           od_ada_b, od_w_in, od_w_out):
    batch, seq, d = x.shape
    ctx_len = ctx.shape[1]
    mix = ev_w_out.shape[1]
    attn_width = mix // 2
    ssd_width = mix - attn_width
    n_ssd_heads = ssd_width // SSD_HEAD_DIM
    hpg = n_ssd_heads // N_SSD_GROUPS
    n_kv = attn_width // HEAD_DIM // Q_PER_KV
    kv_width = n_kv * HEAD_DIM
    conv_ch = ssd_width + 2 * N_SSD_GROUPS * D_STATE
    ctx_col0 = 2 * attn_width + ssd_width
    n_main = ctx_col0 + 2 * kv_width + conv_ch
    m = batch * seq
    x2d = x.reshape(m, d)
    ctx2d = ctx.reshape(batch * ctx_len, d)
    n_mod_rows = 2 * SUBLANES
    assert batch + 1 <= n_mod_rows

    w_in = ev_w_in[0]
    w_main = w_in[:, :n_main].astype(BF16)
    w_dt = w_in[:, n_main:].reshape(d, 2, N_SSD_GROUPS, hpg).transpose(0, 2, 1, 3)
    w_dt = jnp.pad(w_dt.reshape(d, N_SSD_GROUPS, 2 * hpg), ((0, 0), (0, 0), (0, LANES - 2 * hpg)))
    w_dt = w_dt.reshape(d, N_SSD_GROUPS * LANES).astype(BF16)
    cond = jnp.zeros((n_mod_rows, d), F32).at[:batch].set(c).at[batch].set(c_ctx)
    mod0 = _ada(cond, ev_ada_w[0], ev_ada_b[0]).reshape(3 * n_mod_rows, 1, d)

    proj, dt = _inproj(x2d, mod0, ev_norm_w[0], w_main, w_dt,
                       rows_per_mod=seq, mod_row0=0, col0=0, ncols=n_main)
    projc, dtc = _inproj(ctx2d, mod0, ev_norm_w[0], w_main, w_dt,
                         rows_per_mod=batch * ctx_len, mod_row0=batch, col0=ctx_col0, ncols=n_main - ctx_col0)

    cos_t, sin_t = _rope_tables(seq)
    attn = _attention(proj, projc, ev_q_norm[0], ev_k_norm[0], cos_t, sin_t,
                      batch=batch, seq=seq, ctx_len=ctx_len, attn_width=attn_width)
    yssd = _ssd(proj, dt, projc, dtc, ev_conv_w[0], ev_conv_b[0].reshape(1, conv_ch),
                _per_group_lanes(ev_a_log[0], hpg), _per_group_lanes(ev_dt_bias[0], hpg),
                jnp.repeat(ev_d_skip[0], SSD_HEAD_DIM).reshape(1, ssd_width),
                ev_ssd_norm[0].reshape(1, ssd_width),
                batch=batch, seq=seq, ctx_len=ctx_len, attn_width=attn_width)
    x1 = _outproj(attn, 0, yssd, 0, ev_w_out[0].astype(BF16), x2d, mod0, rows_per_mod=seq)

    fw = od_w_out.shape[1]
    cond1 = jnp.zeros((n_mod_rows, d), F32).at[:batch].set(c)
    mod1 = _ada(cond1, od_ada_w[0], od_ada_b[0]).reshape(3 * n_mod_rows, 1, d)
    uz, _ = _inproj(x1, mod1, od_norm_w[0], od_w_in[0].astype(BF16), None,
                    rows_per_mod=seq, mod_row0=0, col0=0, ncols=2 * fw)
    wl, wc = _dft_tables(seq, fw // N_FOURIER_GROUPS)
    ab = _posdft(wl, uz, batch=batch, seq=seq, width=fw)
    fg = _chdft_gate(ab, uz, wc, batch=batch, seq=seq, width=fw)
    x2 = _outproj(fg, 0, fg, 1, od_w_out[0].astype(BF16), x1, mod1, rows_per_mod=seq)
    return x2.reshape(batch, seq, d)
```

```python
import functools
import math

import numpy as np
import jax
import jax.numpy as jnp
from jax import lax
from jax.experimental import pallas as pl
from jax.experimental.pallas import tpu as pltpu

F32 = jnp.float32
BF16 = jnp.bfloat16

GRID_W = 64
HEAD_DIM = 128
Q_PER_KV = 4
ROPE_THETA = 10000.0
ROPE_AXIS_DIM = HEAD_DIM // 2
SSD_HEAD_DIM = 64
N_SSD_GROUPS = 4
D_STATE = 128
D_CONV = 5
SSD_CHUNK = 128
N_FOURIER_GROUPS = 8
RMS_EPS = 1e-6

LANES = 128
SUBLANES = 8
VMEM_LIMIT_BYTES = 56 * 1024 * 1024

CONV_HALO = SUBLANES
LOG2E = 1.4426950408889634


def _cparams(*sem):
    return pltpu.CompilerParams(dimension_semantics=sem, vmem_limit_bytes=VMEM_LIMIT_BYTES)


def _tile(n, pref):
    t = min(n, pref)
    while n % t:
        t //= 2
    return t


def _silu(v):
    return v * jax.nn.sigmoid(v)


def _ada_kernel(c_ref, w_ref, b_ref, o_ref):
    c = c_ref[...]
    s = _silu(c).astype(BF16)
    o_ref[...] = jnp.dot(s, w_ref[...].astype(BF16), preferred_element_type=F32) + b_ref[...]


def _ada(cond, w, b):
    r, d = cond.shape
    n = w.shape[1]
    tn = _tile(n, 1024)
    return pl.pallas_call(
        _ada_kernel,
        grid=(n // tn,),
        in_specs=[pl.BlockSpec((r, d), lambda j: (0, 0)),
                  pl.BlockSpec((d, tn), lambda j: (0, j)),
                  pl.BlockSpec((1, tn), lambda j: (0, j))],
        out_specs=pl.BlockSpec((r, tn), lambda j: (0, j)),
        out_shape=jax.ShapeDtypeStruct((r, n), F32),
        compiler_params=_cparams("parallel"),
        name="ada_mod",
    )(cond, w, b.reshape(1, n))


def _rope(v, cos, sin):
    return v * cos + pltpu.roll(v, HEAD_DIM // 2, 1) * sin


def _inproj_kernel(x_ref, sh_ref, sc_ref, nw_ref, w_ref, *rest, has_dt, has_heads, head_tiles, rope, sub):
    rest = list(rest)
    gain_ref, cos_ref, sin_ref = (rest.pop(0), rest.pop(0), rest.pop(0)) if has_heads else (None, None, None)
    wdt_ref = rest.pop(0) if has_dt else None
    o_ref = rest.pop(0)
    dt_ref = rest.pop(0) if has_dt else None
    (h_ref,) = rest
    tm = x_ref.shape[0]
    tn = o_ref.shape[1]

    @pl.when(pl.program_id(1) == 0)
    def _():
        nw = nw_ref[...]
        sc = 1.0 + sc_ref[...]
        sh = sh_ref[...]

        def body(s, carry):
            r = pl.multiple_of(s * sub, sub)
            x = x_ref[pl.ds(r, sub), :]
            y = x * lax.rsqrt(jnp.mean(x * x, axis=-1, keepdims=True) + RMS_EPS) * nw
            h_ref[pl.ds(r, sub), :] = (y * sc + sh).astype(BF16)
            return carry

        lax.fori_loop(0, tm // sub, body, 0)
        if has_dt:
            dt_ref[...] = jnp.dot(h_ref[...], wdt_ref[...], preferred_element_type=F32)

    def plain():
        o_ref[...] = jnp.dot(h_ref[...], w_ref[...], preferred_element_type=F32).astype(o_ref.dtype)

    def heads():
        pair = 2 * HEAD_DIM
        same_head = (lax.broadcasted_iota(jnp.int32, (pair, pair), 0) // HEAD_DIM
                     == lax.broadcasted_iota(jnp.int32, (pair, pair), 1) // HEAD_DIM)
        head_sum = jnp.where(same_head, 1.0, 0.0).astype(BF16)
        for cb in range(tn // pair):
            acc = jnp.dot(h_ref[...], w_ref[:, cb * pair:(cb + 1) * pair], preferred_element_type=F32)
            ss = jnp.dot((acc * acc).astype(BF16), head_sum, preferred_element_type=F32)
            y = acc * lax.rsqrt(ss * (1.0 / HEAD_DIM) + RMS_EPS) * gain_ref[:, cb * pair:(cb + 1) * pair]
            for hh in range(2):
                yh = y[:, hh * HEAD_DIM:(hh + 1) * HEAD_DIM]
                if rope:
                    yh = _rope(yh, cos_ref[...], sin_ref[...])
                cols = slice(cb * pair + hh * HEAD_DIM, cb * pair + (hh + 1) * HEAD_DIM)
                o_ref[:, cols] = yh.astype(o_ref.dtype)

    if has_heads:
        j = pl.program_id(1)
        is_head_tile = functools.reduce(jnp.logical_or, [j == t for t in head_tiles])
        pl.when(is_head_tile)(heads)
        pl.when(jnp.logical_not(is_head_tile))(plain)
    else:
        plain()


def _inproj(x2d, mod, norm_w, w, wdt, heads, *, rows_per_mod, mod_row0, col0, ncols):
    m, d = x2d.shape
    tm = _tile(rows_per_mod, 1024)
    has_dt = wdt is not None
    has_heads = heads is not None
    bounds = [ncols, col0] + ([c for r in heads[3] for c in r] if has_heads else [])
    tn = _tile(functools.reduce(math.gcd, bounds), 1024)
    cb0 = col0 // tn
    tiles_per_mod = rows_per_mod // tm

    def modrow(i):
        return mod_row0 + i // tiles_per_mod

    in_specs = [pl.BlockSpec((tm, d), lambda i, j: (i, 0)),
                pl.BlockSpec((None, 1, d), lambda i, j: (3 * modrow(i), 0, 0)),
                pl.BlockSpec((None, 1, d), lambda i, j: (3 * modrow(i) + 1, 0, 0)),
                pl.BlockSpec((1, d), lambda i, j: (0, 0)),
                pl.BlockSpec((d, tn), lambda i, j: (0, cb0 + j))]
    args = [x2d, mod, mod, norm_w.reshape(1, d), w]
    head_tiles, rope = (), False
    if has_heads:
        gain, cos_t, sin_t, ranges = heads
        rope = cos_t is not None
        for lo, hi in ranges:
            assert (lo - col0) % tn == 0 and (hi - col0) % tn == 0 and tn % (2 * HEAD_DIM) == 0
            head_tiles += tuple(range((lo - col0) // tn, (hi - col0) // tn))
        if not rope:
            cos_t = sin_t = jnp.zeros((rows_per_mod, HEAD_DIM), F32)
        in_specs += [pl.BlockSpec((1, tn), lambda i, j: (0, cb0 + j)),
                     pl.BlockSpec((tm, HEAD_DIM), lambda i, j: (i % tiles_per_mod, 0)),
                     pl.BlockSpec((tm, HEAD_DIM), lambda i, j: (i % tiles_per_mod, 0))]
        args += [gain, cos_t, sin_t]
    out_specs = [pl.BlockSpec((tm, tn), lambda i, j: (i, j))]
    out_shape = [jax.ShapeDtypeStruct((m, ncols), BF16)]
    if has_dt:
        ndt = wdt.shape[1]
        in_specs.append(pl.BlockSpec((d, ndt), lambda i, j: (0, 0)))
        args.append(wdt)
        out_specs.append(pl.BlockSpec((tm, ndt), lambda i, j: (i, 0)))
        out_shape.append(jax.ShapeDtypeStruct((m, ndt), F32))
    res = pl.pallas_call(
        functools.partial(_inproj_kernel, has_dt=has_dt, has_heads=has_heads, head_tiles=head_tiles,
                          rope=rope, sub=min(32, tm)),
        grid=(m // tm, ncols // tn),
        in_specs=in_specs,
        out_specs=out_specs,
        out_shape=out_shape,
        scratch_shapes=[pltpu.VMEM((tm, d), BF16)],
        compiler_params=_cparams("parallel", "arbitrary"),
        name="mod_inproj",
    )(*args)
    return (res[0], res[1]) if has_dt else (res[0], None)


def _attn_kernel(q_ref, g_ref, kl_ref, vl_ref, kc_ref, vc_ref, o_ref, s_scr, *, nrep, tk):
    tq = q_ref.shape[0]
    rows = nrep * tq
    qs = jnp.concatenate([q_ref[:, r * HEAD_DIM:(r + 1) * HEAD_DIM] for r in range(nrep)], axis=0)
    blocks, col = [], 0
    for k_ref, v_ref in ((kc_ref, vc_ref), (kl_ref, vl_ref)):
        n = k_ref.shape[0]
        for r0 in range(0, n, tk):
            blocks.append((k_ref, v_ref, r0, min(tk, n - r0), col))
            col += min(tk, n - r0)

    half = rows // 2
    halves = (slice(0, half), slice(half, rows))
    m = [None, None]
    for k_ref, _, r0, size, c0 in blocks:
        kb = k_ref[pl.ds(r0, size), :]
        for i, hs in enumerate(halves):
            s = lax.dot_general(qs[hs], kb, (((1,), (1,)), ((), ())), preferred_element_type=F32)
            s_scr[hs, c0:c0 + size] = s
            for c in range(size // LANES):
                sc = s[:, c * LANES:(c + 1) * LANES]
                m[i] = sc if m[i] is None else jnp.maximum(m[i], sc)
    mb = [jnp.broadcast_to(jnp.max(mi, axis=-1, keepdims=True), (half, LANES)) for mi in m]

    acc = [jnp.zeros((half, HEAD_DIM + LANES), F32) for _ in halves]
    for _, v_ref, r0, size, c0 in blocks:
        v_aug = jnp.concatenate([v_ref[pl.ds(r0, size), :], jnp.ones((size, LANES), BF16)], axis=1)
        for i, hs in enumerate(halves):
            p = jnp.concatenate([jnp.exp2(s_scr[hs, c0 + c * LANES:c0 + (c + 1) * LANES] - mb[i])
                                 for c in range(size // LANES)], axis=1).astype(BF16)
            acc[i] = acc[i] + jnp.dot(p, v_aug, preferred_element_type=F32)
    acc = jnp.concatenate(acc, axis=0)
    o = acc[:, :HEAD_DIM] / acc[:, HEAD_DIM:]
    for r in range(nrep):
        g = g_ref[:, r * HEAD_DIM:(r + 1) * HEAD_DIM].astype(F32)
        o_ref[:, r * HEAD_DIM:(r + 1) * HEAD_DIM] = (o[r * tq:(r + 1) * tq] * _silu(g)).astype(o_ref.dtype)


def _attention(proj, projc, *, batch, seq, ctx_len, attn_width):
    n_kv = attn_width // HEAD_DIM // Q_PER_KV
    gw = Q_PER_KV * HEAD_DIM
    tq = _tile(seq, 256)
    tk = 4 * LANES
    assert ctx_len % LANES == 0 and seq % LANES == 0
    tpb = seq // tq
    k_cb = (2 * attn_width + _ssd_width(attn_width)) // HEAD_DIM
    v_cb = k_cb + n_kv
    return pl.pallas_call(
        functools.partial(_attn_kernel, nrep=Q_PER_KV, tk=tk),
        grid=(batch, n_kv, tpb),
        in_specs=[pl.BlockSpec((tq, gw), lambda b, h, t: (b * tpb + t, h)),
                  pl.BlockSpec((tq, gw), lambda b, h, t: (b * tpb + t, attn_width // gw + h)),
                  pl.BlockSpec((seq, HEAD_DIM), lambda b, h, t: (b, k_cb + h)),
                  pl.BlockSpec((seq, HEAD_DIM), lambda b, h, t: (b, v_cb + h)),
                  pl.BlockSpec((ctx_len, HEAD_DIM), lambda b, h, t: (b, h)),
                  pl.BlockSpec((ctx_len, HEAD_DIM), lambda b, h, t: (b, n_kv + h))],
        out_specs=pl.BlockSpec((tq, gw), lambda b, h, t: (b * tpb + t, h)),
        out_shape=jax.ShapeDtypeStruct((batch * seq, attn_width), BF16),
        scratch_shapes=[pltpu.VMEM((Q_PER_KV * tq, ctx_len + seq), F32)],
        compiler_params=_cparams("parallel", "parallel", "arbitrary"),
        name="gqa_attention",
    )(proj, proj, proj, proj, projc, projc)


def _ssd_width(attn_width):
    return attn_width


def _conv_silu(src_ref, pad_ref, w_ref, b_ref, dst_ref, n_rows):
    ch = SSD_CHUNK
    cw = src_ref.shape[1]
    win_rows = ch + 2 * CONV_HALO
    pad_ref[pl.ds(0, CONV_HALO), :] = jnp.zeros((CONV_HALO, cw), F32)
    pad_ref[pl.ds(CONV_HALO + n_rows, CONV_HALO), :] = jnp.zeros((CONV_HALO, cw), F32)

    def copy(i, carry):
        r = pl.multiple_of(i * ch, ch)
        pad_ref[pl.ds(r + CONV_HALO, ch), :] = src_ref[pl.ds(r, ch), :].astype(F32)
        return carry

    lax.fori_loop(0, n_rows // ch, copy, 0)
    w = w_ref[...]
    bias = b_ref[...]

    def body(i, carry):
        r = pl.multiple_of(i * ch, ch)
        win = pad_ref[pl.ds(r, win_rows), :]
        acc = bias + win[CONV_HALO:CONV_HALO + ch] * w[D_CONV // 2:D_CONV // 2 + 1]
        for k in range(D_CONV):
            off = k - D_CONV // 2
            if off == 0:
                continue
            shifted = pltpu.roll(win, (-off) % win_rows, 0)
            acc = acc + shifted[CONV_HALO:CONV_HALO + ch] * w[k:k + 1]
        dst_ref[pl.ds(r, ch), :] = _silu(acc).astype(dst_ref.dtype)
        return carry

    lax.fori_loop(0, n_rows // ch, body, 0)


def _ssd_kernel(xl_ref, bl_ref, cl_ref, zl_ref, dtl_ref, xc_ref, bc_ref, dtc_ref,
                wx_ref, wb_ref, wc_ref, bx_ref, bb_ref, bcv_ref, alog_ref, dtb_ref, dsk_ref, nrm_ref,
                o_ref,
                padx, padb, xs_s, b_s, c_s, xsc_s, bcc_s, y_s, stf, stb, terms_s, bt_s, eb_s, *, hpg):
    ch = SSD_CHUNK
    seq = xl_ref.shape[0]
    ctx_len = xc_ref.shape[0]
    gw = xl_ref.shape[1]
    npair = gw // LANES
    assert 2 * SSD_HEAD_DIM == LANES and hpg == 2 * npair

    _conv_silu(xl_ref, padx, wx_ref, bx_ref, xs_s, seq)
    _conv_silu(bl_ref, padb, wb_ref, bb_ref, b_s, seq)
    _conv_silu(cl_ref, padb, wc_ref, bcv_ref, c_s, seq)
    _conv_silu(xc_ref, padx, wx_ref, bx_ref, xsc_s, ctx_len)
    _conv_silu(bc_ref, padb, wb_ref, bb_ref, bcc_s, ctx_len)

    li = lax.broadcasted_iota(jnp.int32, (ch, ch), 0)
    si = lax.broadcasted_iota(jnp.int32, (ch, ch), 1)
    tri_t = (li <= si).astype(F32)
    nr = alog_ref.shape[0]
    lower = li >= si
    diag = li == si
    xr = lax.broadcasted_iota(jnp.int32, (ch, gw), 0)
    xh = lax.broadcasted_iota(jnp.int32, (ch, gw), 1) // SSD_HEAD_DIM
    spread_f = jnp.where(xr == 2 * nr + xh, 1.0, 0.0).astype(BF16)
    spread_b = jnp.where(xr == 2 * nr + hpg + xh, 1.0, 0.0).astype(BF16)
    row_id = lax.broadcasted_iota(jnp.int32, (nr, ch), 0)
    lane = lax.broadcasted_iota(jnp.int32, (1, LANES), 1)
    head0 = lane < SSD_HEAD_DIM
    neg_a_t = -jnp.exp(alog_ref[...]) * LOG2E
    dt_bias_t = dtb_ref[...]

    def chunk_terms(dt_ref, r):
        x_t = dt_ref[pl.ds(r, ch), :].T[:nr] + dt_bias_t
        dtp_t = jnp.maximum(x_t, 0.0) + jnp.log1p(jnp.exp(-jnp.abs(x_t)))
        da_t = dtp_t * neg_a_t
        p_t = jnp.dot(da_t, tri_t, preferred_element_type=F32, precision=lax.Precision.HIGHEST)
        return dtp_t, p_t, p_t - da_t

    def columns(p_t, e_t):
        tot = p_t[:, ch - 1:ch]
        fac_t = jnp.exp2(jnp.where(row_id < hpg, p_t, tot - e_t))
        pad = jnp.zeros((ch - 3 * nr, ch), F32)
        return jnp.concatenate([p_t, e_t, fac_t, pad], axis=0).T

    def pair_sel(a, b):
        return jnp.where(head0, a, b)

    def block_diag_rhs(x_pair):
        zero = jnp.zeros_like(x_pair)
        return jnp.concatenate([jnp.where(head0, x_pair, zero), jnp.where(head0, zero, x_pair)], axis=0)

    def state_update(st_ref, bt, x_ref, r, w_rows, tot):
        for q in range(npair):
            j0, j1 = 2 * q, 2 * q + 1
            lhs = jnp.concatenate([(bt * w_rows[j0]).astype(BF16), (bt * w_rows[j1]).astype(BF16)], axis=1)
            rhs = block_diag_rhs(x_ref[pl.ds(r, ch), q * LANES:(q + 1) * LANES])
            new = jnp.dot(lhs, rhs, preferred_element_type=F32)
            dec = pair_sel(jnp.exp2(tot[j0]), jnp.exp2(tot[j1]))
            cols = slice(q * LANES, (q + 1) * LANES)
            st_ref[:, cols] = st_ref[:, cols] * dec + new

    def fwd_state_terms(dtp_t, p_t):
        w_rows, tot = [], []
        for j in range(hpg):
            t = p_t[j:j + 1, ch - 1:ch]
            tot.append(t)
            w_rows.append(jnp.exp2(t - p_t[j:j + 1, :]) * dtp_t[j:j + 1, :])
        return w_rows, tot

    def bwd_state_terms(dtp_t, p_t, e_t):
        w_rows, tot = [], []
        for j in range(hpg, 2 * hpg):
            tot.append(p_t[j:j + 1, ch - 1:ch])
            w_rows.append(jnp.exp2(e_t[j:j + 1, :]) * dtp_t[j:j + 1, :])
        return w_rows, tot

    stf[...] = jnp.zeros(stf.shape, F32)
    stb[...] = jnp.zeros(stb.shape, F32)
    n_cc = ctx_len // ch

    def ctx_fwd(c, carry):
        r = pl.multiple_of(c * ch, ch)
        dtp_t, p_t, e_t = chunk_terms(dtc_ref, r)
        bt = bcc_s[pl.ds(r, ch), :].astype(F32).T
        w_rows, tot = fwd_state_terms(dtp_t, p_t)
        state_update(stf, bt, xsc_s, r, w_rows, tot)
        return carry

    def ctx_bwd(i, carry):
        r = pl.multiple_of((n_cc - 1 - i) * ch, ch)
        dtp_t, p_t, e_t = chunk_terms(dtc_ref, r)
        bt = bcc_s[pl.ds(r, ch), :].astype(F32).T
        w_rows, tot = bwd_state_terms(dtp_t, p_t, e_t)
        state_update(stb, bt, xsc_s, r, w_rows, tot)
        return carry

    lax.fori_loop(0, n_cc, ctx_fwd, 0)
    lax.fori_loop(0, n_cc, ctx_bwd, 0)

    n_lc = seq // ch

    def lat_fwd(c, carry):
        r = pl.multiple_of(c * ch, ch)
        dtp_t, p_t, e_t = chunk_terms(dtl_ref, r)
        terms_s[c] = jnp.concatenate([dtp_t, p_t, e_t], axis=0)
        pe = columns(p_t, e_t)
        bm = b_s[pl.ds(r, ch), :]
        cm = c_s[pl.ds(r, ch), :]
        bt = bm.astype(F32).T
        bt_s[c] = bt
        cb = lax.dot_general(cm, bm, (((1,), (1,)), ((), ())), preferred_element_type=F32)
        y_off = jnp.dot(cm, stf[...].astype(BF16), preferred_element_type=F32)
        pe16 = pe.astype(BF16)
        y_off = y_off * jnp.dot(pe16, spread_f, preferred_element_type=F32)
        eb_s[pl.ds(r, ch), :] = jnp.dot(pe16, spread_b, preferred_element_type=F32)
        lg_t = jnp.log2(dtp_t)
        row_f = p_t - lg_t
        row_b = e_t + lg_t
        for q in range(npair):
            ms = []
            for j in (2 * q, 2 * q + 1):
                jb = hpg + j
                pcol = pe[:, j:j + 1]
                ecol = pe[:, nr + jb:nr + jb + 1]
                t = jnp.where(lower, pcol - row_f[j:j + 1, :], row_b[jb:jb + 1, :] - ecol)
                ms.append((cb * (jnp.exp2(t) + jnp.where(diag, dtp_t[jb:jb + 1, :], 0.0))).astype(BF16))
            cols = slice(q * LANES, (q + 1) * LANES)
            rhs = block_diag_rhs(xs_s[pl.ds(r, ch), cols])
            y = jnp.dot(jnp.concatenate(ms, axis=1), rhs, preferred_element_type=F32)
            y_s[pl.ds(r, ch), cols] = y + y_off[:, cols]
        w_rows, tot = fwd_state_terms(dtp_t, p_t)
        state_update(stf, bt, xs_s, r, w_rows, tot)
        return carry

    lax.fori_loop(0, n_lc, lat_fwd, 0, unroll=2)

    dskip = dsk_ref[...]
    nrm = nrm_ref[...]

    def lat_bwd(i, carry):
        c = n_lc - 1 - i
        r = pl.multiple_of(c * ch, ch)
        terms = terms_s[c]
        dtp_t, p_t, e_t = terms[:nr], terms[nr:2 * nr], terms[2 * nr:]
        y_off = jnp.dot(c_s[pl.ds(r, ch), :], stb[...].astype(BF16), preferred_element_type=F32)
        y = y_s[pl.ds(r, ch), :] + y_off * eb_s[pl.ds(r, ch), :]
        w_rows, tot = bwd_state_terms(dtp_t, p_t, e_t)
        state_update(stb, bt_s[c], xs_s, r, w_rows, tot)
        y = (y + xs_s[pl.ds(r, ch), :].astype(F32) * dskip) * _silu(zl_ref[pl.ds(r, ch), :].astype(F32))
        y = y * lax.rsqrt(jnp.mean(y * y, axis=-1, keepdims=True) + RMS_EPS) * nrm
        o_ref[pl.ds(r, ch), :] = y.astype(o_ref.dtype)
        return carry

    lax.fori_loop(0, n_lc, lat_bwd, 0)


def _ssd(proj, dt, projc, dtc, conv_w, conv_b, alog_g, dtb_g, dskip_c, ssd_norm, *,
         batch, seq, ctx_len, attn_width):
    ssd_width = _ssd_width(attn_width)
    n_kv = attn_width // HEAD_DIM // Q_PER_KV
    kv_width = n_kv * HEAD_DIM
    gw = ssd_width // N_SSD_GROUPS
    hpg = gw // SSD_HEAD_DIM
    xs_off = 3 * attn_width + 2 * kv_width
    xs_off_c = 2 * kv_width
    assert xs_off % gw == 0 and xs_off_c % gw == 0 and D_STATE == LANES
    nb = N_SSD_GROUPS
    b_cb = (xs_off + ssd_width) // LANES
    c_cb = b_cb + nb
    b_cb_c = (xs_off_c + ssd_width) // LANES
    cw_b = ssd_width // LANES
    nr = alog_g.shape[1]
    in_specs = [
        pl.BlockSpec((seq, gw), lambda b, g: (b, xs_off // gw + g)),
        pl.BlockSpec((seq, LANES), lambda b, g: (b, b_cb + g)),
        pl.BlockSpec((seq, LANES), lambda b, g: (b, c_cb + g)),
        pl.BlockSpec((seq, gw), lambda b, g: (b, 2 * attn_width // gw + g)),
        pl.BlockSpec((seq, LANES), lambda b, g: (b, g)),
        pl.BlockSpec((ctx_len, gw), lambda b, g: (b, xs_off_c // gw + g)),
        pl.BlockSpec((ctx_len, LANES), lambda b, g: (b, b_cb_c + g)),
        pl.BlockSpec((ctx_len, LANES), lambda b, g: (b, g)),
        pl.BlockSpec((D_CONV, gw), lambda b, g: (0, g)),
        pl.BlockSpec((D_CONV, LANES), lambda b, g: (0, cw_b + g)),
        pl.BlockSpec((D_CONV, LANES), lambda b, g: (0, cw_b + nb + g)),
        pl.BlockSpec((1, gw), lambda b, g: (0, g)),
        pl.BlockSpec((1, LANES), lambda b, g: (0, cw_b + g)),
        pl.BlockSpec((1, LANES), lambda b, g: (0, cw_b + nb + g)),
        pl.BlockSpec((None, nr, LANES), lambda b, g: (g, 0, 0)),
        pl.BlockSpec((None, nr, LANES), lambda b, g: (g, 0, 0)),
        pl.BlockSpec((1, gw), lambda b, g: (0, g)),
        pl.BlockSpec((1, gw), lambda b, g: (0, g)),
    ]
    pad_rows = max(seq, ctx_len) + 2 * CONV_HALO
    n_chunks = seq // SSD_CHUNK
    return pl.pallas_call(
        functools.partial(_ssd_kernel, hpg=hpg),
        grid=(batch, N_SSD_GROUPS),
        in_specs=in_specs,
        out_specs=pl.BlockSpec((seq, gw), lambda b, g: (b, g)),
        out_shape=jax.ShapeDtypeStruct((batch * seq, ssd_width), BF16),
        scratch_shapes=[pltpu.VMEM((pad_rows, gw), F32),
                        pltpu.VMEM((pad_rows, LANES), F32),
                        pltpu.VMEM((seq, gw), BF16),
                        pltpu.VMEM((seq, LANES), BF16),
                        pltpu.VMEM((seq, LANES), BF16),
                        pltpu.VMEM((ctx_len, gw), BF16),
                        pltpu.VMEM((ctx_len, LANES), BF16),
                        pltpu.VMEM((seq, gw), F32),
                        pltpu.VMEM((D_STATE, gw), F32),
                        pltpu.VMEM((D_STATE, gw), F32),
                        pltpu.VMEM((n_chunks, 3 * nr, SSD_CHUNK), F32),
                        pltpu.VMEM((n_chunks, D_STATE, SSD_CHUNK), F32),
                        pltpu.VMEM((seq, gw), F32)],
        compiler_params=_cparams("parallel", "parallel"),
        name="ssd_bidir",
    )(proj, proj, proj, proj, dt, projc, projc, dtc,
      conv_w, conv_w, conv_w, conv_b, conv_b, conv_b, alog_g, dtb_g, dskip_c, ssd_norm)


def _outproj_kernel(a_ref, y_ref, w0_ref, w1_ref, x_ref, gate_ref, o_ref):
    acc = jnp.dot(a_ref[...], w0_ref[...], preferred_element_type=F32)
    acc = acc + jnp.dot(y_ref[...], w1_ref[...], preferred_element_type=F32)
    o_ref[...] = x_ref[...] + gate_ref[...] * acc


def _outproj(lhs0, cb0, lhs1, cb1, w, x2d, mod, *, rows_per_mod):
    m, d = x2d.shape
    kh = w.shape[0] // 2
    tm = _tile(rows_per_mod, 1024)
    tn = _tile(d, 512)
    tiles_per_mod = rows_per_mod // tm
    return pl.pallas_call(
        _outproj_kernel,
        grid=(m // tm, d // tn),
        in_specs=[pl.BlockSpec((tm, kh), lambda i, j: (i, cb0)),
                  pl.BlockSpec((tm, kh), lambda i, j: (i, cb1)),
                  pl.BlockSpec((kh, tn), lambda i, j: (0, j)),
                  pl.BlockSpec((kh, tn), lambda i, j: (1, j)),
                  pl.BlockSpec((tm, tn), lambda i, j: (i, j)),
                  pl.BlockSpec((None, 1, tn), lambda i, j: (3 * (i // tiles_per_mod) + 2, 0, j))],
        out_specs=pl.BlockSpec((tm, tn), lambda i, j: (i, j)),
        out_shape=jax.ShapeDtypeStruct((m, d), F32),
        compiler_params=_cparams("parallel", "arbitrary"),
        name="outproj_residual",
    )(lhs0, lhs1, w, w, x2d, mod)


def _posdft_kernel(w_ref, u_ref, o_ref):
    o_ref[...] = jnp.dot(w_ref[...], u_ref[...], preferred_element_type=F32).astype(o_ref.dtype)


def _posdft(wl, uz, *, batch, seq, width):
    tm = _tile(2 * seq, 1024)
    tn = _tile(width, 1024)
    nt = 2 * seq // tm
    return pl.pallas_call(
        _posdft_kernel,
        grid=(batch, width // tn, nt),
        in_specs=[pl.BlockSpec((tm, seq), lambda b, j, i: (i, 0)),
                  pl.BlockSpec((seq, tn), lambda b, j, i: (b, j))],
        out_specs=pl.BlockSpec((tm, tn), lambda b, j, i: (b * nt + i, j)),
        out_shape=jax.ShapeDtypeStruct((batch * 2 * seq, width), BF16),
        compiler_params=_cparams("parallel", "parallel", "arbitrary"),
        name="fourier_positions",
    )(wl, uz)


def _chdft_kernel(a_ref, b_ref, z_ref, wc_ref, o_ref):
    ab = jnp.concatenate([a_ref[...], b_ref[...]], axis=1)
    f = jnp.dot(ab, wc_ref[...], preferred_element_type=F32)
    o_ref[...] = (f * _silu(z_ref[...].astype(F32))).astype(o_ref.dtype)


def _chdft_gate(ab, uz, wc, *, batch, seq, width):
    gw = width // N_FOURIER_GROUPS
    tm = _tile(seq, 1024)
    tpb = seq // tm
    return pl.pallas_call(
        _chdft_kernel,
        grid=(batch * tpb, N_FOURIER_GROUPS),
        in_specs=[pl.BlockSpec((tm, gw), lambda i, g: ((i // tpb) * 2 * tpb + i % tpb, g)),
                  pl.BlockSpec((tm, gw), lambda i, g: ((i // tpb) * 2 * tpb + tpb + i % tpb, g)),
                  pl.BlockSpec((tm, gw), lambda i, g: (i, N_FOURIER_GROUPS + g)),
                  pl.BlockSpec((2 * gw, gw), lambda i, g: (0, 0))],
        out_specs=pl.BlockSpec((tm, gw), lambda i, g: (i, g)),
        out_shape=jax.ShapeDtypeStruct((batch * seq, width), BF16),
        compiler_params=_cparams("parallel", "parallel"),
        name="fourier_channels_gate",
    )(ab, ab, uz, wc)


def _dft_tables(seq, gw):
    def cs(n):
        k = np.arange(n, dtype=np.int64)
        ang = 2.0 * np.pi * ((k[:, None] * k[None, :]) % n).astype(np.float64) / n
        return np.cos(ang) / np.sqrt(n), np.sin(ang) / np.sqrt(n)
    cl, sl = cs(seq)
    cc, sc = cs(gw)
    wl = np.concatenate([cl, sl], axis=0).astype(np.float32)
    wc = np.concatenate([cc, -sc], axis=0).astype(np.float32)
    return jnp.asarray(wl).astype(BF16), jnp.asarray(wc).astype(BF16)


def _reorder_head_dims(a):
    h = ROPE_AXIS_DIM // 2
    lead = a.shape[:-1]
    a = a.reshape(lead + (a.shape[-1] // HEAD_DIM, 2, 2, h))
    return jnp.swapaxes(a, -3, -2).reshape(lead + (-1,))


def _rope_tables(seq):
    pos = jnp.arange(seq)
    row = (pos // GRID_W).astype(F32)
    col = (pos % GRID_W).astype(F32)
    inv_freq = ROPE_THETA ** (-jnp.arange(0, ROPE_AXIS_DIM, 2, dtype=F32) / ROPE_AXIS_DIM)
    ar, ac = row[:, None] * inv_freq, col[:, None] * inv_freq
    cos_t = jnp.concatenate([jnp.cos(ar), jnp.cos(ac), jnp.cos(ar), jnp.cos(ac)], axis=-1)
    sin_t = jnp.concatenate([-jnp.sin(ar), -jnp.sin(ac), jnp.sin(ar), jnp.sin(ac)], axis=-1)
    return cos_t, sin_t


def _per_group_rows(v, hpg):
    g = v.shape[1] // hpg
    rows = -(-2 * hpg // SUBLANES) * SUBLANES
    t = v.reshape(2, g, hpg).transpose(1, 0, 2).reshape(g, 2 * hpg)
    t = jnp.pad(t, ((0, 0), (0, rows - 2 * hpg)))
    return jnp.broadcast_to(t[:, :, None], (g, rows, LANES))


def kernel(x, c, ctx, c_ctx, ev_norm_w, ev_ada_w, ev_ada_b, ev_w_in, ev_q_norm, ev_k_norm, ev_conv_w,
           ev_conv_b, ev_a_log, ev_dt_bias, ev_d_skip, ev_ssd_norm, ev_w_out, od_norm_w, od_ada_w,
           od_ada_b, od_w_in, od_w_out):
    batch, seq, d = x.shape
    ctx_len = ctx.shape[1]
    mix = ev_w_out.shape[1]
    attn_width = mix // 2
    ssd_width = mix - attn_width
    n_ssd_heads = ssd_width // SSD_HEAD_DIM
    hpg = n_ssd_heads // N_SSD_GROUPS
    n_kv = attn_width // HEAD_DIM // Q_PER_KV
    kv_width = n_kv * HEAD_DIM
    conv_ch = ssd_width + 2 * N_SSD_GROUPS * D_STATE
    ctx_col0 = 2 * attn_width + ssd_width
    n_main = ctx_col0 + 2 * kv_width + conv_ch
    m = batch * seq
    x2d = x.reshape(m, d)
    ctx2d = ctx.reshape(batch * ctx_len, d)
    n_mod_rows = 2 * SUBLANES
    assert batch + 1 <= n_mod_rows

    w_in = ev_w_in[0]
    n_q = attn_width // HEAD_DIM
    k_end = ctx_col0 + kv_width
    w_main = jnp.concatenate([_reorder_head_dims(w_in[:, :attn_width]), w_in[:, attn_width:ctx_col0],
                              _reorder_head_dims(w_in[:, ctx_col0:k_end]), w_in[:, k_end:n_main]],
                             axis=1).astype(BF16)
    qscale = LOG2E * HEAD_DIM ** -0.5
    gain = jnp.concatenate([jnp.tile(_reorder_head_dims(ev_q_norm[0]) * qscale, n_q),
                            jnp.zeros((ctx_col0 - attn_width,), F32),
                            jnp.tile(_reorder_head_dims(ev_k_norm[0]), n_kv),
                            jnp.zeros((n_main - k_end,), F32)]).reshape(1, n_main)
    cos_t, sin_t = _rope_tables(seq)
    qk_ranges = ((0, attn_width), (ctx_col0, ctx_col0 + kv_width))
    w_dt = w_in[:, n_main:].reshape(d, 2, N_SSD_GROUPS, hpg).transpose(0, 2, 1, 3)
    w_dt = jnp.pad(w_dt.reshape(d, N_SSD_GROUPS, 2 * hpg), ((0, 0), (0, 0), (0, LANES - 2 * hpg)))
    w_dt = w_dt.reshape(d, N_SSD_GROUPS * LANES).astype(BF16)
    cond = jnp.zeros((n_mod_rows, d), F32).at[:batch].set(c).at[batch].set(c_ctx)
    mod0 = _ada(cond, ev_ada_w[0], ev_ada_b[0]).reshape(3 * n_mod_rows, 1, d)

    proj, dt = _inproj(x2d, mod0, ev_norm_w[0], w_main, w_dt, (gain, cos_t, sin_t, qk_ranges),
                       rows_per_mod=seq, mod_row0=0, col0=0, ncols=n_main)
    projc, dtc = _inproj(ctx2d, mod0, ev_norm_w[0], w_main, w_dt, (gain, None, None, qk_ranges[1:]),
                         rows_per_mod=batch * ctx_len, mod_row0=batch, col0=ctx_col0, ncols=n_main - ctx_col0)

    attn = _attention(proj, projc, batch=batch, seq=seq, ctx_len=ctx_len, attn_width=attn_width)
    yssd = _ssd(proj, dt, projc, dtc, ev_conv_w[0], ev_conv_b[0].reshape(1, conv_ch),
                _per_group_rows(ev_a_log[0], hpg), _per_group_rows(ev_dt_bias[0], hpg),
                jnp.repeat(ev_d_skip[0], SSD_HEAD_DIM).reshape(1, ssd_width),
                ev_ssd_norm[0].reshape(1, ssd_width),
                batch=batch, seq=seq, ctx_len=ctx_len, attn_width=attn_width)
    x1 = _outproj(attn, 0, yssd, 0, ev_w_out[0].astype(BF16), x2d, mod0, rows_per_mod=seq)

    fw = od_w_out.shape[1]
    cond1 = jnp.zeros((n_mod_rows, d), F32).at[:batch].set(c)
    mod1 = _ada(cond1, od_ada_w[0], od_ada_b[0]).reshape(3 * n_mod_rows, 1, d)
    uz, _ = _inproj(x1, mod1, od_norm_w[0], od_w_in[0].astype(BF16), None, None,
                    rows_per_mod=seq, mod_row0=0, col0=0, ncols=2 * fw)
    wl, wc = _dft_tables(seq, fw // N_FOURIER_GROUPS)
    ab = _posdft(wl, uz, batch=batch, seq=seq, width=fw)
    fg = _chdft_gate(ab, uz, wc, batch=batch, seq=seq, width=fw)
    x2 = _outproj(fg, 0, fg, 1, od_w_out[0].astype(BF16), x1, mod1, rows_per_mod=seq)
    return x2.reshape(batch, seq, d)
```

```python
import functools
import math

import numpy as np
import jax
import jax.numpy as jnp
from jax import lax
from jax.experimental import pallas as pl
from jax.experimental.pallas import tpu as pltpu

F32 = jnp.float32
BF16 = jnp.bfloat16

GRID_W = 64
HEAD_DIM = 128
Q_PER_KV = 4
ROPE_THETA = 10000.0
ROPE_AXIS_DIM = HEAD_DIM // 2
SSD_HEAD_DIM = 64
N_SSD_GROUPS = 4
D_STATE = 128
D_CONV = 5
SSD_CHUNK = 128
N_FOURIER_GROUPS = 8
RMS_EPS = 1e-6

LANES = 128
SUBLANES = 8
VMEM_LIMIT_BYTES = 56 * 1024 * 1024

CONV_HALO = SUBLANES
LOG2E = 1.4426950408889634


def _cparams(*sem):
    return pltpu.CompilerParams(dimension_semantics=sem, vmem_limit_bytes=VMEM_LIMIT_BYTES)


def _tile(n, pref):
    t = min(n, pref)
    while n % t:
        t //= 2
    return t


def _silu(v):
    return v * jax.nn.sigmoid(v)


def _ada_kernel(c_ref, w_ref, b_ref, o_ref):
    c = c_ref[...]
    s = _silu(c).astype(BF16)
    o_ref[...] = jnp.dot(s, w_ref[...].astype(BF16), preferred_element_type=F32) + b_ref[...]


def _ada(cond, w, b):
    r, d = cond.shape
    n = w.shape[1]
    tn = _tile(n, 1024)
    return pl.pallas_call(
        _ada_kernel,
        grid=(n // tn,),
        in_specs=[pl.BlockSpec((r, d), lambda j: (0, 0)),
                  pl.BlockSpec((d, tn), lambda j: (0, j)),
                  pl.BlockSpec((1, tn), lambda j: (0, j))],
        out_specs=pl.BlockSpec((r, tn), lambda j: (0, j)),
        out_shape=jax.ShapeDtypeStruct((r, n), F32),
        compiler_params=_cparams("parallel"),
        name="ada_mod",
    )(cond, w, b.reshape(1, n))


def _inproj_kernel(x_ref, sh_ref, sc_ref, nw_ref, w_ref, *rest, has_dt, sub):
    if has_dt:
        wdt_ref, o_ref, dt_ref, h_ref = rest
    else:
        o_ref, h_ref = rest
    tm = x_ref.shape[0]

    @pl.when(pl.program_id(1) == 0)
    def _():
        nw = nw_ref[...]
        sc = 1.0 + sc_ref[...]
        sh = sh_ref[...]

        def body(s, carry):
            r = pl.multiple_of(s * sub, sub)
            x = x_ref[pl.ds(r, sub), :]
            y = x * lax.rsqrt(jnp.mean(x * x, axis=-1, keepdims=True) + RMS_EPS) * nw
            h_ref[pl.ds(r, sub), :] = (y * sc + sh).astype(BF16)
            return carry

        lax.fori_loop(0, tm // sub, body, 0, unroll=4)
        if has_dt:
            dt_ref[...] = jnp.dot(h_ref[...], wdt_ref[...], preferred_element_type=F32)

    o_ref[...] = jnp.dot(h_ref[...], w_ref[...], preferred_element_type=F32).astype(o_ref.dtype)


def _inproj(x2d, mod, norm_w, w, wdt, *, rows_per_mod, mod_row0, col0, ncols):
    m, d = x2d.shape
    tm = _tile(rows_per_mod, 1024)
    has_dt = wdt is not None
    tn = _tile(math.gcd(ncols, col0), 1024)
    cb0 = col0 // tn
    tiles_per_mod = rows_per_mod // tm

    def modrow(i):
        return mod_row0 + i // tiles_per_mod

    in_specs = [pl.BlockSpec((tm, d), lambda i, j: (i, 0)),
                pl.BlockSpec((None, 1, d), lambda i, j: (3 * modrow(i), 0, 0)),
                pl.BlockSpec((None, 1, d), lambda i, j: (3 * modrow(i) + 1, 0, 0)),
                pl.BlockSpec((1, d), lambda i, j: (0, 0)),
                pl.BlockSpec((d, tn), lambda i, j: (0, cb0 + j))]
    args = [x2d, mod, mod, norm_w.reshape(1, d), w]
    out_specs = [pl.BlockSpec((tm, tn), lambda i, j: (i, j))]
    out_shape = [jax.ShapeDtypeStruct((m, ncols), BF16)]
    if has_dt:
        ndt = wdt.shape[1]
        in_specs.append(pl.BlockSpec((d, ndt), lambda i, j: (0, 0)))
        args.append(wdt)
        out_specs.append(pl.BlockSpec((tm, ndt), lambda i, j: (i, 0)))
        out_shape.append(jax.ShapeDtypeStruct((m, ndt), F32))
    res = pl.pallas_call(
        functools.partial(_inproj_kernel, has_dt=has_dt, sub=min(32, tm)),
        grid=(m // tm, ncols // tn),
        in_specs=in_specs,
        out_specs=out_specs,
        out_shape=out_shape,
        scratch_shapes=[pltpu.VMEM((tm, d), BF16)],
        compiler_params=_cparams("parallel", "arbitrary"),
        name="mod_inproj",
    )(*args)
    return (res[0], res[1]) if has_dt else (res[0], None)


def _norm_rope(v, gain, cos, sin):
    ones = jnp.ones((HEAD_DIM, HEAD_DIM), BF16)
    ss = jnp.dot((v * v).astype(BF16), ones, preferred_element_type=F32)
    y = v * lax.rsqrt(ss * (1.0 / HEAD_DIM) + RMS_EPS) * gain
    if cos is not None:
        quarter = ROPE_AXIS_DIM // 2
        lane = lax.broadcasted_iota(jnp.int32, (1, HEAD_DIM), 1)
        partner = jnp.where(lane % ROPE_AXIS_DIM < quarter,
                            pltpu.roll(y, HEAD_DIM - quarter, 1), pltpu.roll(y, quarter, 1))
        y = y * cos + partner * sin
    return y.astype(BF16)


def _attn_kernel(q_ref, g_ref, kl_ref, vl_ref, kc_ref, vc_ref, qg_ref, kg_ref, cq_ref, sq_ref, ck_ref, sk_ref,
                 o_ref, k_scr, s_scr, *, nrep, tk):
    tq = q_ref.shape[0]
    rows = nrep * tq
    ctx_len = kc_ref.shape[0]
    blocks, col = [], 0
    for v_ref in (vc_ref, vl_ref):
        n = v_ref.shape[0]
        for r0 in range(0, n, tk):
            blocks.append((v_ref, r0, min(tk, n - r0), col))
            col += min(tk, n - r0)

    @pl.when(pl.program_id(2) == 0)
    def _prep_keys():
        kg = kg_ref[...]
        for v_ref, r0, size, c0 in blocks:
            if v_ref is vc_ref:
                k_scr[pl.ds(c0, size), :] = _norm_rope(kc_ref[pl.ds(r0, size), :].astype(F32), kg, None, None)
            else:
                k_scr[pl.ds(c0, size), :] = _norm_rope(kl_ref[pl.ds(r0, size), :].astype(F32), kg,
                                                       ck_ref[pl.ds(r0, size), :], sk_ref[pl.ds(r0, size), :])

    qg = qg_ref[...]
    qs = jnp.concatenate([_norm_rope(q_ref[:, r * HEAD_DIM:(r + 1) * HEAD_DIM].astype(F32), qg,
                                     cq_ref[...], sq_ref[...]) for r in range(nrep)], axis=0)

    half = rows // 2
    halves = (slice(0, half), slice(half, rows))
    m = [None, None]
    for _, r0, size, c0 in blocks:
        kb = k_scr[pl.ds(c0, size), :]
        for i, hs in enumerate(halves):
            s = lax.dot_general(qs[hs], kb, (((1,), (1,)), ((), ())), preferred_element_type=F32)
            s_scr[hs, c0:c0 + size] = s
            for c in range(size // LANES):
                sc = s[:, c * LANES:(c + 1) * LANES]
                m[i] = sc if m[i] is None else jnp.maximum(m[i], sc)
    mb = [jnp.broadcast_to(jnp.max(mi, axis=-1, keepdims=True), (half, LANES)) for mi in m]

    acc = [jnp.zeros((half, HEAD_DIM + LANES), F32) for _ in halves]
    for v_ref, r0, size, c0 in blocks:
        v_aug = jnp.concatenate([v_ref[pl.ds(r0, size), :], jnp.ones((size, LANES), BF16)], axis=1)
        for i, hs in enumerate(halves):
            p = jnp.concatenate([jnp.exp2(s_scr[hs, c0 + c * LANES:c0 + (c + 1) * LANES] - mb[i])
                                 for c in range(size // LANES)], axis=1).astype(BF16)
            acc[i] = acc[i] + jnp.dot(p, v_aug, preferred_element_type=F32)
    acc = jnp.concatenate(acc, axis=0)
    o = acc[:, :HEAD_DIM] / acc[:, HEAD_DIM:]
    for r in range(nrep):
        g = g_ref[:, r * HEAD_DIM:(r + 1) * HEAD_DIM].astype(F32)
        o_ref[:, r * HEAD_DIM:(r + 1) * HEAD_DIM] = (o[r * tq:(r + 1) * tq] * _silu(g)).astype(o_ref.dtype)


def _attention(proj, projc, q_gain, k_gain, cos_t, sin_t, *, batch, seq, ctx_len, attn_width):
    n_kv = attn_width // HEAD_DIM // Q_PER_KV
    gw = Q_PER_KV * HEAD_DIM
    tq = _tile(seq, 256)
    tk = 4 * LANES
    assert ctx_len % LANES == 0 and seq % LANES == 0
    tpb = seq // tq
    k_cb = (2 * attn_width + _ssd_width(attn_width)) // HEAD_DIM
    v_cb = k_cb + n_kv
    return pl.pallas_call(
        functools.partial(_attn_kernel, nrep=Q_PER_KV, tk=tk),
        grid=(batch, n_kv, tpb),
        in_specs=[pl.BlockSpec((tq, gw), lambda b, h, t: (b * tpb + t, h)),
                  pl.BlockSpec((tq, gw), lambda b, h, t: (b * tpb + t, attn_width // gw + h)),
                  pl.BlockSpec((seq, HEAD_DIM), lambda b, h, t: (b, k_cb + h)),
                  pl.BlockSpec((seq, HEAD_DIM), lambda b, h, t: (b, v_cb + h)),
                  pl.BlockSpec((ctx_len, HEAD_DIM), lambda b, h, t: (b, h)),
                  pl.BlockSpec((ctx_len, HEAD_DIM), lambda b, h, t: (b, n_kv + h)),
                  pl.BlockSpec((1, HEAD_DIM), lambda b, h, t: (0, 0)),
                  pl.BlockSpec((1, HEAD_DIM), lambda b, h, t: (0, 0)),
                  pl.BlockSpec((tq, HEAD_DIM), lambda b, h, t: (t, 0)),
                  pl.BlockSpec((tq, HEAD_DIM), lambda b, h, t: (t, 0)),
                  pl.BlockSpec((seq, HEAD_DIM), lambda b, h, t: (0, 0)),
                  pl.BlockSpec((seq, HEAD_DIM), lambda b, h, t: (0, 0))],
        out_specs=pl.BlockSpec((tq, gw), lambda b, h, t: (b * tpb + t, h)),
        out_shape=jax.ShapeDtypeStruct((batch * seq, attn_width), BF16),
        scratch_shapes=[pltpu.VMEM((ctx_len + seq, HEAD_DIM), BF16),
                        pltpu.VMEM((Q_PER_KV * tq, ctx_len + seq), F32)],
        compiler_params=_cparams("parallel", "parallel", "arbitrary"),
        name="gqa_attention",
    )(proj, proj, proj, proj, projc, projc, q_gain, k_gain, cos_t, sin_t, cos_t, sin_t)


def _ssd_width(attn_width):
    return attn_width


def _conv_silu(src_ref, pad_ref, w_ref, b_ref, dst_ref, n_rows):
    ch = SSD_CHUNK
    cw = src_ref.shape[1]
    win_rows = ch + 2 * CONV_HALO
    pad_ref[pl.ds(0, CONV_HALO), :] = jnp.zeros((CONV_HALO, cw), F32)
    pad_ref[pl.ds(CONV_HALO + n_rows, CONV_HALO), :] = jnp.zeros((CONV_HALO, cw), F32)

    def copy(i, carry):
        r = pl.multiple_of(i * ch, ch)
        pad_ref[pl.ds(r + CONV_HALO, ch), :] = src_ref[pl.ds(r, ch), :].astype(F32)
        return carry

    lax.fori_loop(0, n_rows // ch, copy, 0)
    w = w_ref[...]
    bias = b_ref[...]

    def body(i, carry):
        r = pl.multiple_of(i * ch, ch)
        win = pad_ref[pl.ds(r, win_rows), :]
        acc = bias + win[CONV_HALO:CONV_HALO + ch] * w[D_CONV // 2:D_CONV // 2 + 1]
        for k in range(D_CONV):
            off = k - D_CONV // 2
            if off == 0:
                continue
            shifted = pltpu.roll(win, (-off) % win_rows, 0)
            acc = acc + shifted[CONV_HALO:CONV_HALO + ch] * w[k:k + 1]
        dst_ref[pl.ds(r, ch), :] = _silu(acc).astype(dst_ref.dtype)
        return carry

    lax.fori_loop(0, n_rows // ch, body, 0)


def _ssd_kernel(xl_ref, bl_ref, cl_ref, zl_ref, dtl_ref, xc_ref, bc_ref, dtc_ref,
                wx_ref, wb_ref, wc_ref, bx_ref, bb_ref, bcv_ref, alog_ref, dtb_ref, dsk_ref, nrm_ref,
                o_ref,
                padx, padb, xs_s, b_s, c_s, xsc_s, bcc_s, y_s, stf, stb, terms_s, bt_s, eb_s, *, hpg):
    ch = SSD_CHUNK
    seq = xl_ref.shape[0]
    ctx_len = xc_ref.shape[0]
    gw = xl_ref.shape[1]
    npair = gw // LANES
    assert 2 * SSD_HEAD_DIM == LANES and hpg == 2 * npair

    _conv_silu(xl_ref, padx, wx_ref, bx_ref, xs_s, seq)
    _conv_silu(bl_ref, padb, wb_ref, bb_ref, b_s, seq)
    _conv_silu(cl_ref, padb, wc_ref, bcv_ref, c_s, seq)
    _conv_silu(xc_ref, padx, wx_ref, bx_ref, xsc_s, ctx_len)
    _conv_silu(bc_ref, padb, wb_ref, bb_ref, bcc_s, ctx_len)

    li = lax.broadcasted_iota(jnp.int32, (ch, ch), 0)
    si = lax.broadcasted_iota(jnp.int32, (ch, ch), 1)
    tri_t = (li <= si).astype(F32)
    nr = alog_ref.shape[0]
    lower = li >= si
    diag = li == si
    xr = lax.broadcasted_iota(jnp.int32, (ch, gw), 0)
    xh = lax.broadcasted_iota(jnp.int32, (ch, gw), 1) // SSD_HEAD_DIM
    spread_f = jnp.where(xr == 2 * nr + xh, 1.0, 0.0).astype(BF16)
    spread_b = jnp.where(xr == 2 * nr + hpg + xh, 1.0, 0.0).astype(BF16)
    row_id = lax.broadcasted_iota(jnp.int32, (nr, ch), 0)
    lane = lax.broadcasted_iota(jnp.int32, (1, LANES), 1)
    head0 = lane < SSD_HEAD_DIM
    neg_a_t = -jnp.exp(alog_ref[...]) * LOG2E
    dt_bias_t = dtb_ref[...]

    def chunk_terms(dt_ref, r):
        x_t = dt_ref[pl.ds(r, ch), :].T[:nr] + dt_bias_t
        dtp_t = jnp.maximum(x_t, 0.0) + jnp.log1p(jnp.exp(-jnp.abs(x_t)))
        da_t = dtp_t * neg_a_t
        p_t = jnp.dot(da_t, tri_t, preferred_element_type=F32, precision=lax.Precision.HIGHEST)
        return dtp_t, p_t, p_t - da_t

    def columns(p_t, e_t):
        tot = p_t[:, ch - 1:ch]
        fac_t = jnp.exp2(jnp.where(row_id < hpg, p_t, tot - e_t))
        pad = jnp.zeros((ch - 3 * nr, ch), F32)
        return jnp.concatenate([p_t, e_t, fac_t, pad], axis=0).T

    def pair_sel(a, b):
        return jnp.where(head0, a, b)

    def block_diag_rhs(x_pair):
        zero = jnp.zeros_like(x_pair)
        return jnp.concatenate([jnp.where(head0, x_pair, zero), jnp.where(head0, zero, x_pair)], axis=0)

    def state_update(st_ref, bt, x_ref, r, w_rows, tot):
        for q in range(npair):
            j0, j1 = 2 * q, 2 * q + 1
            lhs = jnp.concatenate([(bt * w_rows[j0]).astype(BF16), (bt * w_rows[j1]).astype(BF16)], axis=1)
            rhs = block_diag_rhs(x_ref[pl.ds(r, ch), q * LANES:(q + 1) * LANES])
            new = jnp.dot(lhs, rhs, preferred_element_type=F32)
            dec = pair_sel(jnp.exp2(tot[j0]), jnp.exp2(tot[j1]))
            cols = slice(q * LANES, (q + 1) * LANES)
            st_ref[:, cols] = st_ref[:, cols] * dec + new

    def fwd_state_terms(dtp_t, p_t):
        w_rows, tot = [], []
        for j in range(hpg):
            t = p_t[j:j + 1, ch - 1:ch]
            tot.append(t)
            w_rows.append(jnp.exp2(t - p_t[j:j + 1, :]) * dtp_t[j:j + 1, :])
        return w_rows, tot

    def bwd_state_terms(dtp_t, p_t, e_t):
        w_rows, tot = [], []
        for j in range(hpg, 2 * hpg):
            tot.append(p_t[j:j + 1, ch - 1:ch])
            w_rows.append(jnp.exp2(e_t[j:j + 1, :]) * dtp_t[j:j + 1, :])
        return w_rows, tot

    stf[...] = jnp.zeros(stf.shape, F32)
    stb[...] = jnp.zeros(stb.shape, F32)
    n_cc = ctx_len // ch

    def ctx_fwd(c, carry):
        r = pl.multiple_of(c * ch, ch)
        dtp_t, p_t, e_t = chunk_terms(dtc_ref, r)
        bt = bcc_s[pl.ds(r, ch), :].astype(F32).T
        w_rows, tot = fwd_state_terms(dtp_t, p_t)
        state_update(stf, bt, xsc_s, r, w_rows, tot)
        return carry

    def ctx_bwd(i, carry):
        r = pl.multiple_of((n_cc - 1 - i) * ch, ch)
        dtp_t, p_t, e_t = chunk_terms(dtc_ref, r)
        bt = bcc_s[pl.ds(r, ch), :].astype(F32).T
        w_rows, tot = bwd_state_terms(dtp_t, p_t, e_t)
        state_update(stb, bt, xsc_s, r, w_rows, tot)
        return carry

    lax.fori_loop(0, n_cc, ctx_fwd, 0)
    lax.fori_loop(0, n_cc, ctx_bwd, 0)

    n_lc = seq // ch

    def lat_fwd(c, carry):
        r = pl.multiple_of(c * ch, ch)
        dtp_t, p_t, e_t = chunk_terms(dtl_ref, r)
        terms_s[c] = jnp.concatenate([dtp_t, p_t, e_t], axis=0)
        pe = columns(p_t, e_t)
        bm = b_s[pl.ds(r, ch), :]
        cm = c_s[pl.ds(r, ch), :]
        bt = bm.astype(F32).T
        bt_s[c] = bt
        cb = lax.dot_general(cm, bm, (((1,), (1,)), ((), ())), preferred_element_type=F32)
        y_off = jnp.dot(cm, stf[...].astype(BF16), preferred_element_type=F32)
        pe16 = pe.astype(BF16)
        y_off = y_off * jnp.dot(pe16, spread_f, preferred_element_type=F32)
        eb_s[pl.ds(r, ch), :] = jnp.dot(pe16, spread_b, preferred_element_type=F32)
        lg_t = jnp.log2(dtp_t)
        row_f = p_t - lg_t
        row_b = e_t + lg_t
        for q in range(npair):
            ms = []
            for j in (2 * q, 2 * q + 1):
                jb = hpg + j
                pcol = pe[:, j:j + 1]
                ecol = pe[:, nr + jb:nr + jb + 1]
                t = jnp.where(lower, pcol - row_f[j:j + 1, :], row_b[jb:jb + 1, :] - ecol)
                ms.append((cb * (jnp.exp2(t) + jnp.where(diag, dtp_t[jb:jb + 1, :], 0.0))).astype(BF16))
            cols = slice(q * LANES, (q + 1) * LANES)
            rhs = block_diag_rhs(xs_s[pl.ds(r, ch), cols])
            y = jnp.dot(jnp.concatenate(ms, axis=1), rhs, preferred_element_type=F32)
            y_s[pl.ds(r, ch), cols] = y + y_off[:, cols]
        w_rows, tot = fwd_state_terms(dtp_t, p_t)
        state_update(stf, bt, xs_s, r, w_rows, tot)
        return carry

    lax.fori_loop(0, n_lc, lat_fwd, 0, unroll=2)

    dskip = dsk_ref[...]
    nrm = nrm_ref[...]

    def lat_bwd(i, carry):
        c = n_lc - 1 - i
        r = pl.multiple_of(c * ch, ch)
        terms = terms_s[c]
        dtp_t, p_t, e_t = terms[:nr], terms[nr:2 * nr], terms[2 * nr:]
        y_off = jnp.dot(c_s[pl.ds(r, ch), :], stb[...].astype(BF16), preferred_element_type=F32)
        y = y_s[pl.ds(r, ch), :] + y_off * eb_s[pl.ds(r, ch), :]
        w_rows, tot = bwd_state_terms(dtp_t, p_t, e_t)
        state_update(stb, bt_s[c], xs_s, r, w_rows, tot)
        y = (y + xs_s[pl.ds(r, ch), :].astype(F32) * dskip) * _silu(zl_ref[pl.ds(r, ch), :].astype(F32))
        y = y * lax.rsqrt(jnp.mean(y * y, axis=-1, keepdims=True) + RMS_EPS) * nrm
        o_ref[pl.ds(r, ch), :] = y.astype(o_ref.dtype)
        return carry

    lax.fori_loop(0, n_lc, lat_bwd, 0)


def _ssd(proj, dt, projc, dtc, conv_w, conv_b, alog_g, dtb_g, dskip_c, ssd_norm, *,
         batch, seq, ctx_len, attn_width):
    ssd_width = _ssd_width(attn_width)
    n_kv = attn_width // HEAD_DIM // Q_PER_KV
    kv_width = n_kv * HEAD_DIM
    gw = ssd_width // N_SSD_GROUPS
    hpg = gw // SSD_HEAD_DIM
    xs_off = 3 * attn_width + 2 * kv_width
    xs_off_c = 2 * kv_width
    assert xs_off % gw == 0 and xs_off_c % gw == 0 and D_STATE == LANES
    nb = N_SSD_GROUPS
    b_cb = (xs_off + ssd_width) // LANES
    c_cb = b_cb + nb
    b_cb_c = (xs_off_c + ssd_width) // LANES
    cw_b = ssd_width // LANES
    nr = alog_g.shape[1]
    in_specs = [
        pl.BlockSpec((seq, gw), lambda b, g: (b, xs_off // gw + g)),
        pl.BlockSpec((seq, LANES), lambda b, g: (b, b_cb + g)),
        pl.BlockSpec((seq, LANES), lambda b, g: (b, c_cb + g)),
        pl.BlockSpec((seq, gw), lambda b, g: (b, 2 * attn_width // gw + g)),
        pl.BlockSpec((seq, LANES), lambda b, g: (b, g)),
        pl.BlockSpec((ctx_len, gw), lambda b, g: (b, xs_off_c // gw + g)),
        pl.BlockSpec((ctx_len, LANES), lambda b, g: (b, b_cb_c + g)),
        pl.BlockSpec((ctx_len, LANES), lambda b, g: (b, g)),
        pl.BlockSpec((D_CONV, gw), lambda b, g: (0, g)),
        pl.BlockSpec((D_CONV, LANES), lambda b, g: (0, cw_b + g)),
        pl.BlockSpec((D_CONV, LANES), lambda b, g: (0, cw_b + nb + g)),
        pl.BlockSpec((1, gw), lambda b, g: (0, g)),
        pl.BlockSpec((1, LANES), lambda b, g: (0, cw_b + g)),
        pl.BlockSpec((1, LANES), lambda b, g: (0, cw_b + nb + g)),
        pl.BlockSpec((None, nr, LANES), lambda b, g: (g, 0, 0)),
        pl.BlockSpec((None, nr, LANES), lambda b, g: (g, 0, 0)),
        pl.BlockSpec((1, gw), lambda b, g: (0, g)),
        pl.BlockSpec((1, gw), lambda b, g: (0, g)),
    ]
    pad_rows = max(seq, ctx_len) + 2 * CONV_HALO
    n_chunks = seq // SSD_CHUNK
    return pl.pallas_call(
        functools.partial(_ssd_kernel, hpg=hpg),
        grid=(batch, N_SSD_GROUPS),
        in_specs=in_specs,
        out_specs=pl.BlockSpec((seq, gw), lambda b, g: (b, g)),
        out_shape=jax.ShapeDtypeStruct((batch * seq, ssd_width), BF16),
        scratch_shapes=[pltpu.VMEM((pad_rows, gw), F32),
                        pltpu.VMEM((pad_rows, LANES), F32),
                        pltpu.VMEM((seq, gw), BF16),
                        pltpu.VMEM((seq, LANES), BF16),
                        pltpu.VMEM((seq, LANES), BF16),
                        pltpu.VMEM((ctx_len, gw), BF16),
                        pltpu.VMEM((ctx_len, LANES), BF16),
                        pltpu.VMEM((seq, gw), F32),
                        pltpu.VMEM((D_STATE, gw), F32),
                        pltpu.VMEM((D_STATE, gw), F32),
                        pltpu.VMEM((n_chunks, 3 * nr, SSD_CHUNK), F32),
                        pltpu.VMEM((n_chunks, D_STATE, SSD_CHUNK), F32),
                        pltpu.VMEM((seq, gw), F32)],
        compiler_params=_cparams("parallel", "parallel"),
        name="ssd_bidir",
    )(proj, proj, proj, proj, dt, projc, projc, dtc,
      conv_w, conv_w, conv_w, conv_b, conv_b, conv_b, alog_g, dtb_g, dskip_c, ssd_norm)


def _outproj_kernel(a_ref, y_ref, w0_ref, w1_ref, x_ref, gate_ref, o_ref):
    acc = jnp.dot(a_ref[...], w0_ref[...], preferred_element_type=F32)
    acc = acc + jnp.dot(y_ref[...], w1_ref[...], preferred_element_type=F32)
    o_ref[...] = x_ref[...] + gate_ref[...] * acc


def _outproj(lhs0, cb0, lhs1, cb1, w, x2d, mod, *, rows_per_mod):
    m, d = x2d.shape
    kh = w.shape[0] // 2
    tm = _tile(rows_per_mod, 1024)
    tn = _tile(d, 512)
    tiles_per_mod = rows_per_mod // tm
    return pl.pallas_call(
        _outproj_kernel,
        grid=(m // tm, d // tn),
        in_specs=[pl.BlockSpec((tm, kh), lambda i, j: (i, cb0)),
                  pl.BlockSpec((tm, kh), lambda i, j: (i, cb1)),
                  pl.BlockSpec((kh, tn), lambda i, j: (0, j)),
                  pl.BlockSpec((kh, tn), lambda i, j: (1, j)),
                  pl.BlockSpec((tm, tn), lambda i, j: (i, j)),
                  pl.BlockSpec((None, 1, tn), lambda i, j: (3 * (i // tiles_per_mod) + 2, 0, j))],
        out_specs=pl.BlockSpec((tm, tn), lambda i, j: (i, j)),
        out_shape=jax.ShapeDtypeStruct((m, d), F32),
        compiler_params=_cparams("parallel", "arbitrary"),
        name="outproj_residual",
    )(lhs0, lhs1, w, w, x2d, mod)


FFT_RADIX = 8


def _cmul_root(z, k, n):
    re, im = z
    eighth = (8 * k) // n % 8
    assert (8 * k) % n == 0
    r = math.sqrt(0.5)
    if eighth == 0:
        return re, im
    if eighth == 1:
        return (re + im) * r, (im - re) * r
    if eighth == 2:
        return im, -re
    if eighth == 3:
        return (im - re) * r, -(re + im) * r
    nre, nim = _cmul_root(z, k - n // 2, n)
    return -nre, -nim


def _fft_small(v):
    n = len(v)
    if n == 1:
        return v
    ev, od = _fft_small(v[0::2]), _fft_small(v[1::2])
    out = [None] * n
    for k in range(n // 2):
        tr, ti = _cmul_root(od[k], k, n)
        out[k] = (ev[k][0] + tr, ev[k][1] + ti)
        out[k + n // 2] = (ev[k][0] - tr, ev[k][1] - ti)
    return out


def _fourier_kernel(u_ref, z_ref, w_ref, wc_ref, o_ref, uf_scr, y_scr, ab_scr, *, rows_per_step):
    seq, tn = u_ref.shape
    n2 = seq // FFT_RADIX
    rs = rows_per_step

    nslab = tn // LANES

    def widen(i, carry):
        r = pl.multiple_of(i * LANES, LANES)
        for c in range(nslab):
            uf_scr[c, pl.ds(r, LANES), :] = u_ref[pl.ds(r, LANES), c * LANES:(c + 1) * LANES].astype(F32)
        return carry

    lax.fori_loop(0, seq // LANES, widen, 0)
    for a in range(FFT_RADIX):
        ua = jnp.concatenate([uf_scr[c, pl.ds(a, n2, stride=FFT_RADIX), :] for c in range(nslab)], axis=1)
        y_scr[a] = jnp.dot(w_ref[a], ua.astype(BF16), preferred_element_type=F32)

    def combine(i, carry):
        r = pl.multiple_of(i * rs, rs)
        z = [(y_scr[a, pl.ds(r, rs), :], y_scr[a, pl.ds(n2 + r, rs), :]) for a in range(FFT_RADIX)]
        x = _fft_small(z)
        for k1 in range(FFT_RADIX):
            ab_scr[pl.ds(k1 * n2 + r, rs), 0:tn] = x[k1][0].astype(BF16)
            ab_scr[pl.ds(k1 * n2 + r, rs), tn:2 * tn] = (-x[k1][1]).astype(BF16)
        return carry

    lax.fori_loop(0, n2 // rs, combine, 0)

    tm = min(seq, 4 * LANES)

    def channels(i, carry):
        r = pl.multiple_of(i * tm, tm)
        f = jnp.dot(ab_scr[pl.ds(r, tm), :], wc_ref[...], preferred_element_type=F32)
        o_ref[pl.ds(r, tm), :] = (f * _silu(z_ref[pl.ds(r, tm), :].astype(F32))).astype(o_ref.dtype)
        return carry

    lax.fori_loop(0, seq // tm, channels, 0)


def _fourier_gate(uz, *, batch, seq, width):
    n2 = seq // FFT_RADIX
    gw = width // N_FOURIER_GROUPS
    k2 = np.arange(n2, dtype=np.int64)[None, :, None]
    pos = FFT_RADIX * np.arange(n2, dtype=np.int64)[None, None, :] + np.arange(FFT_RADIX, dtype=np.int64)[:, None, None]
    ang = 2.0 * np.pi * ((k2 * pos) % seq).astype(np.float64) / seq
    w = jnp.asarray((np.concatenate([np.cos(ang), -np.sin(ang)], axis=1) / np.sqrt(seq)).astype(np.float32))
    kc = np.arange(gw, dtype=np.int64)
    angc = 2.0 * np.pi * ((kc[:, None] * kc[None, :]) % gw).astype(np.float64) / gw
    wc = jnp.asarray((np.concatenate([np.cos(angc), -np.sin(angc)], axis=0) / np.sqrt(gw)).astype(np.float32))
    return pl.pallas_call(
        functools.partial(_fourier_kernel, rows_per_step=2 * SUBLANES),
        grid=(batch, N_FOURIER_GROUPS),
        in_specs=[pl.BlockSpec((seq, gw), lambda b, g: (b, g)),
                  pl.BlockSpec((seq, gw), lambda b, g: (b, N_FOURIER_GROUPS + g)),
                  pl.BlockSpec((FFT_RADIX, 2 * n2, n2), lambda b, g: (0, 0, 0)),
                  pl.BlockSpec((2 * gw, gw), lambda b, g: (0, 0))],
        out_specs=pl.BlockSpec((seq, gw), lambda b, g: (b, g)),
        out_shape=jax.ShapeDtypeStruct((batch * seq, width), BF16),
        scratch_shapes=[pltpu.VMEM((gw // LANES, seq, LANES), F32),
                        pltpu.VMEM((FFT_RADIX, 2 * n2, gw), F32),
                        pltpu.VMEM((seq, 2 * gw), BF16)],
        compiler_params=_cparams("parallel", "parallel"),
        name="fourier_mix_gate",
    )(uz, uz, w.astype(BF16), wc.astype(BF16))


def _rope_tables(seq):
    pos = jnp.arange(seq)
    row = (pos // GRID_W).astype(F32)
    col = (pos % GRID_W).astype(F32)
    inv_freq = ROPE_THETA ** (-jnp.arange(0, ROPE_AXIS_DIM, 2, dtype=F32) / ROPE_AXIS_DIM)
    ar, ac = row[:, None] * inv_freq, col[:, None] * inv_freq
    cos_t = jnp.concatenate([jnp.cos(ar), jnp.cos(ar), jnp.cos(ac), jnp.cos(ac)], axis=-1)
    sin_t = jnp.concatenate([-jnp.sin(ar), jnp.sin(ar), -jnp.sin(ac), jnp.sin(ac)], axis=-1)
    return cos_t, sin_t


def _layer(a, j):
    return a.reshape(a.shape[1:]) if a.shape[0] == 1 else a[j]


def _per_group_rows(v, hpg):
    g = v.shape[1] // hpg
    rows = -(-2 * hpg // SUBLANES) * SUBLANES
    t = v.reshape(2, g, hpg).transpose(1, 0, 2).reshape(g, 2 * hpg)
    t = jnp.pad(t, ((0, 0), (0, rows - 2 * hpg)))
    return jnp.broadcast_to(t[:, :, None], (g, rows, LANES))


def kernel(x, c, ctx, c_ctx, ev_norm_w, ev_ada_w, ev_ada_b, ev_w_in, ev_q_norm, ev_k_norm, ev_conv_w,
           ev_conv_b, ev_a_log, ev_dt_bias, ev_d_skip, ev_ssd_norm, ev_w_out, od_norm_w, od_ada_w,
           od_ada_b, od_w_in, od_w_out):
    batch, seq, d = x.shape
    ctx_len = ctx.shape[1]
    mix = ev_w_out.shape[1]
    attn_width = mix // 2
    ssd_width = mix - attn_width
    n_ssd_heads = ssd_width // SSD_HEAD_DIM
    hpg = n_ssd_heads // N_SSD_GROUPS
    n_kv = attn_width // HEAD_DIM // Q_PER_KV
    kv_width = n_kv * HEAD_DIM
    conv_ch = ssd_width + 2 * N_SSD_GROUPS * D_STATE
    ctx_col0 = 2 * attn_width + ssd_width
    n_main = ctx_col0 + 2 * kv_width + conv_ch
    m = batch * seq
    x2d = x.reshape(m, d)
    ctx2d = ctx.reshape(batch * ctx_len, d)
    n_mod_rows = 2 * SUBLANES
    assert batch + 1 <= n_mod_rows

    w_in = _layer(ev_w_in, 0)
    w_main = w_in.astype(BF16)
    qscale = LOG2E * HEAD_DIM ** -0.5
    q_gain = (_layer(ev_q_norm, 0) * qscale).reshape(1, HEAD_DIM)
    k_gain = _layer(ev_k_norm, 0).reshape(1, HEAD_DIM)
    cos_t, sin_t = _rope_tables(seq)
    w_dt = w_in[:, n_main:].reshape(d, 2, N_SSD_GROUPS, hpg).transpose(0, 2, 1, 3)
    w_dt = jnp.pad(w_dt.reshape(d, N_SSD_GROUPS, 2 * hpg), ((0, 0), (0, 0), (0, LANES - 2 * hpg)))
    w_dt = w_dt.reshape(d, N_SSD_GROUPS * LANES).astype(BF16)
    cond = jnp.zeros((n_mod_rows, d), F32).at[:batch].set(c).at[batch].set(c_ctx)
    mod0 = _ada(cond, _layer(ev_ada_w, 0), _layer(ev_ada_b, 0)).reshape(3 * n_mod_rows, 1, d)
    norm_w0 = _layer(ev_norm_w, 0)

    proj, dt = _inproj(x2d, mod0, norm_w0, w_main, w_dt,
                       rows_per_mod=seq, mod_row0=0, col0=0, ncols=n_main)
    projc, dtc = _inproj(ctx2d, mod0, norm_w0, w_main, w_dt,
                         rows_per_mod=batch * ctx_len, mod_row0=batch, col0=ctx_col0, ncols=n_main - ctx_col0)

    attn = _attention(proj, projc, q_gain, k_gain, cos_t, sin_t,
                      batch=batch, seq=seq, ctx_len=ctx_len, attn_width=attn_width)
    yssd = _ssd(proj, dt, projc, dtc, _layer(ev_conv_w, 0), _layer(ev_conv_b, 0).reshape(1, conv_ch),
                _per_group_rows(_layer(ev_a_log, 0), hpg), _per_group_rows(_layer(ev_dt_bias, 0), hpg),
                jnp.repeat(_layer(ev_d_skip, 0), SSD_HEAD_DIM).reshape(1, ssd_width),
                _layer(ev_ssd_norm, 0).reshape(1, ssd_width),
                batch=batch, seq=seq, ctx_len=ctx_len, attn_width=attn_width)
    x1 = _outproj(attn, 0, yssd, 0, _layer(ev_w_out, 0).astype(BF16), x2d, mod0, rows_per_mod=seq)

    fw = od_w_out.shape[1]
    cond1 = jnp.zeros((n_mod_rows, d), F32).at[:batch].set(c)
    mod1 = _ada(cond1, _layer(od_ada_w, 0), _layer(od_ada_b, 0)).reshape(3 * n_mod_rows, 1, d)
    uz, _ = _inproj(x1, mod1, _layer(od_norm_w, 0), _layer(od_w_in, 0).astype(BF16), None,
                    rows_per_mod=seq, mod_row0=0, col0=0, ncols=2 * fw)
    fg = _fourier_gate(uz, batch=batch, seq=seq, width=fw)
    x2 = _outproj(fg, 0, fg, 1, _layer(od_w_out, 0).astype(BF16), x1, mod1, rows_per_mod=seq)
    return x2.reshape(batch, seq, d)
```

```python
import functools
import math

import numpy as np
import jax
import jax.numpy as jnp
from jax import lax
from jax.experimental import pallas as pl
from jax.experimental.pallas import tpu as pltpu

F32 = jnp.float32
BF16 = jnp.bfloat16

GRID_W = 64
HEAD_DIM = 128
Q_PER_KV = 4
ROPE_THETA = 10000.0
ROPE_AXIS_DIM = HEAD_DIM // 2
SSD_HEAD_DIM = 64
N_SSD_GROUPS = 4
D_STATE = 128
D_CONV = 5
SSD_CHUNK = 128
N_FOURIER_GROUPS = 8
RMS_EPS = 1e-6

LANES = 128
SUBLANES = 8
VMEM_LIMIT_BYTES = 56 * 1024 * 1024

CONV_HALO = SSD_CHUNK // 2
LOG2E = 1.4426950408889634


def _cparams(*sem):
    return pltpu.CompilerParams(dimension_semantics=sem, vmem_limit_bytes=VMEM_LIMIT_BYTES)


def _tile(n, pref):
    t = min(n, pref)
    while n % t:
        t //= 2
    return t


def _silu(v):
    return v * jax.nn.sigmoid(v)


def _ada_kernel(c_ref, w_ref, b_ref, o_ref):
    c = c_ref[...]
    s = _silu(c).astype(BF16)
    o_ref[...] = jnp.dot(s, w_ref[...].astype(BF16), preferred_element_type=F32) + b_ref[...]


def _ada(cond, w, b):
    r, d = cond.shape
    n = w.shape[1]
    tn = _tile(n, 1024)
    return pl.pallas_call(
        _ada_kernel,
        grid=(n // tn,),
        in_specs=[pl.BlockSpec((r, d), lambda j: (0, 0)),
                  pl.BlockSpec((d, tn), lambda j: (0, j)),
                  pl.BlockSpec((1, tn), lambda j: (0, j))],
        out_specs=pl.BlockSpec((r, tn), lambda j: (0, j)),
        out_shape=jax.ShapeDtypeStruct((r, n), F32),
        compiler_params=_cparams("parallel"),
        name="ada_mod",
    )(cond, w, b.reshape(1, n))


def _inproj_kernel(x_ref, sh_ref, sc_ref, nw_ref, w_ref, *rest, has_dt, sub):
    if has_dt:
        wdt_ref, o_ref, dt_ref, h_ref = rest
    else:
        o_ref, h_ref = rest
    tm = x_ref.shape[0]

    @pl.when(pl.program_id(1) == 0)
    def _():
        nw = nw_ref[...]
        sc = 1.0 + sc_ref[...]
        sh = sh_ref[...]

        def body(s, carry):
            r = pl.multiple_of(s * sub, sub)
            x = x_ref[pl.ds(r, sub), :]
            y = x * lax.rsqrt(jnp.mean(x * x, axis=-1, keepdims=True) + RMS_EPS) * nw
            h_ref[pl.ds(r, sub), :] = (y * sc + sh).astype(BF16)
            return carry

        lax.fori_loop(0, tm // sub, body, 0, unroll=4)
        if has_dt:
            dt_ref[...] = jnp.dot(h_ref[...], wdt_ref[...], preferred_element_type=F32)

    o_ref[...] = jnp.dot(h_ref[...], w_ref[...], preferred_element_type=F32).astype(o_ref.dtype)


def _inproj(x2d, mod, norm_w, w, wdt, *, rows_per_mod, mod_row0, col0, ncols):
    m, d = x2d.shape
    tm = _tile(rows_per_mod, 1024)
    has_dt = wdt is not None
    tn = _tile(math.gcd(ncols, col0), 2048)
    cb0 = col0 // tn
    tiles_per_mod = rows_per_mod // tm

    def modrow(i):
        return mod_row0 + i // tiles_per_mod

    in_specs = [pl.BlockSpec((tm, d), lambda i, j: (i, 0)),
                pl.BlockSpec((None, 1, d), lambda i, j: (3 * modrow(i), 0, 0)),
                pl.BlockSpec((None, 1, d), lambda i, j: (3 * modrow(i) + 1, 0, 0)),
                pl.BlockSpec((1, d), lambda i, j: (0, 0)),
                pl.BlockSpec((d, tn), lambda i, j: (0, cb0 + j))]
    args = [x2d, mod, mod, norm_w.reshape(1, d), w]
    out_specs = [pl.BlockSpec((tm, tn), lambda i, j: (i, j))]
    out_shape = [jax.ShapeDtypeStruct((m, ncols), BF16)]
    if has_dt:
        ndt = wdt.shape[1]
        in_specs.append(pl.BlockSpec((d, ndt), lambda i, j: (0, 0)))
        args.append(wdt)
        out_specs.append(pl.BlockSpec((tm, ndt), lambda i, j: (i, 0)))
        out_shape.append(jax.ShapeDtypeStruct((m, ndt), F32))
    res = pl.pallas_call(
        functools.partial(_inproj_kernel, has_dt=has_dt, sub=min(32, tm)),
        grid=(m // tm, ncols // tn),
        in_specs=in_specs,
        out_specs=out_specs,
        out_shape=out_shape,
        scratch_shapes=[pltpu.VMEM((tm, d), BF16)],
        compiler_params=_cparams("parallel", "arbitrary"),
        name="mod_inproj",
    )(*args)
    return (res[0], res[1]) if has_dt else (res[0], None)


def _norm_rope(v, gain, cos, sin):
    ones = jnp.ones((HEAD_DIM, HEAD_DIM), BF16)
    ss = jnp.dot((v * v).astype(BF16), ones, preferred_element_type=F32)
    y = v * lax.rsqrt(ss * (1.0 / HEAD_DIM) + RMS_EPS) * gain
    if cos is not None:
        quarter = ROPE_AXIS_DIM // 2
        lane = lax.broadcasted_iota(jnp.int32, (1, HEAD_DIM), 1)
        partner = jnp.where(lane % ROPE_AXIS_DIM < quarter,
                            pltpu.roll(y, HEAD_DIM - quarter, 1), pltpu.roll(y, quarter, 1))
        y = y * cos + partner * sin
    return y.astype(BF16)


def _attn_kernel(q_ref, g_ref, kl_ref, vl_ref, kc_ref, vc_ref, qg_ref, kg_ref, cq_ref, sq_ref, ck_ref, sk_ref,
                 o_ref, k_scr, s_scr, *, nrep, tk):
    tq = q_ref.shape[0]
    rows = nrep * tq
    blocks, col = [], 0
    for v_ref in (vc_ref, vl_ref):
        n = v_ref.shape[0]
        for r0 in range(0, n, tk):
            blocks.append((v_ref, r0, min(tk, n - r0), col))
            col += min(tk, n - r0)

    @pl.when(pl.program_id(2) == 0)
    def _prep_keys():
        kg = kg_ref[...]
        for v_ref, r0, size, c0 in blocks:
            if v_ref is vc_ref:
                k_scr[pl.ds(c0, size), :] = _norm_rope(kc_ref[pl.ds(r0, size), :].astype(F32), kg, None, None)
            else:
                k_scr[pl.ds(c0, size), :] = _norm_rope(kl_ref[pl.ds(r0, size), :].astype(F32), kg,
                                                       ck_ref[pl.ds(r0, size), :], sk_ref[pl.ds(r0, size), :])

    qg = qg_ref[...]
    qs = jnp.concatenate([_norm_rope(q_ref[:, r * HEAD_DIM:(r + 1) * HEAD_DIM].astype(F32), qg,
                                     cq_ref[...], sq_ref[...]) for r in range(nrep)], axis=0)

    half = rows // 2
    halves = (slice(0, half), slice(half, rows))
    m = [None, None]
    for _, r0, size, c0 in blocks:
        kb = k_scr[pl.ds(c0, size), :]
        for i, hs in enumerate(halves):
            s = lax.dot_general(qs[hs], kb, (((1,), (1,)), ((), ())), preferred_element_type=F32)
            s_scr[hs, c0:c0 + size] = s
            for c in range(size // LANES):
                sc = s[:, c * LANES:(c + 1) * LANES]
                m[i] = sc if m[i] is None else jnp.maximum(m[i], sc)
    mb = [jnp.broadcast_to(jnp.max(mi, axis=-1, keepdims=True), (half, LANES)) for mi in m]

    acc = [jnp.zeros((half, HEAD_DIM + LANES), F32) for _ in halves]
    for v_ref, r0, size, c0 in blocks:
        v_aug = jnp.concatenate([v_ref[pl.ds(r0, size), :], jnp.ones((size, LANES), BF16)], axis=1)
        for i, hs in enumerate(halves):
            p = jnp.concatenate([jnp.exp2(s_scr[hs, c0 + c * LANES:c0 + (c + 1) * LANES] - mb[i])
                                 for c in range(size // LANES)], axis=1).astype(BF16)
            acc[i] = acc[i] + jnp.dot(p, v_aug, preferred_element_type=F32)
    acc = jnp.concatenate(acc, axis=0)
    o = acc[:, :HEAD_DIM] / acc[:, HEAD_DIM:]
    for r in range(nrep):
        g = g_ref[:, r * HEAD_DIM:(r + 1) * HEAD_DIM].astype(F32)
        o_ref[:, r * HEAD_DIM:(r + 1) * HEAD_DIM] = (o[r * tq:(r + 1) * tq] * _silu(g)).astype(o_ref.dtype)


def _attention(proj, projc, q_gain, k_gain, cos_t, sin_t, *, batch, seq, ctx_len, attn_width):
    n_kv = attn_width // HEAD_DIM // Q_PER_KV
    gw = Q_PER_KV * HEAD_DIM
    tq = _tile(seq, 256)
    tk = 4 * LANES
    assert ctx_len % LANES == 0 and seq % LANES == 0
    tpb = seq // tq
    k_cb = (2 * attn_width + _ssd_width(attn_width)) // HEAD_DIM
    v_cb = k_cb + n_kv
    return pl.pallas_call(
        functools.partial(_attn_kernel, nrep=Q_PER_KV, tk=tk),
        grid=(batch, n_kv, tpb),
        in_specs=[pl.BlockSpec((tq, gw), lambda b, h, t: (b * tpb + t, h)),
                  pl.BlockSpec((tq, gw), lambda b, h, t: (b * tpb + t, attn_width // gw + h)),
                  pl.BlockSpec((seq, HEAD_DIM), lambda b, h, t: (b, k_cb + h)),
                  pl.BlockSpec((seq, HEAD_DIM), lambda b, h, t: (b, v_cb + h)),
                  pl.BlockSpec((ctx_len, HEAD_DIM), lambda b, h, t: (b, h)),
                  pl.BlockSpec((ctx_len, HEAD_DIM), lambda b, h, t: (b, n_kv + h)),
                  pl.BlockSpec((1, HEAD_DIM), lambda b, h, t: (0, 0)),
                  pl.BlockSpec((1, HEAD_DIM), lambda b, h, t: (0, 0)),
                  pl.BlockSpec((tq, HEAD_DIM), lambda b, h, t: (t, 0)),
                  pl.BlockSpec((tq, HEAD_DIM), lambda b, h, t: (t, 0)),
                  pl.BlockSpec((seq, HEAD_DIM), lambda b, h, t: (0, 0)),
                  pl.BlockSpec((seq, HEAD_DIM), lambda b, h, t: (0, 0))],
        out_specs=pl.BlockSpec((tq, gw), lambda b, h, t: (b * tpb + t, h)),
        out_shape=jax.ShapeDtypeStruct((batch * seq, attn_width), BF16),
        scratch_shapes=[pltpu.VMEM((ctx_len + seq, HEAD_DIM), BF16),
                        pltpu.VMEM((Q_PER_KV * tq, ctx_len + seq), F32)],
        compiler_params=_cparams("parallel", "parallel", "arbitrary"),
        name="gqa_attention",
    )(proj, proj, proj, proj, projc, projc, q_gain, k_gain, cos_t, sin_t, cos_t, sin_t)


def _ssd_width(attn_width):
    return attn_width


def _conv_silu(groups, pad_ref, n_rows):
    ch = SSD_CHUNK
    halo = CONV_HALO
    win_rows = ch + 2 * halo
    widths = [g[0].shape[1] for g in groups]
    offs = [sum(widths[:i]) for i in range(len(groups))]
    total = sum(widths)
    pad_ref[pl.ds(0, halo), 0:total] = jnp.zeros((halo, total), BF16)
    pad_ref[pl.ds(halo + n_rows, halo), 0:total] = jnp.zeros((halo, total), BF16)

    def copy(i, carry):
        r = pl.multiple_of(i * ch, ch)
        for (src_ref, _, _, _), o, cw in zip(groups, offs, widths):
            pad_ref[pl.ds(r + halo, ch), o:o + cw] = src_ref[pl.ds(r, ch), :]
        return carry

    lax.fori_loop(0, n_rows // ch, copy, 0)
    w = jnp.concatenate([g[1][...] for g in groups], axis=1)
    bias = jnp.concatenate([g[2][...] for g in groups], axis=1)
    out_row = lax.broadcasted_iota(jnp.int32, (ch, win_rows), 0)
    win_row = lax.broadcasted_iota(jnp.int32, (ch, win_rows), 1)
    taps = [k for k in range(D_CONV) if k != D_CONV // 2]
    selects = [jnp.where(win_row == out_row + (halo + k - D_CONV // 2), 1.0, 0.0).astype(BF16) for k in taps]

    blk = 2 * LANES

    def body(i, carry):
        r = pl.multiple_of(i * ch, ch)
        ys = []
        for c0 in range(0, total, blk):
            cs = slice(c0, min(c0 + blk, total))
            win = pad_ref[pl.ds(r, win_rows), cs]
            acc = bias[:, cs] + win[halo:halo + ch].astype(F32) * w[D_CONV // 2:D_CONV // 2 + 1, cs]
            for k, select in zip(taps, selects):
                acc = acc + jnp.dot(select, win, preferred_element_type=F32) * w[k:k + 1, cs]
            ys.append(_silu(acc))
        y = jnp.concatenate(ys, axis=1)
        for (_, _, _, dst_ref), o, cw in zip(groups, offs, widths):
            dst_ref[pl.ds(r, ch), :] = y[:, o:o + cw].astype(dst_ref.dtype)
        return carry

    lax.fori_loop(0, n_rows // ch, body, 0)


def _ssd_kernel(xl_ref, bl_ref, cl_ref, zl_ref, dtl_ref, xc_ref, bc_ref, dtc_ref,
                wx_ref, wb_ref, wc_ref, bx_ref, bb_ref, bcv_ref, alog_ref, dtb_ref, dsk_ref, nrm_ref,
                o_ref,
                pad_s, xs_s, b_s, c_s, xsc_s, bcc_s, y_s, stf, stb, terms_s, bt_s, eb_s, *, hpg):
    ch = SSD_CHUNK
    seq = xl_ref.shape[0]
    ctx_len = xc_ref.shape[0]
    gw = xl_ref.shape[1]
    npair = gw // LANES
    assert 2 * SSD_HEAD_DIM == LANES and hpg == 2 * npair

    _conv_silu([(xl_ref, wx_ref, bx_ref, xs_s), (bl_ref, wb_ref, bb_ref, b_s), (cl_ref, wc_ref, bcv_ref, c_s)],
               pad_s, seq)
    _conv_silu([(xc_ref, wx_ref, bx_ref, xsc_s), (bc_ref, wb_ref, bb_ref, bcc_s)], pad_s, ctx_len)

    li = lax.broadcasted_iota(jnp.int32, (ch, ch), 0)
    si = lax.broadcasted_iota(jnp.int32, (ch, ch), 1)
    tri_t = (li <= si).astype(F32)
    nr = alog_ref.shape[0]
    lower = li >= si
    diag = li == si
    xr = lax.broadcasted_iota(jnp.int32, (ch, gw), 0)
    xh = lax.broadcasted_iota(jnp.int32, (ch, gw), 1) // SSD_HEAD_DIM
    spread_f = jnp.where(xr == 2 * nr + xh, 1.0, 0.0).astype(BF16)
    spread_b = jnp.where(xr == 2 * nr + hpg + xh, 1.0, 0.0).astype(BF16)
    row_id = lax.broadcasted_iota(jnp.int32, (nr, ch), 0)
    lane = lax.broadcasted_iota(jnp.int32, (1, LANES), 1)
    head0 = lane < SSD_HEAD_DIM
    neg_a_t = -jnp.exp(alog_ref[...]) * LOG2E
    dt_bias_t = dtb_ref[...]

    def chunk_terms(dt_ref, r):
        x_t = dt_ref[pl.ds(r, ch), :].T[:nr] + dt_bias_t
        dtp_t = jnp.maximum(x_t, 0.0) + jnp.log1p(jnp.exp(-jnp.abs(x_t)))
        da_t = dtp_t * neg_a_t
        p_t = jnp.dot(da_t, tri_t, preferred_element_type=F32, precision=lax.Precision.HIGHEST)
        return dtp_t, p_t, p_t - da_t

    def columns(p_t, e_t):
        tot = p_t[:, ch - 1:ch]
        fac_t = jnp.exp2(jnp.where(row_id < hpg, p_t, tot - e_t))
        pad = jnp.zeros((ch - 3 * nr, ch), F32)
        return jnp.concatenate([p_t, e_t, fac_t, pad], axis=0).T

    def pair_sel(a, b):
        return jnp.where(head0, a, b)

    def block_diag_rhs(x_pair):
        zero = jnp.zeros_like(x_pair)
        return jnp.concatenate([jnp.where(head0, x_pair, zero), jnp.where(head0, zero, x_pair)], axis=0)

    def state_update(st_ref, bt, x_ref, r, w_rows, tot):
        for q in range(npair):
            j0, j1 = 2 * q, 2 * q + 1
            lhs = jnp.concatenate([(bt * w_rows[j0]).astype(BF16), (bt * w_rows[j1]).astype(BF16)], axis=1)
            rhs = block_diag_rhs(x_ref[pl.ds(r, ch), q * LANES:(q + 1) * LANES])
            new = jnp.dot(lhs, rhs, preferred_element_type=F32)
            dec = pair_sel(jnp.exp2(tot[j0]), jnp.exp2(tot[j1]))
            cols = slice(q * LANES, (q + 1) * LANES)
            st_ref[:, cols] = st_ref[:, cols] * dec + new

    def fwd_state_terms(dtp_t, p_t):
        w_rows, tot = [], []
        for j in range(hpg):
            t = p_t[j:j + 1, ch - 1:ch]
            tot.append(t)
            w_rows.append(jnp.exp2(t - p_t[j:j + 1, :]) * dtp_t[j:j + 1, :])
        return w_rows, tot

    def bwd_state_terms(dtp_t, p_t, e_t):
        w_rows, tot = [], []
        for j in range(hpg, 2 * hpg):
            tot.append(p_t[j:j + 1, ch - 1:ch])
            w_rows.append(jnp.exp2(e_t[j:j + 1, :]) * dtp_t[j:j + 1, :])
        return w_rows, tot

    stf[...] = jnp.zeros(stf.shape, F32)
    stb[...] = jnp.zeros(stb.shape, F32)
    n_cc = ctx_len // ch

    def load_terms(c):
        terms = terms_s[c]
        return terms[:nr], terms[nr:2 * nr], terms[2 * nr:]

    def ctx_fwd(c, carry):
        r = pl.multiple_of(c * ch, ch)
        dtp_t, p_t, e_t = chunk_terms(dtc_ref, r)
        terms_s[c] = jnp.concatenate([dtp_t, p_t, e_t], axis=0)
        bt = bcc_s[pl.ds(r, ch), :].astype(F32).T
        bt_s[c] = bt
        w_rows, tot = fwd_state_terms(dtp_t, p_t)
        state_update(stf, bt, xsc_s, r, w_rows, tot)
        return carry

    def ctx_bwd(i, carry):
        c = n_cc - 1 - i
        r = pl.multiple_of(c * ch, ch)
        dtp_t, p_t, e_t = load_terms(c)
        w_rows, tot = bwd_state_terms(dtp_t, p_t, e_t)
        state_update(stb, bt_s[c], xsc_s, r, w_rows, tot)
        return carry

    lax.fori_loop(0, n_cc, ctx_fwd, 0)
    lax.fori_loop(0, n_cc, ctx_bwd, 0)

    n_lc = seq // ch

    def lat_fwd(c, carry):
        r = pl.multiple_of(c * ch, ch)
        dtp_t, p_t, e_t = chunk_terms(dtl_ref, r)
        terms_s[c] = jnp.concatenate([dtp_t, p_t, e_t], axis=0)
        pe = columns(p_t, e_t)
        bm = b_s[pl.ds(r, ch), :]
        cm = c_s[pl.ds(r, ch), :]
        bt = bm.astype(F32).T
        bt_s[c] = bt
        cb = lax.dot_general(cm, bm, (((1,), (1,)), ((), ())), preferred_element_type=F32)
        y_off = jnp.dot(cm, stf[...].astype(BF16), preferred_element_type=F32)
        pe16 = pe.astype(BF16)
        y_off = y_off * jnp.dot(pe16, spread_f, preferred_element_type=F32)
        eb_s[pl.ds(r, ch), :] = jnp.dot(pe16, spread_b, preferred_element_type=F32)
        lg_t = jnp.log2(dtp_t)
        row_f = p_t - lg_t
        row_b = e_t + lg_t
        for q in range(npair):
            ms = []
            for j in (2 * q, 2 * q + 1):
                jb = hpg + j
                pcol = jnp.broadcast_to(p_t[j:j + 1, :], (ch, ch)).T
                ecol = jnp.broadcast_to(e_t[jb:jb + 1, :], (ch, ch)).T
                t = jnp.where(lower, pcol - row_f[j:j + 1, :], row_b[jb:jb + 1, :] - ecol)
                ms.append((cb * (jnp.exp2(t) + jnp.where(diag, dtp_t[jb:jb + 1, :], 0.0))).astype(BF16))
            cols = slice(q * LANES, (q + 1) * LANES)
            rhs = block_diag_rhs(xs_s[pl.ds(r, ch), cols])
            y = jnp.dot(jnp.concatenate(ms, axis=1), rhs, preferred_element_type=F32)
            y_s[pl.ds(r, ch), cols] = y + y_off[:, cols]
        w_rows, tot = fwd_state_terms(dtp_t, p_t)
        state_update(stf, bt, xs_s, r, w_rows, tot)
        return carry

    lax.fori_loop(0, n_lc, lat_fwd, 0, unroll=2)

    dskip = dsk_ref[...]
    nrm = nrm_ref[...]

    def lat_bwd(i, carry):
        c = n_lc - 1 - i
        r = pl.multiple_of(c * ch, ch)
        dtp_t, p_t, e_t = load_terms(c)
        y_off = jnp.dot(c_s[pl.ds(r, ch), :], stb[...].astype(BF16), preferred_element_type=F32)
        y = y_s[pl.ds(r, ch), :] + y_off * eb_s[pl.ds(r, ch), :]
        w_rows, tot = bwd_state_terms(dtp_t, p_t, e_t)
        state_update(stb, bt_s[c], xs_s, r, w_rows, tot)
        y = (y + xs_s[pl.ds(r, ch), :].astype(F32) * dskip) * _silu(zl_ref[pl.ds(r, ch), :].astype(F32))
        y = y * lax.rsqrt(jnp.mean(y * y, axis=-1, keepdims=True) + RMS_EPS) * nrm
        o_ref[pl.ds(r, ch), :] = y.astype(o_ref.dtype)
        return carry

    lax.fori_loop(0, n_lc, lat_bwd, 0)


def _ssd(proj, dt, projc, dtc, conv_w, conv_b, alog_g, dtb_g, dskip_c, ssd_norm, *,
         batch, seq, ctx_len, attn_width):
    ssd_width = _ssd_width(attn_width)
    n_kv = attn_width // HEAD_DIM // Q_PER_KV
    kv_width = n_kv * HEAD_DIM
    gw = ssd_width // N_SSD_GROUPS
    hpg = gw // SSD_HEAD_DIM
    xs_off = 3 * attn_width + 2 * kv_width
    xs_off_c = 2 * kv_width
    assert xs_off % gw == 0 and xs_off_c % gw == 0 and D_STATE == LANES
    nb = N_SSD_GROUPS
    b_cb = (xs_off + ssd_width) // LANES
    c_cb = b_cb + nb
    b_cb_c = (xs_off_c + ssd_width) // LANES
    cw_b = ssd_width // LANES
    nr = alog_g.shape[1]
    in_specs = [
        pl.BlockSpec((seq, gw), lambda b, g: (b, xs_off // gw + g)),
        pl.BlockSpec((seq, LANES), lambda b, g: (b, b_cb + g)),
        pl.BlockSpec((seq, LANES), lambda b, g: (b, c_cb + g)),
        pl.BlockSpec((seq, gw), lambda b, g: (b, 2 * attn_width // gw + g)),
        pl.BlockSpec((seq, LANES), lambda b, g: (b, g)),
        pl.BlockSpec((ctx_len, gw), lambda b, g: (b, xs_off_c // gw + g)),
        pl.BlockSpec((ctx_len, LANES), lambda b, g: (b, b_cb_c + g)),
        pl.BlockSpec((ctx_len, LANES), lambda b, g: (b, g)),
        pl.BlockSpec((D_CONV, gw), lambda b, g: (0, g)),
        pl.BlockSpec((D_CONV, LANES), lambda b, g: (0, cw_b + g)),
        pl.BlockSpec((D_CONV, LANES), lambda b, g: (0, cw_b + nb + g)),
        pl.BlockSpec((1, gw), lambda b, g: (0, g)),
        pl.BlockSpec((1, LANES), lambda b, g: (0, cw_b + g)),
        pl.BlockSpec((1, LANES), lambda b, g: (0, cw_b + nb + g)),
        pl.BlockSpec((None, nr, LANES), lambda b, g: (g, 0, 0)),
        pl.BlockSpec((None, nr, LANES), lambda b, g: (g, 0, 0)),
        pl.BlockSpec((1, gw), lambda b, g: (0, g)),
        pl.BlockSpec((1, gw), lambda b, g: (0, g)),
    ]
    pad_rows = max(seq, ctx_len) + 2 * CONV_HALO
    n_chunks = max(seq, ctx_len) // SSD_CHUNK
    return pl.pallas_call(
        functools.partial(_ssd_kernel, hpg=hpg),
        grid=(batch, N_SSD_GROUPS),
        in_specs=in_specs,
        out_specs=pl.BlockSpec((seq, gw), lambda b, g: (b, g)),
        out_shape=jax.ShapeDtypeStruct((batch * seq, ssd_width), BF16),
        scratch_shapes=[pltpu.VMEM((pad_rows, gw + 2 * D_STATE), BF16),
                        pltpu.VMEM((seq, gw), BF16),
                        pltpu.VMEM((seq, LANES), BF16),
                        pltpu.VMEM((seq, LANES), BF16),
                        pltpu.VMEM((ctx_len, gw), BF16),
                        pltpu.VMEM((ctx_len, LANES), BF16),
                        pltpu.VMEM((seq, gw), F32),
                        pltpu.VMEM((D_STATE, gw), F32),
                        pltpu.VMEM((D_STATE, gw), F32),
                        pltpu.VMEM((n_chunks, 3 * nr, SSD_CHUNK), F32),
                        pltpu.VMEM((n_chunks, D_STATE, SSD_CHUNK), F32),
                        pltpu.VMEM((seq, gw), F32)],
        compiler_params=_cparams("parallel", "parallel"),
        name="ssd_bidir",
    )(proj, proj, proj, proj, dt, projc, projc, dtc,
      conv_w, conv_w, conv_w, conv_b, conv_b, conv_b, alog_g, dtb_g, dskip_c, ssd_norm)


def _outproj_kernel(a_ref, y_ref, w0_ref, w1_ref, x_ref, gate_ref, o_ref):
    acc = jnp.dot(a_ref[...], w0_ref[...], preferred_element_type=F32)
    acc = acc + jnp.dot(y_ref[...], w1_ref[...], preferred_element_type=F32)
    o_ref[...] = x_ref[...] + gate_ref[...] * acc


def _outproj(lhs0, cb0, lhs1, cb1, w, x2d, mod, *, rows_per_mod):
    m, d = x2d.shape
    kh = w.shape[0] // 2
    tm = _tile(rows_per_mod, 1024)
    tn = _tile(d, 1024)
    tiles_per_mod = rows_per_mod // tm
    return pl.pallas_call(
        _outproj_kernel,
        grid=(m // tm, d // tn),
        in_specs=[pl.BlockSpec((tm, kh), lambda i, j: (i, cb0)),
                  pl.BlockSpec((tm, kh), lambda i, j: (i, cb1)),
                  pl.BlockSpec((kh, tn), lambda i, j: (0, j)),
                  pl.BlockSpec((kh, tn), lambda i, j: (1, j)),
                  pl.BlockSpec((tm, tn), lambda i, j: (i, j)),
                  pl.BlockSpec((None, 1, tn), lambda i, j: (3 * (i // tiles_per_mod) + 2, 0, j))],
        out_specs=pl.BlockSpec((tm, tn), lambda i, j: (i, j)),
        out_shape=jax.ShapeDtypeStruct((m, d), F32),
        compiler_params=_cparams("parallel", "arbitrary"),
        name="outproj_residual",
    )(lhs0, lhs1, w, w, x2d, mod)


FFT_RADIX = 8


def _cmul_root(z, k, n):
    re, im = z
    eighth = (8 * k) // n % 8
    assert (8 * k) % n == 0
    r = math.sqrt(0.5)
    if eighth == 0:
        return re, im
    if eighth == 1:
        return (re + im) * r, (im - re) * r
    if eighth == 2:
        return im, -re
    if eighth == 3:
        return (im - re) * r, -(re + im) * r
    nre, nim = _cmul_root(z, k - n // 2, n)
    return -nre, -nim


def _fft_small(v):
    n = len(v)
    if n == 1:
        return v
    ev, od = _fft_small(v[0::2]), _fft_small(v[1::2])
    out = [None] * n
    for k in range(n // 2):
        tr, ti = _cmul_root(od[k], k, n)
        out[k] = (ev[k][0] + tr, ev[k][1] + ti)
        out[k + n // 2] = (ev[k][0] - tr, ev[k][1] - ti)
    return out


def _fourier_kernel(u_ref, z_ref, w_ref, wc_ref, o_ref, uf_scr, y_scr, ab_scr, *, rows_per_step):
    seq, tn = u_ref.shape
    n2 = seq // FFT_RADIX
    rs = rows_per_step

    nslab = tn // LANES

    def widen(i, carry):
        r = pl.multiple_of(i * LANES, LANES)
        for c in range(nslab):
            uf_scr[c, pl.ds(r, LANES), :] = u_ref[pl.ds(r, LANES), c * LANES:(c + 1) * LANES].astype(F32)
        return carry

    lax.fori_loop(0, seq // LANES, widen, 0)
    for a in range(FFT_RADIX):
        ua = jnp.concatenate([uf_scr[c, pl.ds(a, n2, stride=FFT_RADIX), :] for c in range(nslab)], axis=1)
        y_scr[a] = jnp.dot(w_ref[a], ua.astype(BF16), preferred_element_type=F32)

    def combine(i, carry):
        r = pl.multiple_of(i * rs, rs)
        z = [(y_scr[a, pl.ds(r, rs), :], y_scr[a, pl.ds(n2 + r, rs), :]) for a in range(FFT_RADIX)]
        x = _fft_small(z)
        for k1 in range(FFT_RADIX):
            ab_scr[pl.ds(k1 * n2 + r, rs), 0:tn] = x[k1][0].astype(BF16)
            ab_scr[pl.ds(k1 * n2 + r, rs), tn:2 * tn] = (-x[k1][1]).astype(BF16)
        return carry

    lax.fori_loop(0, n2 // rs, combine, 0)

    tm = min(seq, 4 * LANES)

    def channels(i, carry):
        r = pl.multiple_of(i * tm, tm)
        f = jnp.dot(ab_scr[pl.ds(r, tm), :], wc_ref[...], preferred_element_type=F32)
        o_ref[pl.ds(r, tm), :] = (f * _silu(z_ref[pl.ds(r, tm), :].astype(F32))).astype(o_ref.dtype)
        return carry

    lax.fori_loop(0, seq // tm, channels, 0)


def _fourier_gate(uz, *, batch, seq, width):
    n2 = seq // FFT_RADIX
    gw = width // N_FOURIER_GROUPS
    k2 = np.arange(n2, dtype=np.int64)[None, :, None]
    pos = FFT_RADIX * np.arange(n2, dtype=np.int64)[None, None, :] + np.arange(FFT_RADIX, dtype=np.int64)[:, None, None]
    ang = 2.0 * np.pi * ((k2 * pos) % seq).astype(np.float64) / seq
    w = jnp.asarray((np.concatenate([np.cos(ang), -np.sin(ang)], axis=1) / np.sqrt(seq)).astype(np.float32))
    kc = np.arange(gw, dtype=np.int64)
    angc = 2.0 * np.pi * ((kc[:, None] * kc[None, :]) % gw).astype(np.float64) / gw
    wc = jnp.asarray((np.concatenate([np.cos(angc), -np.sin(angc)], axis=0) / np.sqrt(gw)).astype(np.float32))
    return pl.pallas_call(
        functools.partial(_fourier_kernel, rows_per_step=2 * SUBLANES),
        grid=(batch, N_FOURIER_GROUPS),
        in_specs=[pl.BlockSpec((seq, gw), lambda b, g: (b, g)),
                  pl.BlockSpec((seq, gw), lambda b, g: (b, N_FOURIER_GROUPS + g)),
                  pl.BlockSpec((FFT_RADIX, 2 * n2, n2), lambda b, g: (0, 0, 0)),
                  pl.BlockSpec((2 * gw, gw), lambda b, g: (0, 0))],
        out_specs=pl.BlockSpec((seq, gw), lambda b, g: (b, g)),
        out_shape=jax.ShapeDtypeStruct((batch * seq, width), BF16),
        scratch_shapes=[pltpu.VMEM((gw // LANES, seq, LANES), F32),
                        pltpu.VMEM((FFT_RADIX, 2 * n2, gw), F32),
                        pltpu.VMEM((seq, 2 * gw), BF16)],
        compiler_params=_cparams("parallel", "parallel"),
        name="fourier_mix_gate",
    )(uz, uz, w.astype(BF16), wc.astype(BF16))


def _rope_tables(seq):
    pos = jnp.arange(seq)
    row = (pos // GRID_W).astype(F32)
    col = (pos % GRID_W).astype(F32)
    inv_freq = ROPE_THETA ** (-jnp.arange(0, ROPE_AXIS_DIM, 2, dtype=F32) / ROPE_AXIS_DIM)
    ar, ac = row[:, None] * inv_freq, col[:, None] * inv_freq
    cos_t = jnp.concatenate([jnp.cos(ar), jnp.cos(ar), jnp.cos(ac), jnp.cos(ac)], axis=-1)
    sin_t = jnp.concatenate([-jnp.sin(ar), jnp.sin(ar), -jnp.sin(ac), jnp.sin(ac)], axis=-1)
    return cos_t, sin_t


def _layer(a, j):
    return a.reshape(a.shape[1:]) if a.shape[0] == 1 else a[j]


def _per_group_rows(v, hpg):
    g = v.shape[1] // hpg
    rows = -(-2 * hpg // SUBLANES) * SUBLANES
    t = v.reshape(2, g, hpg).transpose(1, 0, 2).reshape(g, 2 * hpg)
    t = jnp.pad(t, ((0, 0), (0, rows - 2 * hpg)))
    return jnp.broadcast_to(t[:, :, None], (g, rows, LANES))


def kernel(x, c, ctx, c_ctx, ev_norm_w, ev_ada_w, ev_ada_b, ev_w_in, ev_q_norm, ev_k_norm, ev_conv_w,
           ev_conv_b, ev_a_log, ev_dt_bias, ev_d_skip, ev_ssd_norm, ev_w_out, od_norm_w, od_ada_w,
           od_ada_b, od_w_in, od_w_out):
    batch, seq, d = x.shape
    ctx_len = ctx.shape[1]
    mix = ev_w_out.shape[1]
    attn_width = mix // 2
    ssd_width = mix - attn_width
    n_ssd_heads = ssd_width // SSD_HEAD_DIM
    hpg = n_ssd_heads // N_SSD_GROUPS
    n_kv = attn_width // HEAD_DIM // Q_PER_KV
    kv_width = n_kv * HEAD_DIM
    conv_ch = ssd_width + 2 * N_SSD_GROUPS * D_STATE
    ctx_col0 = 2 * attn_width + ssd_width
    n_main = ctx_col0 + 2 * kv_width + conv_ch
    m = batch * seq
    x2d = x.reshape(m, d)
    ctx2d = ctx.reshape(batch * ctx_len, d)
    n_mod_rows = 2 * SUBLANES
    assert batch + 1 <= n_mod_rows

    w_in = _layer(ev_w_in, 0)
    w_main = w_in.astype(BF16)
    qscale = LOG2E * HEAD_DIM ** -0.5
    q_gain = (_layer(ev_q_norm, 0) * qscale).reshape(1, HEAD_DIM)
    k_gain = _layer(ev_k_norm, 0).reshape(1, HEAD_DIM)
    cos_t, sin_t = _rope_tables(seq)
    w_dt = w_in[:, n_main:].reshape(d, 2, N_SSD_GROUPS, hpg).transpose(0, 2, 1, 3)
    w_dt = jnp.pad(w_dt.reshape(d, N_SSD_GROUPS, 2 * hpg), ((0, 0), (0, 0), (0, LANES - 2 * hpg)))
    w_dt = w_dt.reshape(d, N_SSD_GROUPS * LANES).astype(BF16)
    cond = jnp.zeros((n_mod_rows, d), F32).at[:batch].set(c).at[batch].set(c_ctx)
    mod0 = _ada(cond, _layer(ev_ada_w, 0), _layer(ev_ada_b, 0)).reshape(3 * n_mod_rows, 1, d)
    norm_w0 = _layer(ev_norm_w, 0)

    proj, dt = _inproj(x2d, mod0, norm_w0, w_main, w_dt,
                       rows_per_mod=seq, mod_row0=0, col0=0, ncols=n_main)
    projc, dtc = _inproj(ctx2d, mod0, norm_w0, w_main, w_dt,
                         rows_per_mod=batch * ctx_len, mod_row0=batch, col0=ctx_col0, ncols=n_main - ctx_col0)

    attn = _attention(proj, projc, q_gain, k_gain, cos_t, sin_t,
                      batch=batch, seq=seq, ctx_len=ctx_len, attn_width=attn_width)
    yssd = _ssd(proj, dt, projc, dtc, _layer(ev_conv_w, 0), _layer(ev_conv_b, 0).reshape(1, conv_ch),
                _per_group_rows(_layer(ev_a_log, 0), hpg), _per_group_rows(_layer(ev_dt_bias, 0), hpg),
                jnp.repeat(_layer(ev_d_skip, 0), SSD_HEAD_DIM).reshape(1, ssd_width),
                _layer(ev_ssd_norm, 0).reshape(1, ssd_width),
                batch=batch, seq=seq, ctx_len=ctx_len, attn_width=attn_width)
    x1 = _outproj(attn, 0, yssd, 0, _layer(ev_w_out, 0).astype(BF16), x2d, mod0, rows_per_mod=seq)

    fw = od_w_out.shape[1]
    cond1 = jnp.zeros((n_mod_rows, d), F32).at[:batch].set(c)
    mod1 = _ada(cond1, _layer(od_ada_w, 0), _layer(od_ada_b, 0)).reshape(3 * n_mod_rows, 1, d)
    uz, _ = _inproj(x1, mod1, _layer(od_norm_w, 0), _layer(od_w_in, 0).astype(BF16), None,
                    rows_per_mod=seq, mod_row0=0, col0=0, ncols=2 * fw)
    fg = _fourier_gate(uz, batch=batch, seq=seq, width=fw)
    x2 = _outproj(fg, 0, fg, 1, _layer(od_w_out, 0).astype(BF16), x1, mod1, rows_per_mod=seq)
    return x2.reshape(batch, seq, d)
```

```python
import functools
import math

import numpy as np
import jax
import jax.numpy as jnp
from jax import lax
from jax.experimental import pallas as pl
from jax.experimental.pallas import tpu as pltpu

F32 = jnp.float32
BF16 = jnp.bfloat16

GRID_W = 64
HEAD_DIM = 128
Q_PER_KV = 4
ROPE_THETA = 10000.0
ROPE_AXIS_DIM = HEAD_DIM // 2
SSD_HEAD_DIM = 64
N_SSD_GROUPS = 4
D_STATE = 128
D_CONV = 5
SSD_CHUNK = 128
N_FOURIER_GROUPS = 8
RMS_EPS = 1e-6

LANES = 128
SUBLANES = 8
VMEM_LIMIT_BYTES = 56 * 1024 * 1024

CONV_HALO = SSD_CHUNK // 2
LOG2E = 1.4426950408889634


def _cparams(*sem):
    return pltpu.CompilerParams(dimension_semantics=sem, vmem_limit_bytes=VMEM_LIMIT_BYTES)


def _tile(n, pref):
    t = min(n, pref)
    while n % t:
        t //= 2
    return t


def _silu(v):
    return v * jax.nn.sigmoid(v)


def _ada_kernel(c_ref, w_ref, b_ref, o_ref):
    c = c_ref[...]
    s = _silu(c).astype(BF16)
    o_ref[...] = jnp.dot(s, w_ref[...].astype(BF16), preferred_element_type=F32) + b_ref[...]


def _ada(cond, w, b):
    r, d = cond.shape
    n = w.shape[1]
    tn = _tile(n, 1024)
    return pl.pallas_call(
        _ada_kernel,
        grid=(n // tn,),
        in_specs=[pl.BlockSpec((r, d), lambda j: (0, 0)),
                  pl.BlockSpec((d, tn), lambda j: (0, j)),
                  pl.BlockSpec((1, tn), lambda j: (0, j))],
        out_specs=pl.BlockSpec((r, tn), lambda j: (0, j)),
        out_shape=jax.ShapeDtypeStruct((r, n), F32),
        compiler_params=_cparams("parallel"),
        name="ada_mod",
    )(cond, w, b.reshape(1, n))


def _inproj_kernel(x_ref, sh_ref, sc_ref, nw_ref, w_ref, *rest, has_dt, sub):
    if has_dt:
        wdt_ref, o_ref, dt_ref, h_ref = rest
    else:
        o_ref, h_ref = rest
    tm = x_ref.shape[0]

    @pl.when(pl.program_id(1) == 0)
    def _():
        nw = nw_ref[...]
        sc = 1.0 + sc_ref[...]
        sh = sh_ref[...]

        def body(s, carry):
            r = pl.multiple_of(s * sub, sub)
            x = x_ref[pl.ds(r, sub), :]
            y = x * lax.rsqrt(jnp.mean(x * x, axis=-1, keepdims=True) + RMS_EPS) * nw
            h_ref[pl.ds(r, sub), :] = (y * sc + sh).astype(BF16)
            return carry

        lax.fori_loop(0, tm // sub, body, 0, unroll=4)
        if has_dt:
            dt_ref[...] = jnp.dot(h_ref[...], wdt_ref[...], preferred_element_type=F32)

    o_ref[...] = jnp.dot(h_ref[...], w_ref[...], preferred_element_type=F32).astype(o_ref.dtype)


def _inproj(x2d, mod, norm_w, w, wdt, *, rows_per_mod, mod_row0, col0, ncols):
    m, d = x2d.shape
    tm = _tile(rows_per_mod, 1024)
    has_dt = wdt is not None
    tn = _tile(math.gcd(ncols, col0), 2048)
    cb0 = col0 // tn
    tiles_per_mod = rows_per_mod // tm

    def modrow(i):
        return mod_row0 + i // tiles_per_mod

    in_specs = [pl.BlockSpec((tm, d), lambda i, j: (i, 0)),
                pl.BlockSpec((None, 1, d), lambda i, j: (3 * modrow(i), 0, 0)),
                pl.BlockSpec((None, 1, d), lambda i, j: (3 * modrow(i) + 1, 0, 0)),
                pl.BlockSpec((1, d), lambda i, j: (0, 0)),
                pl.BlockSpec((d, tn), lambda i, j: (0, cb0 + j))]
    args = [x2d, mod, mod, norm_w.reshape(1, d), w]
    out_specs = [pl.BlockSpec((tm, tn), lambda i, j: (i, j))]
    out_shape = [jax.ShapeDtypeStruct((m, ncols), BF16)]
    if has_dt:
        ndt = wdt.shape[1]
        in_specs.append(pl.BlockSpec((d, ndt), lambda i, j: (0, 0)))
        args.append(wdt)
        out_specs.append(pl.BlockSpec((tm, ndt), lambda i, j: (i, 0)))
        out_shape.append(jax.ShapeDtypeStruct((m, ndt), F32))
    res = pl.pallas_call(
        functools.partial(_inproj_kernel, has_dt=has_dt, sub=min(32, tm)),
        grid=(m // tm, ncols // tn),
        in_specs=in_specs,
        out_specs=out_specs,
        out_shape=out_shape,
        scratch_shapes=[pltpu.VMEM((tm, d), BF16)],
        compiler_params=_cparams("parallel", "arbitrary"),
        name="mod_inproj",
    )(*args)
    return (res[0], res[1]) if has_dt else (res[0], None)


def _norm_rope(v, gain, cos, sin):
    ones = jnp.ones((HEAD_DIM, HEAD_DIM), BF16)
    ss = jnp.dot((v * v).astype(BF16), ones, preferred_element_type=F32)
    y = v * lax.rsqrt(ss * (1.0 / HEAD_DIM) + RMS_EPS) * gain
    if cos is not None:
        quarter = ROPE_AXIS_DIM // 2
        lane = lax.broadcasted_iota(jnp.int32, (1, HEAD_DIM), 1)
        partner = jnp.where(lane % ROPE_AXIS_DIM < quarter,
                            pltpu.roll(y, HEAD_DIM - quarter, 1), pltpu.roll(y, quarter, 1))
        y = y * cos + partner * sin
    return y.astype(BF16)


def _attn_kernel(q_ref, g_ref, kl_ref, vl_ref, kc_ref, vc_ref, qg_ref, kg_ref, cq_ref, sq_ref, ck_ref, sk_ref,
                 o_ref, k_scr, s_scr, *, nrep, tk):
    tq = q_ref.shape[0]
    rows = nrep * tq
    blocks, col = [], 0
    for v_ref in (vc_ref, vl_ref):
        n = v_ref.shape[0]
        for r0 in range(0, n, tk):
            blocks.append((v_ref, r0, min(tk, n - r0), col))
            col += min(tk, n - r0)

    @pl.when(pl.program_id(2) == 0)
    def _prep_keys():
        kg = kg_ref[...]
        for v_ref, r0, size, c0 in blocks:
            if v_ref is vc_ref:
                k_scr[pl.ds(c0, size), :] = _norm_rope(kc_ref[pl.ds(r0, size), :].astype(F32), kg, None, None)
            else:
                k_scr[pl.ds(c0, size), :] = _norm_rope(kl_ref[pl.ds(r0, size), :].astype(F32), kg,
                                                       ck_ref[pl.ds(r0, size), :], sk_ref[pl.ds(r0, size), :])

    qg = qg_ref[...]
    qs = jnp.concatenate([_norm_rope(q_ref[:, r * HEAD_DIM:(r + 1) * HEAD_DIM].astype(F32), qg,
                                     cq_ref[...], sq_ref[...]) for r in range(nrep)], axis=0)

    half = rows // 2
    halves = (slice(0, half), slice(half, rows))
    m = [None, None]
    for _, r0, size, c0 in blocks:
        kb = k_scr[pl.ds(c0, size), :]
        for i, hs in enumerate(halves):
            s = lax.dot_general(qs[hs], kb, (((1,), (1,)), ((), ())), preferred_element_type=F32)
            s_scr[hs, c0:c0 + size] = s
            for c in range(size // LANES):
                sc = s[:, c * LANES:(c + 1) * LANES]
                m[i] = sc if m[i] is None else jnp.maximum(m[i], sc)
    mb = [jnp.broadcast_to(jnp.max(mi, axis=-1, keepdims=True), (half, LANES)) for mi in m]

    acc = [jnp.zeros((half, HEAD_DIM + LANES), F32) for _ in halves]
    for v_ref, r0, size, c0 in blocks:
        v_aug = jnp.concatenate([v_ref[pl.ds(r0, size), :], jnp.ones((size, LANES), BF16)], axis=1)
        for i, hs in enumerate(halves):
            p = jnp.concatenate([jnp.exp2(s_scr[hs, c0 + c * LANES:c0 + (c + 1) * LANES] - mb[i])
                                 for c in range(size // LANES)], axis=1).astype(BF16)
            acc[i] = acc[i] + jnp.dot(p, v_aug, preferred_element_type=F32)
    acc = jnp.concatenate(acc, axis=0)
    o = acc[:, :HEAD_DIM] / acc[:, HEAD_DIM:]
    for r in range(nrep):
        g = g_ref[:, r * HEAD_DIM:(r + 1) * HEAD_DIM].astype(F32)
        o_ref[:, r * HEAD_DIM:(r + 1) * HEAD_DIM] = (o[r * tq:(r + 1) * tq] * _silu(g)).astype(o_ref.dtype)


def _attention(proj, projc, q_gain, k_gain, cos_t, sin_t, *, batch, seq, ctx_len, attn_width):
    n_kv = attn_width // HEAD_DIM // Q_PER_KV
    gw = Q_PER_KV * HEAD_DIM
    tq = _tile(seq, 512)
    tk = 4 * LANES
    assert ctx_len % LANES == 0 and seq % LANES == 0
    tpb = seq // tq
    k_cb = (2 * attn_width + _ssd_width(attn_width)) // HEAD_DIM
    v_cb = k_cb + n_kv
    return pl.pallas_call(
        functools.partial(_attn_kernel, nrep=Q_PER_KV, tk=tk),
        grid=(batch, n_kv, tpb),
        in_specs=[pl.BlockSpec((tq, gw), lambda b, h, t: (b * tpb + t, h)),
                  pl.BlockSpec((tq, gw), lambda b, h, t: (b * tpb + t, attn_width // gw + h)),
                  pl.BlockSpec((seq, HEAD_DIM), lambda b, h, t: (b, k_cb + h)),
                  pl.BlockSpec((seq, HEAD_DIM), lambda b, h, t: (b, v_cb + h)),
                  pl.BlockSpec((ctx_len, HEAD_DIM), lambda b, h, t: (b, h)),
                  pl.BlockSpec((ctx_len, HEAD_DIM), lambda b, h, t: (b, n_kv + h)),
                  pl.BlockSpec((1, HEAD_DIM), lambda b, h, t: (0, 0)),
                  pl.BlockSpec((1, HEAD_DIM), lambda b, h, t: (0, 0)),
                  pl.BlockSpec((tq, HEAD_DIM), lambda b, h, t: (t, 0)),
                  pl.BlockSpec((tq, HEAD_DIM), lambda b, h, t: (t, 0)),
                  pl.BlockSpec((seq, HEAD_DIM), lambda b, h, t: (0, 0)),
                  pl.BlockSpec((seq, HEAD_DIM), lambda b, h, t: (0, 0))],
        out_specs=pl.BlockSpec((tq, gw), lambda b, h, t: (b * tpb + t, h)),
        out_shape=jax.ShapeDtypeStruct((batch * seq, attn_width), BF16),
        scratch_shapes=[pltpu.VMEM((ctx_len + seq, HEAD_DIM), BF16),
                        pltpu.VMEM((Q_PER_KV * tq, ctx_len + seq), F32)],
        compiler_params=_cparams("parallel", "parallel", "arbitrary"),
        name="gqa_attention",
    )(proj, proj, proj, proj, projc, projc, q_gain, k_gain, cos_t, sin_t, cos_t, sin_t)


def _ssd_width(attn_width):
    return attn_width


def _conv_silu(groups, pad_ref, n_rows):
    ch = SSD_CHUNK
    halo = CONV_HALO
    win_rows = ch + 2 * halo
    widths = [g[0].shape[1] for g in groups]
    offs = [sum(widths[:i]) for i in range(len(groups))]
    total = sum(widths)
    pad_ref[pl.ds(0, halo), 0:total] = jnp.zeros((halo, total), BF16)
    pad_ref[pl.ds(halo + n_rows, halo), 0:total] = jnp.zeros((halo, total), BF16)

    def copy(i, carry):
        r = pl.multiple_of(i * ch, ch)
        for (src_ref, _, _, _), o, cw in zip(groups, offs, widths):
            pad_ref[pl.ds(r + halo, ch), o:o + cw] = src_ref[pl.ds(r, ch), :]
        return carry

    lax.fori_loop(0, n_rows // ch, copy, 0)
    w = jnp.concatenate([g[1][...] for g in groups], axis=1)
    bias = jnp.concatenate([g[2][...] for g in groups], axis=1)
    out_row = lax.broadcasted_iota(jnp.int32, (ch, win_rows), 0)
    win_row = lax.broadcasted_iota(jnp.int32, (ch, win_rows), 1)
    taps = [k for k in range(D_CONV) if k != D_CONV // 2]
    selects = [jnp.where(win_row == out_row + (halo + k - D_CONV // 2), 1.0, 0.0).astype(BF16) for k in taps]

    blk = 2 * LANES

    def body(i, carry):
        r = pl.multiple_of(i * ch, ch)
        ys = []
        for c0 in range(0, total, blk):
            cs = slice(c0, min(c0 + blk, total))
            win = pad_ref[pl.ds(r, win_rows), cs]
            acc = bias[:, cs] + win[halo:halo + ch].astype(F32) * w[D_CONV // 2:D_CONV // 2 + 1, cs]
            for k, select in zip(taps, selects):
                acc = acc + jnp.dot(select, win, preferred_element_type=F32) * w[k:k + 1, cs]
            ys.append(_silu(acc))
        y = jnp.concatenate(ys, axis=1)
        for (_, _, _, dst_ref), o, cw in zip(groups, offs, widths):
            dst_ref[pl.ds(r, ch), :] = y[:, o:o + cw].astype(dst_ref.dtype)
        return carry

    lax.fori_loop(0, n_rows // ch, body, 0, unroll=2)


def _ssd_kernel(xl_ref, bl_ref, cl_ref, zl_ref, dtl_ref, xc_ref, bc_ref, dtc_ref,
                wx_ref, wb_ref, wc_ref, bx_ref, bb_ref, bcv_ref, alog_ref, dtb_ref, dsk_ref, nrm_ref,
                o_ref,
                pad_s, xs_s, b_s, c_s, xsc_s, bcc_s, y_s, stf, stb, terms_s, bt_s, eb_s, *, hpg):
    ch = SSD_CHUNK
    seq = xl_ref.shape[0]
    ctx_len = xc_ref.shape[0]
    gw = xl_ref.shape[1]
    npair = gw // LANES
    assert 2 * SSD_HEAD_DIM == LANES and hpg == 2 * npair

    _conv_silu([(xl_ref, wx_ref, bx_ref, xs_s), (bl_ref, wb_ref, bb_ref, b_s), (cl_ref, wc_ref, bcv_ref, c_s)],
               pad_s, seq)
    _conv_silu([(xc_ref, wx_ref, bx_ref, xsc_s), (bc_ref, wb_ref, bb_ref, bcc_s)], pad_s, ctx_len)

    li = lax.broadcasted_iota(jnp.int32, (ch, ch), 0)
    si = lax.broadcasted_iota(jnp.int32, (ch, ch), 1)
    tri_t = (li <= si).astype(F32)
    nr = alog_ref.shape[0]
    lower = li >= si
    diag = li == si
    xr = lax.broadcasted_iota(jnp.int32, (ch, gw), 0)
    xh = lax.broadcasted_iota(jnp.int32, (ch, gw), 1) // SSD_HEAD_DIM
    spread_f = jnp.where(xr == 2 * nr + xh, 1.0, 0.0).astype(BF16)
    spread_b = jnp.where(xr == 2 * nr + hpg + xh, 1.0, 0.0).astype(BF16)
    row_id = lax.broadcasted_iota(jnp.int32, (nr, ch), 0)
    lane = lax.broadcasted_iota(jnp.int32, (1, LANES), 1)
    head0 = lane < SSD_HEAD_DIM
    neg_a_t = -jnp.exp(alog_ref[...]) * LOG2E
    dt_bias_t = dtb_ref[...]

    def chunk_terms(dt_ref, r):
        x_t = dt_ref[pl.ds(r, ch), :].T[:nr] + dt_bias_t
        dtp_t = jnp.maximum(x_t, 0.0) + jnp.log1p(jnp.exp(-jnp.abs(x_t)))
        da_t = dtp_t * neg_a_t
        p_t = jnp.dot(da_t, tri_t, preferred_element_type=F32, precision=lax.Precision.HIGHEST)
        return dtp_t, p_t, p_t - da_t

    def columns(p_t, e_t):
        tot = p_t[:, ch - 1:ch]
        fac_t = jnp.exp2(jnp.where(row_id < hpg, p_t, tot - e_t))
        pad = jnp.zeros((ch - 3 * nr, ch), F32)
        return jnp.concatenate([p_t, e_t, fac_t, pad], axis=0).T

    def pair_sel(a, b):
        return jnp.where(head0, a, b)

    def block_diag_rhs(x_pair):
        zero = jnp.zeros_like(x_pair)
        return jnp.concatenate([jnp.where(head0, x_pair, zero), jnp.where(head0, zero, x_pair)], axis=0)

    def state_update(st_ref, bt, x_ref, r, w_rows, tot):
        for q in range(npair):
            j0, j1 = 2 * q, 2 * q + 1
            lhs = jnp.concatenate([(bt * w_rows[j0]).astype(BF16), (bt * w_rows[j1]).astype(BF16)], axis=1)
            rhs = block_diag_rhs(x_ref[pl.ds(r, ch), q * LANES:(q + 1) * LANES])
            new = jnp.dot(lhs, rhs, preferred_element_type=F32)
            dec = pair_sel(jnp.exp2(tot[j0]), jnp.exp2(tot[j1]))
            cols = slice(q * LANES, (q + 1) * LANES)
            st_ref[:, cols] = st_ref[:, cols] * dec + new

    def fwd_state_terms(dtp_t, p_t):
        w_rows, tot = [], []
        for j in range(hpg):
            t = p_t[j:j + 1, ch - 1:ch]
            tot.append(t)
            w_rows.append(jnp.exp2(t - p_t[j:j + 1, :]) * dtp_t[j:j + 1, :])
        return w_rows, tot

    def bwd_state_terms(dtp_t, p_t, e_t):
        w_rows, tot = [], []
        for j in range(hpg, 2 * hpg):
            tot.append(p_t[j:j + 1, ch - 1:ch])
            w_rows.append(jnp.exp2(e_t[j:j + 1, :]) * dtp_t[j:j + 1, :])
        return w_rows, tot

    stf[...] = jnp.zeros(stf.shape, F32)
    stb[...] = jnp.zeros(stb.shape, F32)
    n_cc = ctx_len // ch

    def load_terms(c):
        terms = terms_s[c]
        return terms[:nr], terms[nr:2 * nr], terms[2 * nr:]

    def ctx_fwd(c, carry):
        r = pl.multiple_of(c * ch, ch)
        dtp_t, p_t, e_t = chunk_terms(dtc_ref, r)
        terms_s[c] = jnp.concatenate([dtp_t, p_t, e_t], axis=0)
        bt = bcc_s[pl.ds(r, ch), :].astype(F32).T
        bt_s[c] = bt
        w_rows, tot = fwd_state_terms(dtp_t, p_t)
        state_update(stf, bt, xsc_s, r, w_rows, tot)
        return carry

    def ctx_bwd(i, carry):
        c = n_cc - 1 - i
        r = pl.multiple_of(c * ch, ch)
        dtp_t, p_t, e_t = load_terms(c)
        w_rows, tot = bwd_state_terms(dtp_t, p_t, e_t)
        state_update(stb, bt_s[c], xsc_s, r, w_rows, tot)
        return carry

    lax.fori_loop(0, n_cc, ctx_fwd, 0, unroll=2)
    lax.fori_loop(0, n_cc, ctx_bwd, 0, unroll=2)

    n_lc = seq // ch

    def lat_fwd(c, carry):
        r = pl.multiple_of(c * ch, ch)
        dtp_t, p_t, e_t = chunk_terms(dtl_ref, r)
        terms_s[c] = jnp.concatenate([dtp_t, p_t, e_t], axis=0)
        pe = columns(p_t, e_t)
        bm = b_s[pl.ds(r, ch), :]
        cm = c_s[pl.ds(r, ch), :]
        bt = bm.astype(F32).T
        bt_s[c] = bt
        cb = lax.dot_general(cm, bm, (((1,), (1,)), ((), ())), preferred_element_type=F32)
        y_off = jnp.dot(cm, stf[...].astype(BF16), preferred_element_type=F32)
        pe16 = pe.astype(BF16)
        y_off = y_off * jnp.dot(pe16, spread_f, preferred_element_type=F32)
        eb_s[pl.ds(r, ch), :] = jnp.dot(pe16, spread_b, preferred_element_type=F32)
        lg_t = jnp.log2(dtp_t)
        row_f = p_t - lg_t
        row_b = e_t + lg_t
        for q in range(npair):
            ms = []
            for j in (2 * q, 2 * q + 1):
                jb = hpg + j
                pcol = jnp.broadcast_to(p_t[j:j + 1, :], (ch, ch)).T
                ecol = jnp.broadcast_to(e_t[jb:jb + 1, :], (ch, ch)).T
                t = jnp.where(lower, pcol - row_f[j:j + 1, :], row_b[jb:jb + 1, :] - ecol)
                ms.append((cb * (jnp.exp2(t) + jnp.where(diag, dtp_t[jb:jb + 1, :], 0.0))).astype(BF16))
            cols = slice(q * LANES, (q + 1) * LANES)
            rhs = block_diag_rhs(xs_s[pl.ds(r, ch), cols])
            y = jnp.dot(jnp.concatenate(ms, axis=1), rhs, preferred_element_type=F32)
            y_s[pl.ds(r, ch), cols] = y + y_off[:, cols]
        w_rows, tot = fwd_state_terms(dtp_t, p_t)
        state_update(stf, bt, xs_s, r, w_rows, tot)
        return carry

    lax.fori_loop(0, n_lc, lat_fwd, 0, unroll=8)

    dskip = dsk_ref[...]
    nrm = nrm_ref[...]

    def lat_bwd(i, carry):
        c = n_lc - 1 - i
        r = pl.multiple_of(c * ch, ch)
        dtp_t, p_t, e_t = load_terms(c)
        y_off = jnp.dot(c_s[pl.ds(r, ch), :], stb[...].astype(BF16), preferred_element_type=F32)
        y = y_s[pl.ds(r, ch), :] + y_off * eb_s[pl.ds(r, ch), :]
        w_rows, tot = bwd_state_terms(dtp_t, p_t, e_t)
        state_update(stb, bt_s[c], xs_s, r, w_rows, tot)
        y = (y + xs_s[pl.ds(r, ch), :].astype(F32) * dskip) * _silu(zl_ref[pl.ds(r, ch), :].astype(F32))
        y = y * lax.rsqrt(jnp.mean(y * y, axis=-1, keepdims=True) + RMS_EPS) * nrm
        o_ref[pl.ds(r, ch), :] = y.astype(o_ref.dtype)
        return carry

    lax.fori_loop(0, n_lc, lat_bwd, 0, unroll=4)


def _ssd(proj, dt, projc, dtc, conv_w, conv_b, alog_g, dtb_g, dskip_c, ssd_norm, *,
         batch, seq, ctx_len, attn_width):
    ssd_width = _ssd_width(attn_width)
    n_kv = attn_width // HEAD_DIM // Q_PER_KV
    kv_width = n_kv * HEAD_DIM
    gw = ssd_width // N_SSD_GROUPS
    hpg = gw // SSD_HEAD_DIM
    xs_off = 3 * attn_width + 2 * kv_width
    xs_off_c = 2 * kv_width
    assert xs_off % gw == 0 and xs_off_c % gw == 0 and D_STATE == LANES
    nb = N_SSD_GROUPS
    b_cb = (xs_off + ssd_width) // LANES
    c_cb = b_cb + nb
    b_cb_c = (xs_off_c + ssd_width) // LANES
    cw_b = ssd_width // LANES
    nr = alog_g.shape[1]
    in_specs = [
        pl.BlockSpec((seq, gw), lambda b, g: (b, xs_off // gw + g)),
        pl.BlockSpec((seq, LANES), lambda b, g: (b, b_cb + g)),
        pl.BlockSpec((seq, LANES), lambda b, g: (b, c_cb + g)),
        pl.BlockSpec((seq, gw), lambda b, g: (b, 2 * attn_width // gw + g)),
        pl.BlockSpec((seq, LANES), lambda b, g: (b, g)),
        pl.BlockSpec((ctx_len, gw), lambda b, g: (b, xs_off_c // gw + g)),
        pl.BlockSpec((ctx_len, LANES), lambda b, g: (b, b_cb_c + g)),
        pl.BlockSpec((ctx_len, LANES), lambda b, g: (b, g)),
        pl.BlockSpec((D_CONV, gw), lambda b, g: (0, g)),
        pl.BlockSpec((D_CONV, LANES), lambda b, g: (0, cw_b + g)),
        pl.BlockSpec((D_CONV, LANES), lambda b, g: (0, cw_b + nb + g)),
        pl.BlockSpec((1, gw), lambda b, g: (0, g)),
        pl.BlockSpec((1, LANES), lambda b, g: (0, cw_b + g)),
        pl.BlockSpec((1, LANES), lambda b, g: (0, cw_b + nb + g)),
        pl.BlockSpec((None, nr, LANES), lambda b, g: (g, 0, 0)),
        pl.BlockSpec((None, nr, LANES), lambda b, g: (g, 0, 0)),
        pl.BlockSpec((1, gw), lambda b, g: (0, g)),
        pl.BlockSpec((1, gw), lambda b, g: (0, g)),
    ]
    pad_rows = max(seq, ctx_len) + 2 * CONV_HALO
    n_chunks = max(seq, ctx_len) // SSD_CHUNK
    return pl.pallas_call(
        functools.partial(_ssd_kernel, hpg=hpg),
        grid=(batch, N_SSD_GROUPS),
        in_specs=in_specs,
        out_specs=pl.BlockSpec((seq, gw), lambda b, g: (b, g)),
        out_shape=jax.ShapeDtypeStruct((batch * seq, ssd_width), BF16),
        scratch_shapes=[pltpu.VMEM((pad_rows, gw + 2 * D_STATE), BF16),
                        pltpu.VMEM((seq, gw), BF16),
                        pltpu.VMEM((seq, LANES), BF16),
                        pltpu.VMEM((seq, LANES), BF16),
                        pltpu.VMEM((ctx_len, gw), BF16),
                        pltpu.VMEM((ctx_len, LANES), BF16),
                        pltpu.VMEM((seq, gw), F32),
                        pltpu.VMEM((D_STATE, gw), F32),
                        pltpu.VMEM((D_STATE, gw), F32),
                        pltpu.VMEM((n_chunks, 3 * nr, SSD_CHUNK), F32),
                        pltpu.VMEM((n_chunks, D_STATE, SSD_CHUNK), F32),
                        pltpu.VMEM((seq, gw), F32)],
        compiler_params=_cparams("parallel", "parallel"),
        name="ssd_bidir",
    )(proj, proj, proj, proj, dt, projc, projc, dtc,
      conv_w, conv_w, conv_w, conv_b, conv_b, conv_b, alog_g, dtb_g, dskip_c, ssd_norm)


def _outproj_kernel(a_ref, y_ref, w0_ref, w1_ref, x_ref, gate_ref, o_ref):
    acc = jnp.dot(a_ref[...], w0_ref[...], preferred_element_type=F32)
    acc = acc + jnp.dot(y_ref[...], w1_ref[...], preferred_element_type=F32)
    o_ref[...] = x_ref[...] + gate_ref[...] * acc


def _outproj(lhs0, cb0, lhs1, cb1, w, x2d, mod, *, rows_per_mod):
    m, d = x2d.shape
    kh = w.shape[0] // 2
    tm = _tile(rows_per_mod, 1024)
    tn = _tile(d, 1024)
    tiles_per_mod = rows_per_mod // tm
    return pl.pallas_call(
        _outproj_kernel,
        grid=(m // tm, d // tn),
        in_specs=[pl.BlockSpec((tm, kh), lambda i, j: (i, cb0)),
                  pl.BlockSpec((tm, kh), lambda i, j: (i, cb1)),
                  pl.BlockSpec((kh, tn), lambda i, j: (0, j)),
                  pl.BlockSpec((kh, tn), lambda i, j: (1, j)),
                  pl.BlockSpec((tm, tn), lambda i, j: (i, j)),
                  pl.BlockSpec((None, 1, tn), lambda i, j: (3 * (i // tiles_per_mod) + 2, 0, j))],
        out_specs=pl.BlockSpec((tm, tn), lambda i, j: (i, j)),
        out_shape=jax.ShapeDtypeStruct((m, d), F32),
        compiler_params=_cparams("parallel", "arbitrary"),
        name="outproj_residual",
    )(lhs0, lhs1, w, w, x2d, mod)


FFT_RADIX = 8


def _cmul_root(z, k, n):
    re, im = z
    eighth = (8 * k) // n % 8
    assert (8 * k) % n == 0
    r = math.sqrt(0.5)
    if eighth == 0:
        return re, im
    if eighth == 1:
        return (re + im) * r, (im - re) * r
    if eighth == 2:
        return im, -re
    if eighth == 3:
        return (im - re) * r, -(re + im) * r
    nre, nim = _cmul_root(z, k - n // 2, n)
    return -nre, -nim


def _fft_small(v):
    n = len(v)
    if n == 1:
        return v
    ev, od = _fft_small(v[0::2]), _fft_small(v[1::2])
    out = [None] * n
    for k in range(n // 2):
        tr, ti = _cmul_root(od[k], k, n)
        out[k] = (ev[k][0] + tr, ev[k][1] + ti)
        out[k + n // 2] = (ev[k][0] - tr, ev[k][1] - ti)
    return out


def _fourier_kernel(u_ref, z_ref, w_ref, wc_ref, o_ref, uf_scr, y_scr, ab_scr, *, rows_per_step):
    seq, tn = u_ref.shape
    n2 = seq // FFT_RADIX
    rs = rows_per_step

    nslab = tn // LANES

    def widen(i, carry):
        r = pl.multiple_of(i * LANES, LANES)
        for c in range(nslab):
            uf_scr[c, pl.ds(r, LANES), :] = u_ref[pl.ds(r, LANES), c * LANES:(c + 1) * LANES].astype(F32)
        return carry

    lax.fori_loop(0, seq // LANES, widen, 0)
    for a in range(FFT_RADIX):
        ua = jnp.concatenate([uf_scr[c, pl.ds(a, n2, stride=FFT_RADIX), :] for c in range(nslab)], axis=1)
        y_scr[a] = jnp.dot(w_ref[a], ua.astype(BF16), preferred_element_type=F32)

    per_pass = ab_scr.shape[0] // (FFT_RADIX * rs)

    def combine_and_mix(g, carry):
        base = pl.multiple_of(g * (per_pass * rs), per_pass * rs)
        for ii in range(per_pass):
            r = base + ii * rs
            z = [(y_scr[a, pl.ds(r, rs), :], y_scr[a, pl.ds(n2 + r, rs), :]) for a in range(FFT_RADIX)]
            x = _fft_small(z)
            for k1 in range(FFT_RADIX):
                rows = pl.ds((ii * FFT_RADIX + k1) * rs, rs)
                ab_scr[rows, 0:tn] = x[k1][0].astype(BF16)
                ab_scr[rows, tn:2 * tn] = x[k1][1].astype(BF16)
        f = jnp.dot(ab_scr[...], wc_ref[...], preferred_element_type=F32)
        for ii in range(per_pass):
            r = base + ii * rs
            for k1 in range(FFT_RADIX):
                s0 = (ii * FFT_RADIX + k1) * rs
                gate = _silu(z_ref[pl.ds(k1 * n2 + r, rs), :].astype(F32))
                o_ref[pl.ds(k1 * n2 + r, rs), :] = (f[s0:s0 + rs] * gate).astype(o_ref.dtype)
        return carry

    lax.fori_loop(0, n2 // (per_pass * rs), combine_and_mix, 0)


def _fourier_gate(uz, *, batch, seq, width):
    n2 = seq // FFT_RADIX
    gw = width // N_FOURIER_GROUPS
    k2 = np.arange(n2, dtype=np.int64)[None, :, None]
    pos = FFT_RADIX * np.arange(n2, dtype=np.int64)[None, None, :] + np.arange(FFT_RADIX, dtype=np.int64)[:, None, None]
    ang = 2.0 * np.pi * ((k2 * pos) % seq).astype(np.float64) / seq
    w = jnp.asarray((np.concatenate([np.cos(ang), -np.sin(ang)], axis=1) / np.sqrt(seq)).astype(np.float32))
    kc = np.arange(gw, dtype=np.int64)
    angc = 2.0 * np.pi * ((kc[:, None] * kc[None, :]) % gw).astype(np.float64) / gw
    wc = jnp.asarray((np.concatenate([np.cos(angc), np.sin(angc)], axis=0) / np.sqrt(gw)).astype(np.float32))
    rs = 2 * SUBLANES
    per_pass = min(4, n2 // rs)
    return pl.pallas_call(
        functools.partial(_fourier_kernel, rows_per_step=rs),
        grid=(batch, N_FOURIER_GROUPS),
        in_specs=[pl.BlockSpec((seq, gw), lambda b, g: (b, g)),
                  pl.BlockSpec((seq, gw), lambda b, g: (b, N_FOURIER_GROUPS + g)),
                  pl.BlockSpec((FFT_RADIX, 2 * n2, n2), lambda b, g: (0, 0, 0)),
                  pl.BlockSpec((2 * gw, gw), lambda b, g: (0, 0))],
        out_specs=pl.BlockSpec((seq, gw), lambda b, g: (b, g)),
        out_shape=jax.ShapeDtypeStruct((batch * seq, width), BF16),
        scratch_shapes=[pltpu.VMEM((gw // LANES, seq, LANES), F32),
                        pltpu.VMEM((FFT_RADIX, 2 * n2, gw), F32),
                        pltpu.VMEM((per_pass * FFT_RADIX * rs, 2 * gw), BF16)],
        compiler_params=_cparams("parallel", "parallel"),
        name="fourier_mix_gate",
    )(uz, uz, w.astype(BF16), wc.astype(BF16))


def _rope_tables(seq):
    pos = jnp.arange(seq)
    row = (pos // GRID_W).astype(F32)
    col = (pos % GRID_W).astype(F32)
    inv_freq = ROPE_THETA ** (-jnp.arange(0, ROPE_AXIS_DIM, 2, dtype=F32) / ROPE_AXIS_DIM)
    ar, ac = row[:, None] * inv_freq, col[:, None] * inv_freq
    cos_t = jnp.concatenate([jnp.cos(ar), jnp.cos(ar), jnp.cos(ac), jnp.cos(ac)], axis=-1)
    sin_t = jnp.concatenate([-jnp.sin(ar), jnp.sin(ar), -jnp.sin(ac), jnp.sin(ac)], axis=-1)
    return cos_t, sin_t


def _layer(a, j):
    return a.reshape(a.shape[1:]) if a.shape[0] == 1 else a[j]


def _per_group_rows(v, hpg):
    g = v.shape[1] // hpg
    rows = -(-2 * hpg // SUBLANES) * SUBLANES
    t = v.reshape(2, g, hpg).transpose(1, 0, 2).reshape(g, 2 * hpg)
    t = jnp.pad(t, ((0, 0), (0, rows - 2 * hpg)))
    return jnp.broadcast_to(t[:, :, None], (g, rows, LANES))


def kernel(x, c, ctx, c_ctx, ev_norm_w, ev_ada_w, ev_ada_b, ev_w_in, ev_q_norm, ev_k_norm, ev_conv_w,
           ev_conv_b, ev_a_log, ev_dt_bias, ev_d_skip, ev_ssd_norm, ev_w_out, od_norm_w, od_ada_w,
           od_ada_b, od_w_in, od_w_out):
    batch, seq, d = x.shape
    ctx_len = ctx.shape[1]
    mix = ev_w_out.shape[1]
    attn_width = mix // 2
    ssd_width = mix - attn_width
    n_ssd_heads = ssd_width // SSD_HEAD_DIM
    hpg = n_ssd_heads // N_SSD_GROUPS
    n_kv = attn_width // HEAD_DIM // Q_PER_KV
    kv_width = n_kv * HEAD_DIM
    conv_ch = ssd_width + 2 * N_SSD_GROUPS * D_STATE
    ctx_col0 = 2 * attn_width + ssd_width
    n_main = ctx_col0 + 2 * kv_width + conv_ch
    m = batch * seq
    x2d = x.reshape(m, d)
    ctx2d = ctx.reshape(batch * ctx_len, d)
    n_mod_rows = 2 * SUBLANES
    assert batch + 1 <= n_mod_rows

    w_in = _layer(ev_w_in, 0)
    w_main = w_in.astype(BF16)
    qscale = LOG2E * HEAD_DIM ** -0.5
    q_gain = (_layer(ev_q_norm, 0) * qscale).reshape(1, HEAD_DIM)
    k_gain = _layer(ev_k_norm, 0).reshape(1, HEAD_DIM)
    cos_t, sin_t = _rope_tables(seq)
    w_dt = w_in[:, n_main:].reshape(d, 2, N_SSD_GROUPS, hpg).transpose(0, 2, 1, 3)
    w_dt = jnp.pad(w_dt.reshape(d, N_SSD_GROUPS, 2 * hpg), ((0, 0), (0, 0), (0, LANES - 2 * hpg)))
    w_dt = w_dt.reshape(d, N_SSD_GROUPS * LANES).astype(BF16)
    cond = jnp.zeros((n_mod_rows, d), F32).at[:batch].set(c).at[batch].set(c_ctx)
    mod0 = _ada(cond, _layer(ev_ada_w, 0), _layer(ev_ada_b, 0)).reshape(3 * n_mod_rows, 1, d)
    norm_w0 = _layer(ev_norm_w, 0)

    proj, dt = _inproj(x2d, mod0, norm_w0, w_main, w_dt,
                       rows_per_mod=seq, mod_row0=0, col0=0, ncols=n_main)
    projc, dtc = _inproj(ctx2d, mod0, norm_w0, w_main, w_dt,
                         rows_per_mod=batch * ctx_len, mod_row0=batch, col0=ctx_col0, ncols=n_main - ctx_col0)

    attn = _attention(proj, projc, q_gain, k_gain, cos_t, sin_t,
                      batch=batch, seq=seq, ctx_len=ctx_len, attn_width=attn_width)
    yssd = _ssd(proj, dt, projc, dtc, _layer(ev_conv_w, 0), _layer(ev_conv_b, 0).reshape(1, conv_ch),
                _per_group_rows(_layer(ev_a_log, 0), hpg), _per_group_rows(_layer(ev_dt_bias, 0), hpg),
                jnp.repeat(_layer(ev_d_skip, 0), SSD_HEAD_DIM).reshape(1, ssd_width),
                _layer(ev_ssd_norm, 0).reshape(1, ssd_width),
                batch=batch, seq=seq, ctx_len=ctx_len, attn_width=attn_width)
    x1 = _outproj(attn, 0, yssd, 0, _layer(ev_w_out, 0).astype(BF16), x2d, mod0, rows_per_mod=seq)

    fw = od_w_out.shape[1]
    cond1 = jnp.zeros((n_mod_rows, d), F32).at[:batch].set(c)
    mod1 = _ada(cond1, _layer(od_ada_w, 0), _layer(od_ada_b, 0)).reshape(3 * n_mod_rows, 1, d)
    uz, _ = _inproj(x1, mod1, _layer(od_norm_w, 0), _layer(od_w_in, 0).astype(BF16), None,
                    rows_per_mod=seq, mod_row0=0, col0=0, ncols=2 * fw)
    fg = _fourier_gate(uz, batch=batch, seq=seq, width=fw)
    x2 = _outproj(fg, 0, fg, 1, _layer(od_w_out, 0).astype(BF16), x1, mod1, rows_per_mod=seq)
    return x2.reshape(batch, seq, d)
```

```python
import functools
import math

import numpy as np
import jax
import jax.numpy as jnp
from jax import lax
from jax.experimental import pallas as pl
from jax.experimental.pallas import tpu as pltpu

F32 = jnp.float32
BF16 = jnp.bfloat16

GRID_W = 64
HEAD_DIM = 128
Q_PER_KV = 4
ROPE_THETA = 10000.0
ROPE_AXIS_DIM = HEAD_DIM // 2
SSD_HEAD_DIM = 64
N_SSD_GROUPS = 4
D_STATE = 128
D_CONV = 5
SSD_CHUNK = 128
N_FOURIER_GROUPS = 8
RMS_EPS = 1e-6

LANES = 128
SUBLANES = 8
VMEM_LIMIT_BYTES = 56 * 1024 * 1024

CONV_HALO = SSD_CHUNK // 2

ROW_TILE = 1024
INPROJ_COL_TILE = 2048
OUTPROJ_COL_TILE = 1024
ADA_COL_TILE = 1024
ATTN_Q_TILE = 512
ATTN_KEY_TILE = 4 * LANES
LOG2E = 1.4426950408889634


def _cparams(*sem):
    return pltpu.CompilerParams(dimension_semantics=sem, vmem_limit_bytes=VMEM_LIMIT_BYTES)


def _tile(n, pref):
    t = min(n, pref)
    while n % t:
        t //= 2
    return t


def _silu(v):
    h = 0.5 * v
    return h + h * jnp.tanh(h)


def _ada_kernel(c_ref, w_ref, b_ref, o_ref):
    c = c_ref[...]
    s = _silu(c).astype(BF16)
    o_ref[...] = jnp.dot(s, w_ref[...].astype(BF16), preferred_element_type=F32) + b_ref[...]


def _ada(cond, w, b):
    r, d = cond.shape
    n = w.shape[1]
    tn = _tile(n, ADA_COL_TILE)
    return pl.pallas_call(
        _ada_kernel,
        grid=(n // tn,),
        in_specs=[pl.BlockSpec((r, d), lambda j: (0, 0)),
                  pl.BlockSpec((d, tn), lambda j: (0, j)),
                  pl.BlockSpec((1, tn), lambda j: (0, j))],
        out_specs=pl.BlockSpec((r, tn), lambda j: (0, j)),
        out_shape=jax.ShapeDtypeStruct((r, n), F32),
        compiler_params=_cparams("parallel"),
        name="ada_mod",
    )(cond, w, b.reshape(1, n))


def _inproj_kernel(x_ref, sh_ref, sc_ref, nw_ref, w_ref, *rest, has_dt, sub):
    if has_dt:
        wdt_ref, o_ref, dt_ref, h_ref = rest
    else:
        o_ref, h_ref = rest
    tm = x_ref.shape[0]

    @pl.when(pl.program_id(1) == 0)
    def _():
        nw = nw_ref[...]
        sc = 1.0 + sc_ref[...]
        sh = sh_ref[...]

        def body(s, carry):
            r = pl.multiple_of(s * sub, sub)
            x = x_ref[pl.ds(r, sub), :]
            y = x * lax.rsqrt(jnp.mean(x * x, axis=-1, keepdims=True) + RMS_EPS) * nw
            h_ref[pl.ds(r, sub), :] = (y * sc + sh).astype(BF16)
            return carry

        lax.fori_loop(0, tm // sub, body, 0, unroll=4)
        if has_dt:
            dt_ref[...] = jnp.dot(h_ref[...], wdt_ref[...], preferred_element_type=F32)

    o_ref[...] = jnp.dot(h_ref[...], w_ref[...], preferred_element_type=F32).astype(o_ref.dtype)


def _inproj(x2d, mod, norm_w, w, wdt, *, rows_per_mod, mod_row0, col0, ncols):
    m, d = x2d.shape
    tm = _tile(rows_per_mod, ROW_TILE)
    has_dt = wdt is not None
    tn = _tile(math.gcd(ncols, col0), INPROJ_COL_TILE)
    cb0 = col0 // tn
    tiles_per_mod = rows_per_mod // tm

    def modrow(i):
        return mod_row0 + i // tiles_per_mod

    in_specs = [pl.BlockSpec((tm, d), lambda i, j: (i, 0)),
                pl.BlockSpec((None, 1, d), lambda i, j: (3 * modrow(i), 0, 0)),
                pl.BlockSpec((None, 1, d), lambda i, j: (3 * modrow(i) + 1, 0, 0)),
                pl.BlockSpec((1, d), lambda i, j: (0, 0)),
                pl.BlockSpec((d, tn), lambda i, j: (0, cb0 + j))]
    args = [x2d, mod, mod, norm_w.reshape(1, d), w]
    out_specs = [pl.BlockSpec((tm, tn), lambda i, j: (i, j))]
    out_shape = [jax.ShapeDtypeStruct((m, ncols), BF16)]
    if has_dt:
        ndt = wdt.shape[1]
        in_specs.append(pl.BlockSpec((d, ndt), lambda i, j: (0, 0)))
        args.append(wdt)
        out_specs.append(pl.BlockSpec((tm, ndt), lambda i, j: (i, 0)))
        out_shape.append(jax.ShapeDtypeStruct((m, ndt), F32))
    res = pl.pallas_call(
        functools.partial(_inproj_kernel, has_dt=has_dt, sub=min(32, tm)),
        grid=(m // tm, ncols // tn),
        in_specs=in_specs,
        out_specs=out_specs,
        out_shape=out_shape,
        scratch_shapes=[pltpu.VMEM((tm, d), BF16)],
        compiler_params=_cparams("parallel", "arbitrary"),
        name="mod_inproj",
    )(*args)
    return (res[0], res[1]) if has_dt else (res[0], None)


def _norm_rope(v, gain, cos, sin):
    ones = jnp.ones((HEAD_DIM, HEAD_DIM), BF16)
    ss = jnp.dot((v * v).astype(BF16), ones, preferred_element_type=F32)
    y = v * lax.rsqrt(ss * (1.0 / HEAD_DIM) + RMS_EPS) * gain
    if cos is not None:
        quarter = ROPE_AXIS_DIM // 2
        lane = lax.broadcasted_iota(jnp.int32, (1, HEAD_DIM), 1)
        partner = jnp.where(lane % ROPE_AXIS_DIM < quarter,
                            pltpu.roll(y, HEAD_DIM - quarter, 1), pltpu.roll(y, quarter, 1))
        y = y * cos + partner * sin
    return y.astype(BF16)


def _attn_kernel(q_ref, g_ref, kl_ref, vl_ref, kc_ref, vc_ref, qg_ref, kg_ref, cq_ref, sq_ref, ck_ref, sk_ref,
                 o_ref, k_scr, s_scr, *, nrep, tk):
    tq = q_ref.shape[0]
    rows = nrep * tq
    blocks, col = [], 0
    for v_ref in (vc_ref, vl_ref):
        n = v_ref.shape[0]
        for r0 in range(0, n, tk):
            blocks.append((v_ref, r0, min(tk, n - r0), col))
            col += min(tk, n - r0)

    @pl.when(pl.program_id(2) == 0)
    def _prep_keys():
        kg = kg_ref[...]
        for v_ref, r0, size, c0 in blocks:
            if v_ref is vc_ref:
                k_scr[pl.ds(c0, size), :] = _norm_rope(kc_ref[pl.ds(r0, size), :].astype(F32), kg, None, None)
            else:
                k_scr[pl.ds(c0, size), :] = _norm_rope(kl_ref[pl.ds(r0, size), :].astype(F32), kg,
                                                       ck_ref[pl.ds(r0, size), :], sk_ref[pl.ds(r0, size), :])

    qg = qg_ref[...]
    qs = jnp.concatenate([_norm_rope(q_ref[:, r * HEAD_DIM:(r + 1) * HEAD_DIM].astype(F32), qg,
                                     cq_ref[...], sq_ref[...]) for r in range(nrep)], axis=0)

    half = rows // 2
    halves = (slice(0, half), slice(half, rows))
    m = [None, None]
    for _, r0, size, c0 in blocks:
        kb = k_scr[pl.ds(c0, size), :]
        for i, hs in enumerate(halves):
            s = lax.dot_general(qs[hs], kb, (((1,), (1,)), ((), ())), preferred_element_type=F32)
            s_scr[hs, c0:c0 + size] = s
            for c in range(size // LANES):
                sc = s[:, c * LANES:(c + 1) * LANES]
                m[i] = sc if m[i] is None else jnp.maximum(m[i], sc)
    mb = [jnp.broadcast_to(jnp.max(mi, axis=-1, keepdims=True), (half, LANES)) for mi in m]

    acc = [jnp.zeros((half, HEAD_DIM + LANES), F32) for _ in halves]
    for v_ref, r0, size, c0 in blocks:
        v_aug = jnp.concatenate([v_ref[pl.ds(r0, size), :], jnp.ones((size, LANES), BF16)], axis=1)
        for i, hs in enumerate(halves):
            p = jnp.concatenate([jnp.exp2(s_scr[hs, c0 + c * LANES:c0 + (c + 1) * LANES] - mb[i])
                                 for c in range(size // LANES)], axis=1).astype(BF16)
            acc[i] = acc[i] + jnp.dot(p, v_aug, preferred_element_type=F32)
    acc = jnp.concatenate(acc, axis=0)
    o = acc[:, :HEAD_DIM] / acc[:, HEAD_DIM:]
    for r in range(nrep):
        g = g_ref[:, r * HEAD_DIM:(r + 1) * HEAD_DIM].astype(F32)
        o_ref[:, r * HEAD_DIM:(r + 1) * HEAD_DIM] = (o[r * tq:(r + 1) * tq] * _silu(g)).astype(o_ref.dtype)


def _attention(proj, projc, q_gain, k_gain, cos_t, sin_t, *, batch, seq, ctx_len, attn_width):
    n_kv = attn_width // HEAD_DIM // Q_PER_KV
    gw = Q_PER_KV * HEAD_DIM
    tq = _tile(seq, ATTN_Q_TILE)
    tk = ATTN_KEY_TILE
    assert ctx_len % LANES == 0 and seq % LANES == 0
    tpb = seq // tq
    k_cb = (2 * attn_width + _ssd_width(attn_width)) // HEAD_DIM
    v_cb = k_cb + n_kv
    return pl.pallas_call(
        functools.partial(_attn_kernel, nrep=Q_PER_KV, tk=tk),
        grid=(batch, n_kv, tpb),
        in_specs=[pl.BlockSpec((tq, gw), lambda b, h, t: (b * tpb + t, h)),
                  pl.BlockSpec((tq, gw), lambda b, h, t: (b * tpb + t, attn_width // gw + h)),
                  pl.BlockSpec((seq, HEAD_DIM), lambda b, h, t: (b, k_cb + h)),
                  pl.BlockSpec((seq, HEAD_DIM), lambda b, h, t: (b, v_cb + h)),
                  pl.BlockSpec((ctx_len, HEAD_DIM), lambda b, h, t: (b, h)),
                  pl.BlockSpec((ctx_len, HEAD_DIM), lambda b, h, t: (b, n_kv + h)),
                  pl.BlockSpec((1, HEAD_DIM), lambda b, h, t: (0, 0)),
                  pl.BlockSpec((1, HEAD_DIM), lambda b, h, t: (0, 0)),
                  pl.BlockSpec((tq, HEAD_DIM), lambda b, h, t: (t, 0)),
                  pl.BlockSpec((tq, HEAD_DIM), lambda b, h, t: (t, 0)),
                  pl.BlockSpec((seq, HEAD_DIM), lambda b, h, t: (0, 0)),
                  pl.BlockSpec((seq, HEAD_DIM), lambda b, h, t: (0, 0))],
        out_specs=pl.BlockSpec((tq, gw), lambda b, h, t: (b * tpb + t, h)),
        out_shape=jax.ShapeDtypeStruct((batch * seq, attn_width), BF16),
        scratch_shapes=[pltpu.VMEM((ctx_len + seq, HEAD_DIM), BF16),
                        pltpu.VMEM((Q_PER_KV * tq, ctx_len + seq), F32)],
        compiler_params=_cparams("parallel", "parallel", "arbitrary"),
        name="gqa_attention",
    )(proj, proj, proj, proj, projc, projc, q_gain, k_gain, cos_t, sin_t, cos_t, sin_t)


def _ssd_width(attn_width):
    return attn_width


def _conv_silu(groups, pad_ref, n_rows):
    ch = SSD_CHUNK
    halo = CONV_HALO
    win_rows = ch + 2 * halo
    widths = [g[0].shape[1] for g in groups]
    offs = [sum(widths[:i]) for i in range(len(groups))]
    total = sum(widths)
    pad_ref[pl.ds(0, halo), 0:total] = jnp.zeros((halo, total), BF16)
    pad_ref[pl.ds(halo + n_rows, halo), 0:total] = jnp.zeros((halo, total), BF16)

    def copy(i, carry):
        r = pl.multiple_of(i * ch, ch)
        for (src_ref, _, _, _), o, cw in zip(groups, offs, widths):
            pad_ref[pl.ds(r + halo, ch), o:o + cw] = src_ref[pl.ds(r, ch), :]
        return carry

    lax.fori_loop(0, n_rows // ch, copy, 0)
    w = jnp.concatenate([g[1][...] for g in groups], axis=1)
    bias = jnp.concatenate([g[2][...] for g in groups], axis=1)
    out_row = lax.broadcasted_iota(jnp.int32, (ch, win_rows), 0)
    win_row = lax.broadcasted_iota(jnp.int32, (ch, win_rows), 1)
    taps = [k for k in range(D_CONV) if k != D_CONV // 2]
    selects = [jnp.where(win_row == out_row + (halo + k - D_CONV // 2), 1.0, 0.0).astype(BF16) for k in taps]

    blk = 2 * LANES

    def body(i, carry):
        r = pl.multiple_of(i * ch, ch)
        ys = []
        for c0 in range(0, total, blk):
            cs = slice(c0, min(c0 + blk, total))
            win = pad_ref[pl.ds(r, win_rows), cs]
            acc = bias[:, cs] + win[halo:halo + ch].astype(F32) * w[D_CONV // 2:D_CONV // 2 + 1, cs]
            for k, select in zip(taps, selects):
                acc = acc + jnp.dot(select, win, preferred_element_type=F32) * w[k:k + 1, cs]
            ys.append(_silu(acc))
        y = jnp.concatenate(ys, axis=1)
        for (_, _, _, dst_ref), o, cw in zip(groups, offs, widths):
            dst_ref[pl.ds(r, ch), :] = y[:, o:o + cw].astype(dst_ref.dtype)
        return carry

    lax.fori_loop(0, n_rows // ch, body, 0, unroll=2)


def _ssd_kernel(xl_ref, bl_ref, cl_ref, zl_ref, dtl_ref, xc_ref, bc_ref, dtc_ref,
                wx_ref, wb_ref, wc_ref, bx_ref, bb_ref, bcv_ref, alog_ref, dtb_ref, dsk_ref, nrm_ref,
                o_ref,
                pad_s, xs_s, b_s, c_s, xsc_s, bcc_s, y_s, stf, stb, terms_s, bt_s, eb_s, *, hpg):
    ch = SSD_CHUNK
    seq = xl_ref.shape[0]
    ctx_len = xc_ref.shape[0]
    gw = xl_ref.shape[1]
    npair = gw // LANES
    assert 2 * SSD_HEAD_DIM == LANES and hpg == 2 * npair

    _conv_silu([(xl_ref, wx_ref, bx_ref, xs_s), (bl_ref, wb_ref, bb_ref, b_s), (cl_ref, wc_ref, bcv_ref, c_s)],
               pad_s, seq)
    _conv_silu([(xc_ref, wx_ref, bx_ref, xsc_s), (bc_ref, wb_ref, bb_ref, bcc_s)], pad_s, ctx_len)

    li = lax.broadcasted_iota(jnp.int32, (ch, ch), 0)
    si = lax.broadcasted_iota(jnp.int32, (ch, ch), 1)
    tri_t = (li <= si).astype(F32)
    nr = alog_ref.shape[0]
    lower = li >= si
    upper = si >= li
    diag = li == si
    xr = lax.broadcasted_iota(jnp.int32, (ch, gw), 0)
    xh = lax.broadcasted_iota(jnp.int32, (ch, gw), 1) // SSD_HEAD_DIM
    spread_f = jnp.where(xr == 2 * nr + xh, 1.0, 0.0).astype(BF16)
    spread_b = jnp.where(xr == 2 * nr + hpg + xh, 1.0, 0.0).astype(BF16)
    row_id = lax.broadcasted_iota(jnp.int32, (nr, ch), 0)
    lane = lax.broadcasted_iota(jnp.int32, (1, LANES), 1)
    head0 = lane < SSD_HEAD_DIM
    neg_a_t = -jnp.exp(alog_ref[...]) * LOG2E
    dt_bias_t = dtb_ref[...]

    def chunk_terms(dt_ref, r):
        x_t = dt_ref[pl.ds(r, ch), :].T[:nr] + dt_bias_t
        dtp_t = jnp.maximum(x_t, 0.0) + jnp.log1p(jnp.exp(-jnp.abs(x_t)))
        da_t = dtp_t * neg_a_t
        p_t = jnp.dot(da_t, tri_t, preferred_element_type=F32, precision=lax.Precision.HIGHEST)
        return dtp_t, p_t, p_t - da_t

    def columns(p_t, e_t):
        tot = p_t[:, ch - 1:ch]
        fac_t = jnp.exp2(jnp.where(row_id < hpg, p_t, tot - e_t))
        pad = jnp.zeros((ch - 3 * nr, ch), F32)
        return jnp.concatenate([p_t, e_t, fac_t, pad], axis=0).T

    def pair_sel(a, b):
        return jnp.where(head0, a, b)

    def block_diag_rhs(x_pair):
        zero = jnp.zeros_like(x_pair)
        return jnp.concatenate([jnp.where(head0, x_pair, zero), jnp.where(head0, zero, x_pair)], axis=0)

    def state_update(st_ref, bt, x_ref, r, w_rows, tot):
        for q in range(npair):
            j0, j1 = 2 * q, 2 * q + 1
            lhs = jnp.concatenate([(bt * w_rows[j0]).astype(BF16), (bt * w_rows[j1]).astype(BF16)], axis=1)
            rhs = block_diag_rhs(x_ref[pl.ds(r, ch), q * LANES:(q + 1) * LANES])
            new = jnp.dot(lhs, rhs, preferred_element_type=F32)
            dec = pair_sel(jnp.exp2(tot[j0]), jnp.exp2(tot[j1]))
            cols = slice(q * LANES, (q + 1) * LANES)
            st_ref[:, cols] = st_ref[:, cols] * dec + new

    def fwd_state_terms(dtp_t, p_t):
        w_rows, tot = [], []
        for j in range(hpg):
            t = p_t[j:j + 1, ch - 1:ch]
            tot.append(t)
            w_rows.append(jnp.exp2(t - p_t[j:j + 1, :]) * dtp_t[j:j + 1, :])
        return w_rows, tot

    def bwd_state_terms(dtp_t, p_t, e_t):
        w_rows, tot = [], []
        for j in range(hpg, 2 * hpg):
            tot.append(p_t[j:j + 1, ch - 1:ch])
            w_rows.append(jnp.exp2(e_t[j:j + 1, :]) * dtp_t[j:j + 1, :])
        return w_rows, tot

    stf[...] = jnp.zeros(stf.shape, F32)
    stb[...] = jnp.zeros(stb.shape, F32)
    n_cc = ctx_len // ch

    def load_terms(c):
        terms = terms_s[c]
        return terms[:nr], terms[nr:2 * nr], terms[2 * nr:]

    def ctx_fwd(c, carry):
        r = pl.multiple_of(c * ch, ch)
        dtp_t, p_t, e_t = chunk_terms(dtc_ref, r)
        terms_s[c] = jnp.concatenate([dtp_t, p_t, e_t], axis=0)
        bt = bcc_s[pl.ds(r, ch), :].astype(F32).T
        bt_s[c] = bt
        w_rows, tot = fwd_state_terms(dtp_t, p_t)
        state_update(stf, bt, xsc_s, r, w_rows, tot)
        return carry

    def ctx_bwd(i, carry):
        c = n_cc - 1 - i
        r = pl.multiple_of(c * ch, ch)
        dtp_t, p_t, e_t = load_terms(c)
        w_rows, tot = bwd_state_terms(dtp_t, p_t, e_t)
        state_update(stb, bt_s[c], xsc_s, r, w_rows, tot)
        return carry

    lax.fori_loop(0, n_cc, ctx_fwd, 0, unroll=2)
    lax.fori_loop(0, n_cc, ctx_bwd, 0, unroll=2)

    n_lc = seq // ch

    def lat_fwd(c, carry):
        r = pl.multiple_of(c * ch, ch)
        dtp_t, p_t, e_t = chunk_terms(dtl_ref, r)
        terms_s[c] = jnp.concatenate([dtp_t, p_t, e_t], axis=0)
        pe = columns(p_t, e_t)
        bm = b_s[pl.ds(r, ch), :]
        cm = c_s[pl.ds(r, ch), :]
        bt = bm.astype(F32).T
        bt_s[c] = bt
        cb = lax.dot_general(cm, bm, (((1,), (1,)), ((), ())), preferred_element_type=F32)
        y_off = jnp.dot(cm, stf[...].astype(BF16), preferred_element_type=F32)
        pe16 = pe.astype(BF16)
        y_off = y_off * jnp.dot(pe16, spread_f, preferred_element_type=F32)
        eb_s[pl.ds(r, ch), :] = jnp.dot(pe16, spread_b, preferred_element_type=F32)
        lg_t = jnp.log2(dtp_t)
        row_f = p_t - lg_t
        row_b = e_t + lg_t
        for q in range(npair):
            ms = []
            for j in (2 * q, 2 * q + 1):
                jb = hpg + j
                col = jnp.where(upper, jnp.broadcast_to(p_t[j:j + 1, :], (ch, ch)),
                                jnp.broadcast_to(e_t[jb:jb + 1, :], (ch, ch))).T
                t = jnp.where(lower, col - row_f[j:j + 1, :], row_b[jb:jb + 1, :] - col)
                ms.append((cb * (jnp.exp2(t) + jnp.where(diag, dtp_t[jb:jb + 1, :], 0.0))).astype(BF16))
            cols = slice(q * LANES, (q + 1) * LANES)
            rhs = block_diag_rhs(xs_s[pl.ds(r, ch), cols])
            y = jnp.dot(jnp.concatenate(ms, axis=1), rhs, preferred_element_type=F32)
            y_s[pl.ds(r, ch), cols] = y + y_off[:, cols]
        w_rows, tot = fwd_state_terms(dtp_t, p_t)
        state_update(stf, bt, xs_s, r, w_rows, tot)
        return carry

    lax.fori_loop(0, n_lc, lat_fwd, 0, unroll=8)

    dskip = dsk_ref[...]
    nrm = nrm_ref[...]

    def lat_bwd(i, carry):
        c = n_lc - 1 - i
        r = pl.multiple_of(c * ch, ch)
        dtp_t, p_t, e_t = load_terms(c)
        y_off = jnp.dot(c_s[pl.ds(r, ch), :], stb[...].astype(BF16), preferred_element_type=F32)
        y = y_s[pl.ds(r, ch), :] + y_off * eb_s[pl.ds(r, ch), :]
        w_rows, tot = bwd_state_terms(dtp_t, p_t, e_t)
        state_update(stb, bt_s[c], xs_s, r, w_rows, tot)
        y = (y + xs_s[pl.ds(r, ch), :].astype(F32) * dskip) * _silu(zl_ref[pl.ds(r, ch), :].astype(F32))
        y = y * lax.rsqrt(jnp.mean(y * y, axis=-1, keepdims=True) + RMS_EPS) * nrm
        o_ref[pl.ds(r, ch), :] = y.astype(o_ref.dtype)
        return carry

    lax.fori_loop(0, n_lc, lat_bwd, 0, unroll=4)


def _ssd(proj, dt, projc, dtc, conv_w, conv_b, alog_g, dtb_g, dskip_c, ssd_norm, *,
         batch, seq, ctx_len, attn_width):
    ssd_width = _ssd_width(attn_width)
    n_kv = attn_width // HEAD_DIM // Q_PER_KV
    kv_width = n_kv * HEAD_DIM
    gw = ssd_width // N_SSD_GROUPS
    hpg = gw // SSD_HEAD_DIM
    xs_off = 3 * attn_width + 2 * kv_width
    xs_off_c = 2 * kv_width
    assert xs_off % gw == 0 and xs_off_c % gw == 0 and D_STATE == LANES
    nb = N_SSD_GROUPS
    b_cb = (xs_off + ssd_width) // LANES
    c_cb = b_cb + nb
    b_cb_c = (xs_off_c + ssd_width) // LANES
    cw_b = ssd_width // LANES
    nr = alog_g.shape[1]
    in_specs = [
        pl.BlockSpec((seq, gw), lambda b, g: (b, xs_off // gw + g)),
        pl.BlockSpec((seq, LANES), lambda b, g: (b, b_cb + g)),
        pl.BlockSpec((seq, LANES), lambda b, g: (b, c_cb + g)),
        pl.BlockSpec((seq, gw), lambda b, g: (b, 2 * attn_width // gw + g)),
        pl.BlockSpec((seq, LANES), lambda b, g: (b, g)),
        pl.BlockSpec((ctx_len, gw), lambda b, g: (b, xs_off_c // gw + g)),
        pl.BlockSpec((ctx_len, LANES), lambda b, g: (b, b_cb_c + g)),
        pl.BlockSpec((ctx_len, LANES), lambda b, g: (b, g)),
        pl.BlockSpec((D_CONV, gw), lambda b, g: (0, g)),
        pl.BlockSpec((D_CONV, LANES), lambda b, g: (0, cw_b + g)),
        pl.BlockSpec((D_CONV, LANES), lambda b, g: (0, cw_b + nb + g)),
        pl.BlockSpec((1, gw), lambda b, g: (0, g)),
        pl.BlockSpec((1, LANES), lambda b, g: (0, cw_b + g)),
        pl.BlockSpec((1, LANES), lambda b, g: (0, cw_b + nb + g)),
        pl.BlockSpec((None, nr, LANES), lambda b, g: (g, 0, 0)),
        pl.BlockSpec((None, nr, LANES), lambda b, g: (g, 0, 0)),
        pl.BlockSpec((1, gw), lambda b, g: (0, g)),
        pl.BlockSpec((1, gw), lambda b, g: (0, g)),
    ]
    pad_rows = max(seq, ctx_len) + 2 * CONV_HALO
    n_chunks = max(seq, ctx_len) // SSD_CHUNK
    return pl.pallas_call(
        functools.partial(_ssd_kernel, hpg=hpg),
        grid=(batch, N_SSD_GROUPS),
        in_specs=in_specs,
        out_specs=pl.BlockSpec((seq, gw), lambda b, g: (b, g)),
        out_shape=jax.ShapeDtypeStruct((batch * seq, ssd_width), BF16),
        scratch_shapes=[pltpu.VMEM((pad_rows, gw + 2 * D_STATE), BF16),
                        pltpu.VMEM((seq, gw), BF16),
                        pltpu.VMEM((seq, LANES), BF16),
                        pltpu.VMEM((seq, LANES), BF16),
                        pltpu.VMEM((ctx_len, gw), BF16),
                        pltpu.VMEM((ctx_len, LANES), BF16),
                        pltpu.VMEM((seq, gw), F32),
                        pltpu.VMEM((D_STATE, gw), F32),
                        pltpu.VMEM((D_STATE, gw), F32),
                        pltpu.VMEM((n_chunks, 3 * nr, SSD_CHUNK), F32),
                        pltpu.VMEM((n_chunks, D_STATE, SSD_CHUNK), F32),
                        pltpu.VMEM((seq, gw), F32)],
        compiler_params=_cparams("parallel", "parallel"),
        name="ssd_bidir",
    )(proj, proj, proj, proj, dt, projc, projc, dtc,
      conv_w, conv_w, conv_w, conv_b, conv_b, conv_b, alog_g, dtb_g, dskip_c, ssd_norm)


def _outproj_kernel(a_ref, y_ref, w0_ref, w1_ref, x_ref, gate_ref, o_ref):
    acc = jnp.dot(a_ref[...], w0_ref[...], preferred_element_type=F32)
    acc = acc + jnp.dot(y_ref[...], w1_ref[...], preferred_element_type=F32)
    o_ref[...] = x_ref[...] + gate_ref[...] * acc


def _outproj(lhs0, cb0, lhs1, cb1, w, x2d, mod, *, rows_per_mod):
    m, d = x2d.shape
    kh = w.shape[0] // 2
    tm = _tile(rows_per_mod, ROW_TILE)
    tn = _tile(d, OUTPROJ_COL_TILE)
    tiles_per_mod = rows_per_mod // tm
    return pl.pallas_call(
        _outproj_kernel,
        grid=(m // tm, d // tn),
        in_specs=[pl.BlockSpec((tm, kh), lambda i, j: (i, cb0)),
                  pl.BlockSpec((tm, kh), lambda i, j: (i, cb1)),
                  pl.BlockSpec((kh, tn), lambda i, j: (0, j)),
                  pl.BlockSpec((kh, tn), lambda i, j: (1, j)),
                  pl.BlockSpec((tm, tn), lambda i, j: (i, j)),
                  pl.BlockSpec((None, 1, tn), lambda i, j: (3 * (i // tiles_per_mod) + 2, 0, j))],
        out_specs=pl.BlockSpec((tm, tn), lambda i, j: (i, j)),
        out_shape=jax.ShapeDtypeStruct((m, d), F32),
        compiler_params=_cparams("parallel", "arbitrary"),
        name="outproj_residual",
    )(lhs0, lhs1, w, w, x2d, mod)


FFT_RADIX = 8


def _cmul_root(z, k, n):
    re, im = z
    eighth = (8 * k) // n % 8
    assert (8 * k) % n == 0
    r = math.sqrt(0.5)
    if eighth == 0:
        return re, im
    if eighth == 1:
        return (re + im) * r, (im - re) * r
    if eighth == 2:
        return im, -re
    if eighth == 3:
        return (im - re) * r, -(re + im) * r
    nre, nim = _cmul_root(z, k - n // 2, n)
    return -nre, -nim


def _fft_small(v):
    n = len(v)
    if n == 1:
        return v
    ev, od = _fft_small(v[0::2]), _fft_small(v[1::2])
    out = [None] * n
    for k in range(n // 2):
        tr, ti = _cmul_root(od[k], k, n)
        out[k] = (ev[k][0] + tr, ev[k][1] + ti)
        out[k + n // 2] = (ev[k][0] - tr, ev[k][1] - ti)
    return out


def _fourier_kernel(u_ref, z_ref, w_ref, wc_ref, o_ref, uf_scr, y_scr, ab_scr, *, rows_per_step):
    seq, tn = u_ref.shape
    n2 = seq // FFT_RADIX
    rs = rows_per_step

    nslab = tn // LANES

    def widen(i, carry):
        r = pl.multiple_of(i * LANES, LANES)
        for c in range(nslab):
            uf_scr[c, pl.ds(r, LANES), :] = u_ref[pl.ds(r, LANES), c * LANES:(c + 1) * LANES].astype(F32)
        return carry

    lax.fori_loop(0, seq // LANES, widen, 0)
    for a in range(FFT_RADIX):
        ua = jnp.concatenate([uf_scr[c, pl.ds(a, n2, stride=FFT_RADIX), :] for c in range(nslab)], axis=1)
        y_scr[a] = jnp.dot(w_ref[a], ua.astype(BF16), preferred_element_type=F32)

    per_pass = ab_scr.shape[0] // (FFT_RADIX * rs)

    def combine_and_mix(g, carry):
        base = pl.multiple_of(g * (per_pass * rs), per_pass * rs)
        for ii in range(per_pass):
            r = base + ii * rs
            z = [(y_scr[a, pl.ds(r, rs), :], y_scr[a, pl.ds(n2 + r, rs), :]) for a in range(FFT_RADIX)]
            x = _fft_small(z)
            for k1 in range(FFT_RADIX):
                rows = pl.ds((ii * FFT_RADIX + k1) * rs, rs)
                ab_scr[rows, 0:tn] = x[k1][0].astype(BF16)
                ab_scr[rows, tn:2 * tn] = x[k1][1].astype(BF16)
        f = jnp.dot(ab_scr[...], wc_ref[...], preferred_element_type=F32)
        for ii in range(per_pass):
            r = base + ii * rs
            for k1 in range(FFT_RADIX):
                s0 = (ii * FFT_RADIX + k1) * rs
                gate = _silu(z_ref[pl.ds(k1 * n2 + r, rs), :].astype(F32))
                o_ref[pl.ds(k1 * n2 + r, rs), :] = (f[s0:s0 + rs] * gate).astype(o_ref.dtype)
        return carry

    lax.fori_loop(0, n2 // (per_pass * rs), combine_and_mix, 0)


def _fourier_gate(uz, *, batch, seq, width):
    n2 = seq // FFT_RADIX
    gw = width // N_FOURIER_GROUPS
    k2 = np.arange(n2, dtype=np.int64)[None, :, None]
    pos = FFT_RADIX * np.arange(n2, dtype=np.int64)[None, None, :] + np.arange(FFT_RADIX, dtype=np.int64)[:, None, None]
    ang = 2.0 * np.pi * ((k2 * pos) % seq).astype(np.float64) / seq
    w = jnp.asarray((np.concatenate([np.cos(ang), -np.sin(ang)], axis=1) / np.sqrt(seq)).astype(np.float32))
    kc = np.arange(gw, dtype=np.int64)
    angc = 2.0 * np.pi * ((kc[:, None] * kc[None, :]) % gw).astype(np.float64) / gw
    wc = jnp.asarray((np.concatenate([np.cos(angc), np.sin(angc)], axis=0) / np.sqrt(gw)).astype(np.float32))
    rs = 2 * SUBLANES
    per_pass = min(4, n2 // rs)
    return pl.pallas_call(
        functools.partial(_fourier_kernel, rows_per_step=rs),
        grid=(batch, N_FOURIER_GROUPS),
        in_specs=[pl.BlockSpec((seq, gw), lambda b, g: (b, g)),
                  pl.BlockSpec((seq, gw), lambda b, g: (b, N_FOURIER_GROUPS + g)),
                  pl.BlockSpec((FFT_RADIX, 2 * n2, n2), lambda b, g: (0, 0, 0)),
                  pl.BlockSpec((2 * gw, gw), lambda b, g: (0, 0))],
        out_specs=pl.BlockSpec((seq, gw), lambda b, g: (b, g)),
        out_shape=jax.ShapeDtypeStruct((batch * seq, width), BF16),
        scratch_shapes=[pltpu.VMEM((gw // LANES, seq, LANES), F32),
                        pltpu.VMEM((FFT_RADIX, 2 * n2, gw), F32),
                        pltpu.VMEM((per_pass * FFT_RADIX * rs, 2 * gw), BF16)],
        compiler_params=_cparams("parallel", "parallel"),
        name="fourier_mix_gate",
    )(uz, uz, w.astype(BF16), wc.astype(BF16))


def _rope_tables(seq):
    pos = jnp.arange(seq)
    row = (pos // GRID_W).astype(F32)
    col = (pos % GRID_W).astype(F32)
    inv_freq = ROPE_THETA ** (-jnp.arange(0, ROPE_AXIS_DIM, 2, dtype=F32) / ROPE_AXIS_DIM)
    ar, ac = row[:, None] * inv_freq, col[:, None] * inv_freq
    cos_t = jnp.concatenate([jnp.cos(ar), jnp.cos(ar), jnp.cos(ac), jnp.cos(ac)], axis=-1)
    sin_t = jnp.concatenate([-jnp.sin(ar), jnp.sin(ar), -jnp.sin(ac), jnp.sin(ac)], axis=-1)
    return cos_t, sin_t


def _layer(a, j):
    return a.reshape(a.shape[1:]) if a.shape[0] == 1 else a[j]


def _per_group_rows(v, hpg):
    g = v.shape[1] // hpg
    rows = -(-2 * hpg // SUBLANES) * SUBLANES
    t = v.reshape(2, g, hpg).transpose(1, 0, 2).reshape(g, 2 * hpg)
    t = jnp.pad(t, ((0, 0), (0, rows - 2 * hpg)))
    return jnp.broadcast_to(t[:, :, None], (g, rows, LANES))


def kernel(x, c, ctx, c_ctx, ev_norm_w, ev_ada_w, ev_ada_b, ev_w_in, ev_q_norm, ev_k_norm, ev_conv_w,
           ev_conv_b, ev_a_log, ev_dt_bias, ev_d_skip, ev_ssd_norm, ev_w_out, od_norm_w, od_ada_w,
           od_ada_b, od_w_in, od_w_out):
    batch, seq, d = x.shape
    ctx_len = ctx.shape[1]
    mix = ev_w_out.shape[1]
    attn_width = mix // 2
    ssd_width = mix - attn_width
    n_ssd_heads = ssd_width // SSD_HEAD_DIM
    hpg = n_ssd_heads // N_SSD_GROUPS
    n_kv = attn_width // HEAD_DIM // Q_PER_KV
    kv_width = n_kv * HEAD_DIM
    conv_ch = ssd_width + 2 * N_SSD_GROUPS * D_STATE
    ctx_col0 = 2 * attn_width + ssd_width
    n_main = ctx_col0 + 2 * kv_width + conv_ch
    m = batch * seq
    x2d = x.reshape(m, d)
    ctx2d = ctx.reshape(batch * ctx_len, d)
    n_mod_rows = 2 * SUBLANES
    assert batch + 1 <= n_mod_rows

    w_in = _layer(ev_w_in, 0)
    w_main = w_in.astype(BF16)
    qscale = LOG2E * HEAD_DIM ** -0.5
    q_gain = (_layer(ev_q_norm, 0) * qscale).reshape(1, HEAD_DIM)
    k_gain = _layer(ev_k_norm, 0).reshape(1, HEAD_DIM)
    cos_t, sin_t = _rope_tables(seq)
    dt_cols = w_main[:, n_main:]
    lane_pad = jnp.zeros((d, LANES - 2 * hpg), BF16)
    w_dt = jnp.concatenate(
        [blk for g in range(N_SSD_GROUPS)
         for blk in (dt_cols[:, g * hpg:(g + 1) * hpg],
                     dt_cols[:, n_ssd_heads + g * hpg:n_ssd_heads + (g + 1) * hpg], lane_pad)], axis=1)
    cond = jnp.zeros((n_mod_rows, d), F32).at[:batch].set(c).at[batch].set(c_ctx)
    mod0 = _ada(cond, _layer(ev_ada_w, 0), _layer(ev_ada_b, 0)).reshape(3 * n_mod_rows, 1, d)
    norm_w0 = _layer(ev_norm_w, 0)

    proj, dt = _inproj(x2d, mod0, norm_w0, w_main, w_dt,
                       rows_per_mod=seq, mod_row0=0, col0=0, ncols=n_main)
    projc, dtc = _inproj(ctx2d, mod0, norm_w0, w_main, w_dt,
                         rows_per_mod=batch * ctx_len, mod_row0=batch, col0=ctx_col0, ncols=n_main - ctx_col0)

    attn = _attention(proj, projc, q_gain, k_gain, cos_t, sin_t,
                      batch=batch, seq=seq, ctx_len=ctx_len, attn_width=attn_width)
    yssd = _ssd(proj, dt, projc, dtc, _layer(ev_conv_w, 0), _layer(ev_conv_b, 0).reshape(1, conv_ch),
                _per_group_rows(_layer(ev_a_log, 0), hpg), _per_group_rows(_layer(ev_dt_bias, 0), hpg),
                jnp.repeat(_layer(ev_d_skip, 0), SSD_HEAD_DIM).reshape(1, ssd_width),
                _layer(ev_ssd_norm, 0).reshape(1, ssd_width),
                batch=batch, seq=seq, ctx_len=ctx_len, attn_width=attn_width)
    x1 = _outproj(attn, 0, yssd, 0, _layer(ev_w_out, 0).astype(BF16), x2d, mod0, rows_per_mod=seq)

    fw = od_w_out.shape[1]
    cond1 = jnp.zeros((n_mod_rows, d), F32).at[:batch].set(c)
    mod1 = _ada(cond1, _layer(od_ada_w, 0), _layer(od_ada_b, 0)).reshape(3 * n_mod_rows, 1, d)
    uz, _ = _inproj(x1, mod1, _layer(od_norm_w, 0), _layer(od_w_in, 0).astype(BF16), None,
                    rows_per_mod=seq, mod_row0=0, col0=0, ncols=2 * fw)
    fg = _fourier_gate(uz, batch=batch, seq=seq, width=fw)
    x2 = _outproj(fg, 0, fg, 1, _layer(od_w_out, 0).astype(BF16), x1, mod1, rows_per_mod=seq)
    return x2.reshape(batch, seq, d)
```

```python
import functools
import math

import numpy as np
import jax
import jax.numpy as jnp
from jax import lax
from jax.experimental import pallas as pl
from jax.experimental.pallas import tpu as pltpu

F32 = jnp.float32
BF16 = jnp.bfloat16

GRID_W = 64
HEAD_DIM = 128
Q_PER_KV = 4
ROPE_THETA = 10000.0
ROPE_AXIS_DIM = HEAD_DIM // 2
SSD_HEAD_DIM = 64
N_SSD_GROUPS = 4
D_STATE = 128
D_CONV = 5
SSD_CHUNK = 128
N_FOURIER_GROUPS = 8
RMS_EPS = 1e-6

LANES = 128
SUBLANES = 8
VMEM_LIMIT_BYTES = 56 * 1024 * 1024

CONV_HALO = SSD_CHUNK // 2

ROW_TILE = 1024
INPROJ_COL_TILE = 2048
OUTPROJ_COL_TILE = 1024
ADA_COL_TILE = 1024
ATTN_Q_TILE = 512
ATTN_KEY_TILE = 4 * LANES
LOG2E = 1.4426950408889634


def _cparams(*sem):
    return pltpu.CompilerParams(dimension_semantics=sem, vmem_limit_bytes=VMEM_LIMIT_BYTES)


def _tile(n, pref):
    t = min(n, pref)
    while n % t:
        t //= 2
    return t


def _silu(v):
    h = 0.5 * v
    return h + h * jnp.tanh(h)


def _ada_kernel(c_ref, w_ref, b_ref, o_ref):
    c = c_ref[...]
    s = _silu(c).astype(BF16)
    o_ref[...] = jnp.dot(s, w_ref[...].astype(BF16), preferred_element_type=F32) + b_ref[...]


def _ada(cond, w, b):
    r, d = cond.shape
    n = w.shape[1]
    tn = _tile(n, ADA_COL_TILE)
    return pl.pallas_call(
        _ada_kernel,
        grid=(n // tn,),
        in_specs=[pl.BlockSpec((r, d), lambda j: (0, 0)),
                  pl.BlockSpec((d, tn), lambda j: (0, j)),
                  pl.BlockSpec((1, tn), lambda j: (0, j))],
        out_specs=pl.BlockSpec((r, tn), lambda j: (0, j)),
        out_shape=jax.ShapeDtypeStruct((r, n), F32),
        compiler_params=_cparams("parallel"),
        name="ada_mod",
    )(cond, w, b.reshape(1, n))


def _inproj_kernel(x_ref, sh_ref, sc_ref, nw_ref, w_ref, *rest, has_dt, sub):
    if has_dt:
        wdt_ref, o_ref, dt_ref, h_ref = rest
    else:
        o_ref, h_ref = rest
    tm = x_ref.shape[0]

    @pl.when(pl.program_id(1) == 0)
    def _():
        nw = nw_ref[...]
        sc = 1.0 + sc_ref[...]
        sh = sh_ref[...]

        def body(s, carry):
            r = pl.multiple_of(s * sub, sub)
            x = x_ref[pl.ds(r, sub), :]
            y = x * lax.rsqrt(jnp.mean(x * x, axis=-1, keepdims=True) + RMS_EPS) * nw
            h_ref[pl.ds(r, sub), :] = (y * sc + sh).astype(BF16)
            return carry

        lax.fori_loop(0, tm // sub, body, 0, unroll=8)
        if has_dt:
            dt_ref[...] = jnp.dot(h_ref[...], wdt_ref[...], preferred_element_type=F32)

    o_ref[...] = jnp.dot(h_ref[...], w_ref[...], preferred_element_type=F32).astype(o_ref.dtype)


def _inproj(x2d, mod, norm_w, w, wdt, *, rows_per_mod, mod_row0, col0, ncols):
    m, d = x2d.shape
    tm = _tile(rows_per_mod, ROW_TILE)
    has_dt = wdt is not None
    tn = _tile(math.gcd(ncols, col0), INPROJ_COL_TILE)
    cb0 = col0 // tn
    tiles_per_mod = rows_per_mod // tm

    def modrow(i):
        return mod_row0 + i // tiles_per_mod

    in_specs = [pl.BlockSpec((tm, d), lambda i, j: (i, 0)),
                pl.BlockSpec((None, 1, d), lambda i, j: (3 * modrow(i), 0, 0)),
                pl.BlockSpec((None, 1, d), lambda i, j: (3 * modrow(i) + 1, 0, 0)),
                pl.BlockSpec((1, d), lambda i, j: (0, 0)),
                pl.BlockSpec((d, tn), lambda i, j: (0, cb0 + j))]
    args = [x2d, mod, mod, norm_w.reshape(1, d), w]
    out_specs = [pl.BlockSpec((tm, tn), lambda i, j: (i, j))]
    out_shape = [jax.ShapeDtypeStruct((m, ncols), BF16)]
    if has_dt:
        ndt = wdt.shape[1]
        in_specs.append(pl.BlockSpec((d, ndt), lambda i, j: (0, 0)))
        args.append(wdt)
        out_specs.append(pl.BlockSpec((tm, ndt), lambda i, j: (i, 0)))
        out_shape.append(jax.ShapeDtypeStruct((m, ndt), F32))
    res = pl.pallas_call(
        functools.partial(_inproj_kernel, has_dt=has_dt, sub=min(32, tm)),
        grid=(m // tm, ncols // tn),
        in_specs=in_specs,
        out_specs=out_specs,
        out_shape=out_shape,
        scratch_shapes=[pltpu.VMEM((tm, d), BF16)],
        compiler_params=_cparams("parallel", "arbitrary"),
        name="mod_inproj",
    )(*args)
    return (res[0], res[1]) if has_dt else (res[0], None)


def _norm_rope(v, gain, cos, sin):
    ones = jnp.ones((HEAD_DIM, HEAD_DIM), BF16)
    ss = jnp.dot((v * v).astype(BF16), ones, preferred_element_type=F32)
    y = v * lax.rsqrt(ss * (1.0 / HEAD_DIM) + RMS_EPS) * gain
    if cos is not None:
        quarter = ROPE_AXIS_DIM // 2
        lane = lax.broadcasted_iota(jnp.int32, (1, HEAD_DIM), 1)
        partner = jnp.where(lane % ROPE_AXIS_DIM < quarter,
                            pltpu.roll(y, HEAD_DIM - quarter, 1), pltpu.roll(y, quarter, 1))
        y = y * cos + partner * sin
    return y.astype(BF16)


def _attn_kernel(q_ref, g_ref, kl_ref, vl_ref, kc_ref, vc_ref, qg_ref, kg_ref, cq_ref, sq_ref, ck_ref, sk_ref,
                 o_ref, k_scr, s_scr, *, nrep, tk):
    tq = q_ref.shape[0]
    rows = nrep * tq
    blocks, col = [], 0
    for v_ref in (vc_ref, vl_ref):
        n = v_ref.shape[0]
        for r0 in range(0, n, tk):
            blocks.append((v_ref, r0, min(tk, n - r0), col))
            col += min(tk, n - r0)

    @pl.when(pl.program_id(2) == 0)
    def _prep_keys():
        kg = kg_ref[...]
        for v_ref, r0, size, c0 in blocks:
            if v_ref is vc_ref:
                k_scr[pl.ds(c0, size), :] = _norm_rope(kc_ref[pl.ds(r0, size), :].astype(F32), kg, None, None)
            else:
                k_scr[pl.ds(c0, size), :] = _norm_rope(kl_ref[pl.ds(r0, size), :].astype(F32), kg,
                                                       ck_ref[pl.ds(r0, size), :], sk_ref[pl.ds(r0, size), :])

    qg = qg_ref[...]
    qs = jnp.concatenate([_norm_rope(q_ref[:, r * HEAD_DIM:(r + 1) * HEAD_DIM].astype(F32), qg,
                                     cq_ref[...], sq_ref[...]) for r in range(nrep)], axis=0)

    n_parts = 2
    half = rows // n_parts
    halves = tuple(slice(i * half, (i + 1) * half) for i in range(n_parts))
    m = [None] * n_parts
    mb = [None] * n_parts
    acc = [jnp.zeros((half, HEAD_DIM + LANES), F32) for _ in halves]

    def pass1(i, blk):
        _, r0, size, c0 = blk
        s = lax.dot_general(qs[halves[i]], k_scr[pl.ds(c0, size), :], (((1,), (1,)), ((), ())),
                            preferred_element_type=F32)
        s_scr[halves[i], c0:c0 + size] = s
        for c in range(size // LANES):
            sc = s[:, c * LANES:(c + 1) * LANES]
            m[i] = sc if m[i] is None else jnp.maximum(m[i], sc)

    def pass2(i, blk):
        v_ref, r0, size, c0 = blk
        v_aug = jnp.concatenate([v_ref[pl.ds(r0, size), :], jnp.ones((size, LANES), BF16)], axis=1)
        p = jnp.concatenate([jnp.exp2(s_scr[halves[i], c0 + c * LANES:c0 + (c + 1) * LANES] - mb[i])
                             for c in range(size // LANES)], axis=1).astype(BF16)
        acc[i] = acc[i] + jnp.dot(p, v_aug, preferred_element_type=F32)

    def row_max(i):
        mb[i] = jnp.broadcast_to(jnp.max(m[i], axis=-1, keepdims=True), (half, LANES))

    for phase in range(n_parts + 1):
        for blk in blocks:
            if phase < n_parts:
                pass1(phase, blk)
            if phase > 0:
                pass2(phase - 1, blk)
        if phase < n_parts:
            row_max(phase)
    acc = jnp.concatenate(acc, axis=0)
    o = acc[:, :HEAD_DIM] / acc[:, HEAD_DIM:]
    for r in range(nrep):
        g = g_ref[:, r * HEAD_DIM:(r + 1) * HEAD_DIM].astype(F32)
        o_ref[:, r * HEAD_DIM:(r + 1) * HEAD_DIM] = (o[r * tq:(r + 1) * tq] * _silu(g)).astype(o_ref.dtype)


def _attention(proj, projc, q_gain, k_gain, cos_t, sin_t, *, batch, seq, ctx_len, attn_width):
    n_kv = attn_width // HEAD_DIM // Q_PER_KV
    gw = Q_PER_KV * HEAD_DIM
    tq = _tile(seq, ATTN_Q_TILE)
    tk = ATTN_KEY_TILE
    assert ctx_len % LANES == 0 and seq % LANES == 0
    tpb = seq // tq
    k_cb = (2 * attn_width + _ssd_width(attn_width)) // HEAD_DIM
    v_cb = k_cb + n_kv
    return pl.pallas_call(
        functools.partial(_attn_kernel, nrep=Q_PER_KV, tk=tk),
        grid=(batch, n_kv, tpb),
        in_specs=[pl.BlockSpec((tq, gw), lambda b, h, t: (b * tpb + t, h)),
                  pl.BlockSpec((tq, gw), lambda b, h, t: (b * tpb + t, attn_width // gw + h)),
                  pl.BlockSpec((seq, HEAD_DIM), lambda b, h, t: (b, k_cb + h)),
                  pl.BlockSpec((seq, HEAD_DIM), lambda b, h, t: (b, v_cb + h)),
                  pl.BlockSpec((ctx_len, HEAD_DIM), lambda b, h, t: (b, h)),
                  pl.BlockSpec((ctx_len, HEAD_DIM), lambda b, h, t: (b, n_kv + h)),
                  pl.BlockSpec((1, HEAD_DIM), lambda b, h, t: (0, 0)),
                  pl.BlockSpec((1, HEAD_DIM), lambda b, h, t: (0, 0)),
                  pl.BlockSpec((tq, HEAD_DIM), lambda b, h, t: (t, 0)),
                  pl.BlockSpec((tq, HEAD_DIM), lambda b, h, t: (t, 0)),
                  pl.BlockSpec((seq, HEAD_DIM), lambda b, h, t: (0, 0)),
                  pl.BlockSpec((seq, HEAD_DIM), lambda b, h, t: (0, 0))],
        out_specs=pl.BlockSpec((tq, gw), lambda b, h, t: (b * tpb + t, h)),
        out_shape=jax.ShapeDtypeStruct((batch * seq, attn_width), BF16),
        scratch_shapes=[pltpu.VMEM((ctx_len + seq, HEAD_DIM), BF16),
                        pltpu.VMEM((Q_PER_KV * tq, ctx_len + seq), F32)],
        compiler_params=_cparams("parallel", "parallel", "arbitrary"),
        name="gqa_attention",
    )(proj, proj, proj, proj, projc, projc, q_gain, k_gain, cos_t, sin_t, cos_t, sin_t)


def _ssd_width(attn_width):
    return attn_width


def _conv_silu(groups, pad_ref, n_rows):
    ch = SSD_CHUNK
    halo = CONV_HALO
    win_rows = ch + 2 * halo
    widths = [g[0].shape[1] for g in groups]
    offs = [sum(widths[:i]) for i in range(len(groups))]
    total = sum(widths)
    pad_ref[pl.ds(0, halo), 0:total] = jnp.zeros((halo, total), BF16)
    pad_ref[pl.ds(halo + n_rows, halo), 0:total] = jnp.zeros((halo, total), BF16)

    def copy(i, carry):
        r = pl.multiple_of(i * ch, ch)
        for (src_ref, _, _, _), o, cw in zip(groups, offs, widths):
            pad_ref[pl.ds(r + halo, ch), o:o + cw] = src_ref[pl.ds(r, ch), :]
        return carry

    lax.fori_loop(0, n_rows // ch, copy, 0)
    w = jnp.concatenate([g[1][...] for g in groups], axis=1)
    bias = jnp.concatenate([g[2][...] for g in groups], axis=1)
    out_row = lax.broadcasted_iota(jnp.int32, (ch, win_rows), 0)
    win_row = lax.broadcasted_iota(jnp.int32, (ch, win_rows), 1)
    taps = [k for k in range(D_CONV) if k != D_CONV // 2]
    selects = [jnp.where(win_row == out_row + (halo + k - D_CONV // 2), 1.0, 0.0).astype(BF16) for k in taps]

    blk = 2 * LANES

    def body(i, carry):
        r = pl.multiple_of(i * ch, ch)
        ys = []
        for c0 in range(0, total, blk):
            cs = slice(c0, min(c0 + blk, total))
            win = pad_ref[pl.ds(r, win_rows), cs]
            acc = bias[:, cs] + win[halo:halo + ch].astype(F32) * w[D_CONV // 2:D_CONV // 2 + 1, cs]
            for k, select in zip(taps, selects):
                acc = acc + jnp.dot(select, win, preferred_element_type=F32) * w[k:k + 1, cs]
            ys.append(_silu(acc))
        y = jnp.concatenate(ys, axis=1)
        for (_, _, _, dst_ref), o, cw in zip(groups, offs, widths):
            dst_ref[pl.ds(r, ch), :] = y[:, o:o + cw].astype(dst_ref.dtype)
        return carry

    lax.fori_loop(0, n_rows // ch, body, 0, unroll=2)


def _ssd_kernel(xl_ref, bl_ref, cl_ref, zl_ref, dtl_ref, xc_ref, bc_ref, dtc_ref,
                wx_ref, wb_ref, wc_ref, bx_ref, bb_ref, bcv_ref, alog_ref, dtb_ref, dsk_ref, nrm_ref,
                o_ref,
                pad_s, xs_s, b_s, c_s, xsc_s, bcc_s, y_s, stf, stb, terms_s, bt_s, eb_s, *, hpg):
    ch = SSD_CHUNK
    seq = xl_ref.shape[0]
    ctx_len = xc_ref.shape[0]
    gw = xl_ref.shape[1]
    npair = gw // LANES
    assert 2 * SSD_HEAD_DIM == LANES and hpg == 2 * npair

    _conv_silu([(xl_ref, wx_ref, bx_ref, xs_s), (bl_ref, wb_ref, bb_ref, b_s), (cl_ref, wc_ref, bcv_ref, c_s)],
               pad_s, seq)
    _conv_silu([(xc_ref, wx_ref, bx_ref, xsc_s), (bc_ref, wb_ref, bb_ref, bcc_s)], pad_s, ctx_len)

    li = lax.broadcasted_iota(jnp.int32, (ch, ch), 0)
    si = lax.broadcasted_iota(jnp.int32, (ch, ch), 1)
    tri_t = (li <= si).astype(F32)
    nr = alog_ref.shape[0]
    lower = li >= si
    upper = si >= li
    diag = li == si
    xr = lax.broadcasted_iota(jnp.int32, (ch, gw), 0)
    xh = lax.broadcasted_iota(jnp.int32, (ch, gw), 1) // SSD_HEAD_DIM
    spread_f = jnp.where(xr == 2 * nr + xh, 1.0, 0.0).astype(BF16)
    spread_b = jnp.where(xr == 2 * nr + hpg + xh, 1.0, 0.0).astype(BF16)
    row_id = lax.broadcasted_iota(jnp.int32, (nr, ch), 0)
    lane = lax.broadcasted_iota(jnp.int32, (1, LANES), 1)
    head0 = lane < SSD_HEAD_DIM
    neg_a_t = -jnp.exp(alog_ref[...]) * LOG2E
    dt_bias_t = dtb_ref[...]

    def chunk_terms(dt_ref, r):
        x_t = dt_ref[pl.ds(r, ch), :].T[:nr] + dt_bias_t
        dtp_t = jnp.maximum(x_t, 0.0) + jnp.log1p(jnp.exp(-jnp.abs(x_t)))
        da_t = dtp_t * neg_a_t
        p_t = jnp.dot(da_t, tri_t, preferred_element_type=F32, precision=lax.Precision.HIGHEST)
        return dtp_t, p_t, p_t - da_t

    def columns(p_t, e_t):
        tot = p_t[:, ch - 1:ch]
        fac_t = jnp.exp2(jnp.where(row_id < hpg, p_t, tot - e_t))
        pad = jnp.zeros((ch - 3 * nr, ch), F32)
        return jnp.concatenate([p_t, e_t, fac_t, pad], axis=0).T

    def pair_sel(a, b):
        return jnp.where(head0, a, b)

    def block_diag_rhs(x_pair):
        zero = jnp.zeros_like(x_pair)
        return jnp.concatenate([jnp.where(head0, x_pair, zero), jnp.where(head0, zero, x_pair)], axis=0)

    def state_update(st_ref, bt, x_ref, r, w_rows, tot):
        for q in range(npair):
            j0, j1 = 2 * q, 2 * q + 1
            lhs = jnp.concatenate([(bt * w_rows[j0]).astype(BF16), (bt * w_rows[j1]).astype(BF16)], axis=1)
            rhs = block_diag_rhs(x_ref[pl.ds(r, ch), q * LANES:(q + 1) * LANES])
            new = jnp.dot(lhs, rhs, preferred_element_type=F32)
            dec = pair_sel(jnp.exp2(tot[j0]), jnp.exp2(tot[j1]))
            cols = slice(q * LANES, (q + 1) * LANES)
            st_ref[:, cols] = st_ref[:, cols] * dec + new

    def fwd_state_terms(dtp_t, p_t):
        w_rows, tot = [], []
        for j in range(hpg):
            t = p_t[j:j + 1, ch - 1:ch]
            tot.append(t)
            w_rows.append(jnp.exp2(t - p_t[j:j + 1, :]) * dtp_t[j:j + 1, :])
        return w_rows, tot

    def bwd_state_terms(dtp_t, p_t, e_t):
        w_rows, tot = [], []
        for j in range(hpg, 2 * hpg):
            tot.append(p_t[j:j + 1, ch - 1:ch])
            w_rows.append(jnp.exp2(e_t[j:j + 1, :]) * dtp_t[j:j + 1, :])
        return w_rows, tot

    stf[...] = jnp.zeros(stf.shape, F32)
    stb[...] = jnp.zeros(stb.shape, F32)
    n_cc = ctx_len // ch

    def load_terms(c):
        terms = terms_s[c]
        return terms[:nr], terms[nr:2 * nr], terms[2 * nr:]

    def ctx_fwd(c, carry):
        r = pl.multiple_of(c * ch, ch)
        dtp_t, p_t, e_t = chunk_terms(dtc_ref, r)
        terms_s[c] = jnp.concatenate([dtp_t, p_t, e_t], axis=0)
        bt = bcc_s[pl.ds(r, ch), :].astype(F32).T
        bt_s[c] = bt
        w_rows, tot = fwd_state_terms(dtp_t, p_t)
        state_update(stf, bt, xsc_s, r, w_rows, tot)
        return carry

    def ctx_bwd(i, carry):
        c = n_cc - 1 - i
        r = pl.multiple_of(c * ch, ch)
        dtp_t, p_t, e_t = load_terms(c)
        w_rows, tot = bwd_state_terms(dtp_t, p_t, e_t)
        state_update(stb, bt_s[c], xsc_s, r, w_rows, tot)
        return carry

    lax.fori_loop(0, n_cc, ctx_fwd, 0, unroll=2)
    lax.fori_loop(0, n_cc, ctx_bwd, 0, unroll=2)

    n_lc = seq // ch

    def lat_fwd(c, carry):
        r = pl.multiple_of(c * ch, ch)
        dtp_t, p_t, e_t = chunk_terms(dtl_ref, r)
        terms_s[c] = jnp.concatenate([dtp_t, p_t, e_t], axis=0)
        pe = columns(p_t, e_t)
        bm = b_s[pl.ds(r, ch), :]
        cm = c_s[pl.ds(r, ch), :]
        bt = bm.astype(F32).T
        bt_s[c] = bt
        cb = lax.dot_general(cm, bm, (((1,), (1,)), ((), ())), preferred_element_type=F32)
        y_off = jnp.dot(cm, stf[...].astype(BF16), preferred_element_type=F32)
        pe16 = pe.astype(BF16)
        y_off = y_off * jnp.dot(pe16, spread_f, preferred_element_type=F32)
        eb_s[pl.ds(r, ch), :] = jnp.dot(pe16, spread_b, preferred_element_type=F32)
        lg_t = jnp.log2(dtp_t)
        row_f = p_t - lg_t
        row_b = e_t + lg_t
        for q in range(npair):
            ms = []
            for j in (2 * q, 2 * q + 1):
                jb = hpg + j
                col = jnp.where(upper, jnp.broadcast_to(p_t[j:j + 1, :], (ch, ch)),
                                jnp.broadcast_to(e_t[jb:jb + 1, :], (ch, ch))).T
                t = jnp.where(lower, col - row_f[j:j + 1, :], row_b[jb:jb + 1, :] - col)
                ms.append((cb * (jnp.exp2(t) + jnp.where(diag, dtp_t[jb:jb + 1, :], 0.0))).astype(BF16))
            cols = slice(q * LANES, (q + 1) * LANES)
            rhs = block_diag_rhs(xs_s[pl.ds(r, ch), cols])
            y = jnp.dot(jnp.concatenate(ms, axis=1), rhs, preferred_element_type=F32)
            y_s[pl.ds(r, ch), cols] = y + y_off[:, cols]
        w_rows, tot = fwd_state_terms(dtp_t, p_t)
        state_update(stf, bt, xs_s, r, w_rows, tot)
        return carry

    lax.fori_loop(0, n_lc, lat_fwd, 0, unroll=8)

    dskip = dsk_ref[...]
    nrm = nrm_ref[...]

    def lat_bwd(i, carry):
        c = n_lc - 1 - i
        r = pl.multiple_of(c * ch, ch)
        dtp_t, p_t, e_t = load_terms(c)
        y_off = jnp.dot(c_s[pl.ds(r, ch), :], stb[...].astype(BF16), preferred_element_type=F32)
        y = y_s[pl.ds(r, ch), :] + y_off * eb_s[pl.ds(r, ch), :]
        w_rows, tot = bwd_state_terms(dtp_t, p_t, e_t)
        state_update(stb, bt_s[c], xs_s, r, w_rows, tot)
        y = (y + xs_s[pl.ds(r, ch), :].astype(F32) * dskip) * _silu(zl_ref[pl.ds(r, ch), :].astype(F32))
        y = y * lax.rsqrt(jnp.mean(y * y, axis=-1, keepdims=True) + RMS_EPS) * nrm
        o_ref[pl.ds(r, ch), :] = y.astype(o_ref.dtype)
        return carry

    lax.fori_loop(0, n_lc, lat_bwd, 0, unroll=4)


def _ssd(proj, dt, projc, dtc, conv_w, conv_b, alog_g, dtb_g, dskip_c, ssd_norm, *,
         batch, seq, ctx_len, attn_width):
    ssd_width = _ssd_width(attn_width)
    n_kv = attn_width // HEAD_DIM // Q_PER_KV
    kv_width = n_kv * HEAD_DIM
    gw = ssd_width // N_SSD_GROUPS
    hpg = gw // SSD_HEAD_DIM
    xs_off = 3 * attn_width + 2 * kv_width
    xs_off_c = 2 * kv_width
    assert xs_off % gw == 0 and xs_off_c % gw == 0 and D_STATE == LANES
    nb = N_SSD_GROUPS
    b_cb = (xs_off + ssd_width) // LANES
    c_cb = b_cb + nb
    b_cb_c = (xs_off_c + ssd_width) // LANES
    cw_b = ssd_width // LANES
    nr = alog_g.shape[1]
    in_specs = [
        pl.BlockSpec((seq, gw), lambda b, g: (b, xs_off // gw + g)),
        pl.BlockSpec((seq, LANES), lambda b, g: (b, b_cb + g)),
        pl.BlockSpec((seq, LANES), lambda b, g: (b, c_cb + g)),
        pl.BlockSpec((seq, gw), lambda b, g: (b, 2 * attn_width // gw + g)),
        pl.BlockSpec((seq, LANES), lambda b, g: (b, g)),
        pl.BlockSpec((ctx_len, gw), lambda b, g: (b, xs_off_c // gw + g)),
        pl.BlockSpec((ctx_len, LANES), lambda b, g: (b, b_cb_c + g)),
        pl.BlockSpec((ctx_len, LANES), lambda b, g: (b, g)),
        pl.BlockSpec((D_CONV, gw), lambda b, g: (0, g)),
        pl.BlockSpec((D_CONV, LANES), lambda b, g: (0, cw_b + g)),
        pl.BlockSpec((D_CONV, LANES), lambda b, g: (0, cw_b + nb + g)),
        pl.BlockSpec((1, gw), lambda b, g: (0, g)),
        pl.BlockSpec((1, LANES), lambda b, g: (0, cw_b + g)),
        pl.BlockSpec((1, LANES), lambda b, g: (0, cw_b + nb + g)),
        pl.BlockSpec((None, nr, LANES), lambda b, g: (g, 0, 0)),
        pl.BlockSpec((None, nr, LANES), lambda b, g: (g, 0, 0)),
        pl.BlockSpec((1, gw), lambda b, g: (0, g)),
        pl.BlockSpec((1, gw), lambda b, g: (0, g)),
    ]
    pad_rows = max(seq, ctx_len) + 2 * CONV_HALO
    n_chunks = max(seq, ctx_len) // SSD_CHUNK
    return pl.pallas_call(
        functools.partial(_ssd_kernel, hpg=hpg),
        grid=(batch, N_SSD_GROUPS),
        in_specs=in_specs,
        out_specs=pl.BlockSpec((seq, gw), lambda b, g: (b, g)),
        out_shape=jax.ShapeDtypeStruct((batch * seq, ssd_width), BF16),
        scratch_shapes=[pltpu.VMEM((pad_rows, gw + 2 * D_STATE), BF16),
                        pltpu.VMEM((seq, gw), BF16),
                        pltpu.VMEM((seq, LANES), BF16),
                        pltpu.VMEM((seq, LANES), BF16),
                        pltpu.VMEM((ctx_len, gw), BF16),
                        pltpu.VMEM((ctx_len, LANES), BF16),
                        pltpu.VMEM((seq, gw), F32),
                        pltpu.VMEM((D_STATE, gw), F32),
                        pltpu.VMEM((D_STATE, gw), F32),
                        pltpu.VMEM((n_chunks, 3 * nr, SSD_CHUNK), F32),
                        pltpu.VMEM((n_chunks, D_STATE, SSD_CHUNK), F32),
                        pltpu.VMEM((seq, gw), F32)],
        compiler_params=_cparams("parallel", "parallel"),
        name="ssd_bidir",
    )(proj, proj, proj, proj, dt, projc, projc, dtc,
      conv_w, conv_w, conv_w, conv_b, conv_b, conv_b, alog_g, dtb_g, dskip_c, ssd_norm)


def _outproj_kernel(a_ref, y_ref, w0_ref, w1_ref, x_ref, gate_ref, o_ref):
    acc = jnp.dot(a_ref[...], w0_ref[...], preferred_element_type=F32)
    acc = acc + jnp.dot(y_ref[...], w1_ref[...], preferred_element_type=F32)
    o_ref[...] = x_ref[...] + gate_ref[...] * acc


def _outproj(lhs0, cb0, lhs1, cb1, w, x2d, mod, *, rows_per_mod):
    m, d = x2d.shape
    kh = w.shape[0] // 2
    tm = _tile(rows_per_mod, ROW_TILE)
    tn = _tile(d, OUTPROJ_COL_TILE)
    tiles_per_mod = rows_per_mod // tm
    return pl.pallas_call(
        _outproj_kernel,
        grid=(m // tm, d // tn),
        in_specs=[pl.BlockSpec((tm, kh), lambda i, j: (i, cb0)),
                  pl.BlockSpec((tm, kh), lambda i, j: (i, cb1)),
                  pl.BlockSpec((kh, tn), lambda i, j: (0, j)),
                  pl.BlockSpec((kh, tn), lambda i, j: (1, j)),
                  pl.BlockSpec((tm, tn), lambda i, j: (i, j)),
                  pl.BlockSpec((None, 1, tn), lambda i, j: (3 * (i // tiles_per_mod) + 2, 0, j))],
        out_specs=pl.BlockSpec((tm, tn), lambda i, j: (i, j)),
        out_shape=jax.ShapeDtypeStruct((m, d), F32),
        compiler_params=_cparams("parallel", "arbitrary"),
        name="outproj_residual",
    )(lhs0, lhs1, w, w, x2d, mod)


FFT_RADIX = 8


def _cmul_root(z, k, n):
    re, im = z
    eighth = (8 * k) // n % 8
    assert (8 * k) % n == 0
    r = math.sqrt(0.5)
    if eighth == 0:
        return re, im
    if eighth == 1:
        return (re + im) * r, (im - re) * r
    if eighth == 2:
        return im, -re
    if eighth == 3:
        return (im - re) * r, -(re + im) * r
    nre, nim = _cmul_root(z, k - n // 2, n)
    return -nre, -nim


def _fft_small(v):
    n = len(v)
    if n == 1:
        return v
    ev, od = _fft_small(v[0::2]), _fft_small(v[1::2])
    out = [None] * n
    for k in range(n // 2):
        tr, ti = _cmul_root(od[k], k, n)
        out[k] = (ev[k][0] + tr, ev[k][1] + ti)
        out[k + n // 2] = (ev[k][0] - tr, ev[k][1] - ti)
    return out


def _fourier_kernel(u_ref, z_ref, w_ref, wc_ref, o_ref, uf_scr, y_scr, ab_scr, *, rows_per_step):
    seq, tn = u_ref.shape
    n2 = seq // FFT_RADIX
    rs = rows_per_step

    nslab = tn // LANES

    def widen(i, carry):
        r = pl.multiple_of(i * LANES, LANES)
        for c in range(nslab):
            uf_scr[c, pl.ds(r, LANES), :] = u_ref[pl.ds(r, LANES), c * LANES:(c + 1) * LANES].astype(F32)
        return carry

    lax.fori_loop(0, seq // LANES, widen, 0, unroll=4)
    for a in range(FFT_RADIX):
        ua = jnp.concatenate([uf_scr[c, pl.ds(a, n2, stride=FFT_RADIX), :] for c in range(nslab)], axis=1)
        y_scr[a] = jnp.dot(w_ref[a], ua.astype(BF16), preferred_element_type=F32)

    per_pass = ab_scr.shape[0] // (FFT_RADIX * rs)

    def combine_and_mix(g, carry):
        base = pl.multiple_of(g * (per_pass * rs), per_pass * rs)
        for ii in range(per_pass):
            r = base + ii * rs
            z = [(y_scr[a, pl.ds(r, rs), :], y_scr[a, pl.ds(n2 + r, rs), :]) for a in range(FFT_RADIX)]
            x = _fft_small(z)
            for k1 in range(FFT_RADIX):
                rows = pl.ds((ii * FFT_RADIX + k1) * rs, rs)
                ab_scr[rows, 0:tn] = x[k1][0].astype(BF16)
                ab_scr[rows, tn:2 * tn] = x[k1][1].astype(BF16)
        f = jnp.dot(ab_scr[...], wc_ref[...], preferred_element_type=F32)
        for ii in range(per_pass):
            r = base + ii * rs
            for k1 in range(FFT_RADIX):
                s0 = (ii * FFT_RADIX + k1) * rs
                gate = _silu(z_ref[pl.ds(k1 * n2 + r, rs), :].astype(F32))
                o_ref[pl.ds(k1 * n2 + r, rs), :] = (f[s0:s0 + rs] * gate).astype(o_ref.dtype)
        return carry

    lax.fori_loop(0, n2 // (per_pass * rs), combine_and_mix, 0)


def _fourier_gate(uz, *, batch, seq, width):
    n2 = seq // FFT_RADIX
    gw = width // N_FOURIER_GROUPS
    k2 = np.arange(n2, dtype=np.int64)[None, :, None]
    pos = FFT_RADIX * np.arange(n2, dtype=np.int64)[None, None, :] + np.arange(FFT_RADIX, dtype=np.int64)[:, None, None]
    ang = 2.0 * np.pi * ((k2 * pos) % seq).astype(np.float64) / seq
    w = jnp.asarray((np.concatenate([np.cos(ang), -np.sin(ang)], axis=1) / np.sqrt(seq)).astype(np.float32))
    kc = np.arange(gw, dtype=np.int64)
    angc = 2.0 * np.pi * ((kc[:, None] * kc[None, :]) % gw).astype(np.float64) / gw
    wc = jnp.asarray((np.concatenate([np.cos(angc), np.sin(angc)], axis=0) / np.sqrt(gw)).astype(np.float32))
    rs = 2 * SUBLANES
    per_pass = min(4, n2 // rs)
    return pl.pallas_call(
        functools.partial(_fourier_kernel, rows_per_step=rs),
        grid=(batch, N_FOURIER_GROUPS),
        in_specs=[pl.BlockSpec((seq, gw), lambda b, g: (b, g)),
                  pl.BlockSpec((seq, gw), lambda b, g: (b, N_FOURIER_GROUPS + g)),
                  pl.BlockSpec((FFT_RADIX, 2 * n2, n2), lambda b, g: (0, 0, 0)),
                  pl.BlockSpec((2 * gw, gw), lambda b, g: (0, 0))],
        out_specs=pl.BlockSpec((seq, gw), lambda b, g: (b, g)),
        out_shape=jax.ShapeDtypeStruct((batch * seq, width), BF16),
        scratch_shapes=[pltpu.VMEM((gw // LANES, seq, LANES), F32),
                        pltpu.VMEM((FFT_RADIX, 2 * n2, gw), F32),
                        pltpu.VMEM((per_pass * FFT_RADIX * rs, 2 * gw), BF16)],
        compiler_params=_cparams("parallel", "parallel"),
        name="fourier_mix_gate",
    )(uz, uz, w.astype(BF16), wc.astype(BF16))


def _rope_tables(seq):
    pos = jnp.arange(seq)
    row = (pos // GRID_W).astype(F32)
    col = (pos % GRID_W).astype(F32)
    inv_freq = ROPE_THETA ** (-jnp.arange(0, ROPE_AXIS_DIM, 2, dtype=F32) / ROPE_AXIS_DIM)
    ar, ac = row[:, None] * inv_freq, col[:, None] * inv_freq
    cos_t = jnp.concatenate([jnp.cos(ar), jnp.cos(ar), jnp.cos(ac), jnp.cos(ac)], axis=-1)
    sin_t = jnp.concatenate([-jnp.sin(ar), jnp.sin(ar), -jnp.sin(ac), jnp.sin(ac)], axis=-1)
    return cos_t, sin_t


def _layer(a, j):
    return a.reshape(a.shape[1:]) if a.shape[0] == 1 else a[j]


def _per_group_rows(v, hpg):
    g = v.shape[1] // hpg
    rows = -(-2 * hpg // SUBLANES) * SUBLANES
    t = v.reshape(2, g, hpg).transpose(1, 0, 2).reshape(g, 2 * hpg)
    t = jnp.pad(t, ((0, 0), (0, rows - 2 * hpg)))
    return jnp.broadcast_to(t[:, :, None], (g, rows, LANES))


def kernel(x, c, ctx, c_ctx, ev_norm_w, ev_ada_w, ev_ada_b, ev_w_in, ev_q_norm, ev_k_norm, ev_conv_w,
           ev_conv_b, ev_a_log, ev_dt_bias, ev_d_skip, ev_ssd_norm, ev_w_out, od_norm_w, od_ada_w,
           od_ada_b, od_w_in, od_w_out):
    batch, seq, d = x.shape
    ctx_len = ctx.shape[1]
    mix = ev_w_out.shape[1]
    attn_width = mix // 2
    ssd_width = mix - attn_width
    n_ssd_heads = ssd_width // SSD_HEAD_DIM
    hpg = n_ssd_heads // N_SSD_GROUPS
    n_kv = attn_width // HEAD_DIM // Q_PER_KV
    kv_width = n_kv * HEAD_DIM
    conv_ch = ssd_width + 2 * N_SSD_GROUPS * D_STATE
    ctx_col0 = 2 * attn_width + ssd_width
    n_main = ctx_col0 + 2 * kv_width + conv_ch
    m = batch * seq
    x2d = x.reshape(m, d)
    ctx2d = ctx.reshape(batch * ctx_len, d)
    n_mod_rows = 2 * SUBLANES
    assert batch + 1 <= n_mod_rows

    w_in = _layer(ev_w_in, 0)
    w_main = w_in.astype(BF16)
    qscale = LOG2E * HEAD_DIM ** -0.5
    q_gain = (_layer(ev_q_norm, 0) * qscale).reshape(1, HEAD_DIM)
    k_gain = _layer(ev_k_norm, 0).reshape(1, HEAD_DIM)
    cos_t, sin_t = _rope_tables(seq)
    dt_cols = w_main[:, n_main:]
    lane_pad = jnp.zeros((d, LANES - 2 * hpg), BF16)
    w_dt = jnp.concatenate(
        [blk for g in range(N_SSD_GROUPS)
         for blk in (dt_cols[:, g * hpg:(g + 1) * hpg],
                     dt_cols[:, n_ssd_heads + g * hpg:n_ssd_heads + (g + 1) * hpg], lane_pad)], axis=1)
    cond = jnp.zeros((n_mod_rows, d), F32).at[:batch].set(c).at[batch].set(c_ctx)
    mod0 = _ada(cond, _layer(ev_ada_w, 0), _layer(ev_ada_b, 0)).reshape(3 * n_mod_rows, 1, d)
    norm_w0 = _layer(ev_norm_w, 0)

    proj, dt = _inproj(x2d, mod0, norm_w0, w_main, w_dt,
                       rows_per_mod=seq, mod_row0=0, col0=0, ncols=n_main)
    projc, dtc = _inproj(ctx2d, mod0, norm_w0, w_main, w_dt,
                         rows_per_mod=batch * ctx_len, mod_row0=batch, col0=ctx_col0, ncols=n_main - ctx_col0)

    attn = _attention(proj, projc, q_gain, k_gain, cos_t, sin_t,
                      batch=batch, seq=seq, ctx_len=ctx_len, attn_width=attn_width)
    yssd = _ssd(proj, dt, projc, dtc, _layer(ev_conv_w, 0), _layer(ev_conv_b, 0).reshape(1, conv_ch),
                _per_group_rows(_layer(ev_a_log, 0), hpg), _per_group_rows(_layer(ev_dt_bias, 0), hpg),
                jnp.repeat(_layer(ev_d_skip, 0), SSD_HEAD_DIM).reshape(1, ssd_width),
                _layer(ev_ssd_norm, 0).reshape(1, ssd_width),
                batch=batch, seq=seq, ctx_len=ctx_len, attn_width=attn_width)
    x1 = _outproj(attn, 0, yssd, 0, _layer(ev_w_out, 0).astype(BF16), x2d, mod0, rows_per_mod=seq)

    fw = od_w_out.shape[1]
    cond1 = jnp.zeros((n_mod_rows, d), F32).at[:batch].set(c)
    mod1 = _ada(cond1, _layer(od_ada_w, 0), _layer(od_ada_b, 0)).reshape(3 * n_mod_rows, 1, d)
    uz, _ = _inproj(x1, mod1, _layer(od_norm_w, 0), _layer(od_w_in, 0).astype(BF16), None,
                    rows_per_mod=seq, mod_row0=0, col0=0, ncols=2 * fw)
    fg = _fourier_gate(uz, batch=batch, seq=seq, width=fw)
    x2 = _outproj(fg, 0, fg, 1, _layer(od_w_out, 0).astype(BF16), x1, mod1, rows_per_mod=seq)
    return x2.reshape(batch, seq, d)
```

```python
import functools
import math

import numpy as np
import jax
import jax.numpy as jnp
from jax import lax
from jax.experimental import pallas as pl
from jax.experimental.pallas import tpu as pltpu

F32 = jnp.float32
BF16 = jnp.bfloat16

GRID_W = 64
HEAD_DIM = 128
Q_PER_KV = 4
ROPE_THETA = 10000.0
ROPE_AXIS_DIM = HEAD_DIM // 2
SSD_HEAD_DIM = 64
N_SSD_GROUPS = 4
D_STATE = 128
D_CONV = 5
SSD_CHUNK = 128
N_FOURIER_GROUPS = 8
RMS_EPS = 1e-6

LANES = 128
SUBLANES = 8
VMEM_LIMIT_BYTES = 56 * 1024 * 1024

CONV_HALO = SSD_CHUNK // 2

ROW_TILE = 1024
INPROJ_COL_TILE = 2048
OUTPROJ_COL_TILE = 1024
ADA_COL_TILE = 1024
ATTN_Q_TILE = 512
ATTN_KEY_TILE = 4 * LANES
LOG2E = 1.4426950408889634


def _cparams(*sem):
    return pltpu.CompilerParams(dimension_semantics=sem, vmem_limit_bytes=VMEM_LIMIT_BYTES)


def _tile(n, pref):
    t = min(n, pref)
    while n % t:
        t //= 2
    return t


def _silu(v):
    h = 0.5 * v
    return h + h * jnp.tanh(h)


def _ada_kernel(c_ref, w_ref, b_ref, o_ref):
    c = c_ref[...]
    s = _silu(c).astype(BF16)
    o_ref[...] = jnp.dot(s, w_ref[...].astype(BF16), preferred_element_type=F32) + b_ref[...]


def _ada(cond, w, b):
    r, d = cond.shape
    n = w.shape[1]
    tn = _tile(n, ADA_COL_TILE)
    return pl.pallas_call(
        _ada_kernel,
        grid=(n // tn,),
        in_specs=[pl.BlockSpec((r, d), lambda j: (0, 0)),
                  pl.BlockSpec((d, tn), lambda j: (0, j)),
                  pl.BlockSpec((1, tn), lambda j: (0, j))],
        out_specs=pl.BlockSpec((r, tn), lambda j: (0, j)),
        out_shape=jax.ShapeDtypeStruct((r, n), F32),
        compiler_params=_cparams("parallel"),
        name="ada_mod",
    )(cond, w, b.reshape(1, n))


def _inproj_kernel(x_ref, sh_ref, sc_ref, nw_ref, w_ref, *rest, has_dt, sub):
    if has_dt:
        wdt_ref, o_ref, dt_ref, h_ref = rest
    else:
        o_ref, h_ref = rest
    tm = x_ref.shape[0]

    @pl.when(pl.program_id(1) == 0)
    def _():
        gain = nw_ref[...] * (1.0 + sc_ref[...])
        sh = sh_ref[...]

        def body(s, carry):
            r = pl.multiple_of(s * sub, sub)
            x = x_ref[pl.ds(r, sub), :]
            y = x * lax.rsqrt(jnp.mean(x * x, axis=-1, keepdims=True) + RMS_EPS)
            h_ref[pl.ds(r, sub), :] = (y * gain + sh).astype(BF16)
            return carry

        lax.fori_loop(0, tm // sub, body, 0, unroll=8)
        if has_dt:
            dt_ref[...] = jnp.dot(h_ref[...], wdt_ref[...], preferred_element_type=F32)

    o_ref[...] = jnp.dot(h_ref[...], w_ref[...], preferred_element_type=F32).astype(o_ref.dtype)


def _inproj(x2d, mod, norm_w, w, wdt, *, rows_per_mod, mod_row0, col0, ncols):
    m, d = x2d.shape
    tm = _tile(rows_per_mod, ROW_TILE)
    has_dt = wdt is not None
    tn = _tile(math.gcd(ncols, col0), INPROJ_COL_TILE)
    cb0 = col0 // tn
    tiles_per_mod = rows_per_mod // tm

    def modrow(i):
        return mod_row0 + i // tiles_per_mod

    in_specs = [pl.BlockSpec((tm, d), lambda i, j: (i, 0)),
                pl.BlockSpec((None, 1, d), lambda i, j: (3 * modrow(i), 0, 0)),
                pl.BlockSpec((None, 1, d), lambda i, j: (3 * modrow(i) + 1, 0, 0)),
                pl.BlockSpec((1, d), lambda i, j: (0, 0)),
                pl.BlockSpec((d, tn), lambda i, j: (0, cb0 + j))]
    args = [x2d, mod, mod, norm_w.reshape(1, d), w]
    out_specs = [pl.BlockSpec((tm, tn), lambda i, j: (i, j))]
    out_shape = [jax.ShapeDtypeStruct((m, ncols), BF16)]
    if has_dt:
        ndt = wdt.shape[1]
        in_specs.append(pl.BlockSpec((d, ndt), lambda i, j: (0, 0)))
        args.append(wdt)
        out_specs.append(pl.BlockSpec((tm, ndt), lambda i, j: (i, 0)))
        out_shape.append(jax.ShapeDtypeStruct((m, ndt), F32))
    res = pl.pallas_call(
        functools.partial(_inproj_kernel, has_dt=has_dt, sub=min(32, tm)),
        grid=(m // tm, ncols // tn),
        in_specs=in_specs,
        out_specs=out_specs,
        out_shape=out_shape,
        scratch_shapes=[pltpu.VMEM((tm, d), BF16)],
        compiler_params=_cparams("parallel", "arbitrary"),
        name="mod_inproj",
    )(*args)
    return (res[0], res[1]) if has_dt else (res[0], None)


def _norm_rope(v, gain, cos, sin):
    ones = jnp.ones((HEAD_DIM, HEAD_DIM), BF16)
    ss = jnp.dot((v * v).astype(BF16), ones, preferred_element_type=F32)
    y = v * lax.rsqrt(ss * (1.0 / HEAD_DIM) + RMS_EPS) * gain
    if cos is not None:
        quarter = ROPE_AXIS_DIM // 2
        lane = lax.broadcasted_iota(jnp.int32, (1, HEAD_DIM), 1)
        partner = jnp.where(lane % ROPE_AXIS_DIM < quarter,
                            pltpu.roll(y, HEAD_DIM - quarter, 1), pltpu.roll(y, quarter, 1))
        y = y * cos + partner * sin
    return y.astype(BF16)


def _attn_kernel(q_ref, g_ref, kl_ref, vl_ref, kc_ref, vc_ref, qg_ref, kg_ref, cq_ref, sq_ref, ck_ref, sk_ref,
                 o_ref, k_scr, s_scr, *, nrep, tk):
    tq = q_ref.shape[0]
    rows = nrep * tq
    blocks, col = [], 0
    for v_ref in (vc_ref, vl_ref):
        n = v_ref.shape[0]
        for r0 in range(0, n, tk):
            blocks.append((v_ref, r0, min(tk, n - r0), col))
            col += min(tk, n - r0)

    @pl.when(pl.program_id(2) == 0)
    def _prep_keys():
        kg = kg_ref[...]
        for v_ref, r0, size, c0 in blocks:
            if v_ref is vc_ref:
                k_scr[pl.ds(c0, size), :] = _norm_rope(kc_ref[pl.ds(r0, size), :].astype(F32), kg, None, None)
            else:
                k_scr[pl.ds(c0, size), :] = _norm_rope(kl_ref[pl.ds(r0, size), :].astype(F32), kg,
                                                       ck_ref[pl.ds(r0, size), :], sk_ref[pl.ds(r0, size), :])

    qg = qg_ref[...]
    qs = jnp.concatenate([_norm_rope(q_ref[:, r * HEAD_DIM:(r + 1) * HEAD_DIM].astype(F32), qg,
                                     cq_ref[...], sq_ref[...]) for r in range(nrep)], axis=0)

    n_parts = 2
    half = rows // n_parts
    halves = tuple(slice(i * half, (i + 1) * half) for i in range(n_parts))
    m = [None] * n_parts
    mb = [None] * n_parts
    acc = [jnp.zeros((half, HEAD_DIM + LANES), F32) for _ in halves]

    def pass1(i, blk):
        _, r0, size, c0 = blk
        s = lax.dot_general(qs[halves[i]], k_scr[pl.ds(c0, size), :], (((1,), (1,)), ((), ())),
                            preferred_element_type=F32)
        s_scr[halves[i], c0:c0 + size] = s
        for c in range(size // LANES):
            sc = s[:, c * LANES:(c + 1) * LANES]
            m[i] = sc if m[i] is None else jnp.maximum(m[i], sc)

    def pass2(i, blk):
        v_ref, r0, size, c0 = blk
        v_aug = jnp.concatenate([v_ref[pl.ds(r0, size), :], jnp.ones((size, LANES), BF16)], axis=1)
        p = jnp.concatenate([jnp.exp2(s_scr[halves[i], c0 + c * LANES:c0 + (c + 1) * LANES] - mb[i])
                             for c in range(size // LANES)], axis=1).astype(BF16)
        acc[i] = acc[i] + jnp.dot(p, v_aug, preferred_element_type=F32)

    def row_max(i):
        mb[i] = jnp.broadcast_to(jnp.max(m[i], axis=-1, keepdims=True), (half, LANES))

    for phase in range(n_parts + 1):
        for blk in blocks:
            if phase < n_parts:
                pass1(phase, blk)
            if phase > 0:
                pass2(phase - 1, blk)
        if phase < n_parts:
            row_max(phase)
    acc = jnp.concatenate(acc, axis=0)
    o = acc[:, :HEAD_DIM] / acc[:, HEAD_DIM:]
    for r in range(nrep):
        g = g_ref[:, r * HEAD_DIM:(r + 1) * HEAD_DIM].astype(F32)
        o_ref[:, r * HEAD_DIM:(r + 1) * HEAD_DIM] = (o[r * tq:(r + 1) * tq] * _silu(g)).astype(o_ref.dtype)


def _attention(proj, projc, q_gain, k_gain, cos_t, sin_t, *, batch, seq, ctx_len, attn_width):
    n_kv = attn_width // HEAD_DIM // Q_PER_KV
    gw = Q_PER_KV * HEAD_DIM
    tq = _tile(seq, ATTN_Q_TILE)
    tk = ATTN_KEY_TILE
    assert ctx_len % LANES == 0 and seq % LANES == 0
    tpb = seq // tq
    k_cb = (2 * attn_width + _ssd_width(attn_width)) // HEAD_DIM
    v_cb = k_cb + n_kv
    return pl.pallas_call(
        functools.partial(_attn_kernel, nrep=Q_PER_KV, tk=tk),
        grid=(batch, n_kv, tpb),
        in_specs=[pl.BlockSpec((tq, gw), lambda b, h, t: (b * tpb + t, h)),
                  pl.BlockSpec((tq, gw), lambda b, h, t: (b * tpb + t, attn_width // gw + h)),
                  pl.BlockSpec((seq, HEAD_DIM), lambda b, h, t: (b, k_cb + h)),
                  pl.BlockSpec((seq, HEAD_DIM), lambda b, h, t: (b, v_cb + h)),
                  pl.BlockSpec((ctx_len, HEAD_DIM), lambda b, h, t: (b, h)),
                  pl.BlockSpec((ctx_len, HEAD_DIM), lambda b, h, t: (b, n_kv + h)),
                  pl.BlockSpec((1, HEAD_DIM), lambda b, h, t: (0, 0)),
                  pl.BlockSpec((1, HEAD_DIM), lambda b, h, t: (0, 0)),
                  pl.BlockSpec((tq, HEAD_DIM), lambda b, h, t: (t, 0)),
                  pl.BlockSpec((tq, HEAD_DIM), lambda b, h, t: (t, 0)),
                  pl.BlockSpec((seq, HEAD_DIM), lambda b, h, t: (0, 0)),
                  pl.BlockSpec((seq, HEAD_DIM), lambda b, h, t: (0, 0))],
        out_specs=pl.BlockSpec((tq, gw), lambda b, h, t: (b * tpb + t, h)),
        out_shape=jax.ShapeDtypeStruct((batch * seq, attn_width), BF16),
        scratch_shapes=[pltpu.VMEM((ctx_len + seq, HEAD_DIM), BF16),
                        pltpu.VMEM((Q_PER_KV * tq, ctx_len + seq), F32)],
        compiler_params=_cparams("parallel", "parallel", "arbitrary"),
        name="gqa_attention",
    )(proj, proj, proj, proj, projc, projc, q_gain, k_gain, cos_t, sin_t, cos_t, sin_t)


def _ssd_width(attn_width):
    return attn_width


def _conv_silu(groups, pad_ref, n_rows):
    ch = SSD_CHUNK
    halo = CONV_HALO
    win_rows = ch + 2 * halo
    widths = [g[0].shape[1] for g in groups]
    offs = [sum(widths[:i]) for i in range(len(groups))]
    total = sum(widths)
    pad_ref[pl.ds(0, halo), 0:total] = jnp.zeros((halo, total), BF16)
    pad_ref[pl.ds(halo + n_rows, halo), 0:total] = jnp.zeros((halo, total), BF16)

    def copy(i, carry):
        r = pl.multiple_of(i * ch, ch)
        for (src_ref, _, _, _), o, cw in zip(groups, offs, widths):
            pad_ref[pl.ds(r + halo, ch), o:o + cw] = src_ref[pl.ds(r, ch), :]
        return carry

    lax.fori_loop(0, n_rows // ch, copy, 0)
    w = jnp.concatenate([g[1][...] for g in groups], axis=1)
    bias = jnp.concatenate([g[2][...] for g in groups], axis=1)
    out_row = lax.broadcasted_iota(jnp.int32, (ch, win_rows), 0)
    win_row = lax.broadcasted_iota(jnp.int32, (ch, win_rows), 1)
    taps = [k for k in range(D_CONV) if k != D_CONV // 2]
    selects = [jnp.where(win_row == out_row + (halo + k - D_CONV // 2), 1.0, 0.0).astype(BF16) for k in taps]

    blk = 2 * LANES

    def body(i, carry):
        r = pl.multiple_of(i * ch, ch)
        ys = []
        for c0 in range(0, total, blk):
            cs = slice(c0, min(c0 + blk, total))
            win = pad_ref[pl.ds(r, win_rows), cs]
            acc = bias[:, cs] + win[halo:halo + ch].astype(F32) * w[D_CONV // 2:D_CONV // 2 + 1, cs]
            for k, select in zip(taps, selects):
                acc = acc + jnp.dot(select, win, preferred_element_type=F32) * w[k:k + 1, cs]
            ys.append(_silu(acc))
        y = jnp.concatenate(ys, axis=1)
        for (_, _, _, dst_ref), o, cw in zip(groups, offs, widths):
            dst_ref[pl.ds(r, ch), :] = y[:, o:o + cw].astype(dst_ref.dtype)
        return carry

    lax.fori_loop(0, n_rows // ch, body, 0, unroll=2)


def _ssd_kernel(xl_ref, bl_ref, cl_ref, zl_ref, dtl_ref, xc_ref, bc_ref, dtc_ref,
                wx_ref, wb_ref, wc_ref, bx_ref, bb_ref, bcv_ref, alog_ref, dtb_ref, dsk_ref, nrm_ref,
                o_ref,
                pad_s, xs_s, b_s, c_s, xsc_s, bcc_s, y_s, stf, stb, terms_s, bt_s, eb_s, *, hpg):
    ch = SSD_CHUNK
    seq = xl_ref.shape[0]
    ctx_len = xc_ref.shape[0]
    gw = xl_ref.shape[1]
    npair = gw // LANES
    assert 2 * SSD_HEAD_DIM == LANES and hpg == 2 * npair

    _conv_silu([(xl_ref, wx_ref, bx_ref, xs_s), (bl_ref, wb_ref, bb_ref, b_s), (cl_ref, wc_ref, bcv_ref, c_s)],
               pad_s, seq)
    _conv_silu([(xc_ref, wx_ref, bx_ref, xsc_s), (bc_ref, wb_ref, bb_ref, bcc_s)], pad_s, ctx_len)

    li = lax.broadcasted_iota(jnp.int32, (ch, ch), 0)
    si = lax.broadcasted_iota(jnp.int32, (ch, ch), 1)
    tri_t = (li <= si).astype(F32)
    nr = alog_ref.shape[0]
    below = li > si
    upper = si >= li
    diag = li == si
    xr = lax.broadcasted_iota(jnp.int32, (ch, gw), 0)
    xh = lax.broadcasted_iota(jnp.int32, (ch, gw), 1) // SSD_HEAD_DIM
    spread_f = jnp.where(xr == 2 * nr + xh, 1.0, 0.0).astype(BF16)
    spread_b = jnp.where(xr == 2 * nr + hpg + xh, 1.0, 0.0).astype(BF16)
    row_id = lax.broadcasted_iota(jnp.int32, (nr, ch), 0)
    lane = lax.broadcasted_iota(jnp.int32, (1, LANES), 1)
    head0 = lane < SSD_HEAD_DIM
    neg_a_t = -jnp.exp(alog_ref[...]) * LOG2E
    dt_bias_t = dtb_ref[...]

    def chunk_terms(dt_ref, r):
        x_t = dt_ref[pl.ds(r, ch), :].T[:nr] + dt_bias_t
        dtp_t = jnp.maximum(x_t, 0.0) + jnp.log1p(jnp.exp(-jnp.abs(x_t)))
        da_t = dtp_t * neg_a_t
        p_t = jnp.dot(da_t, tri_t, preferred_element_type=F32, precision=lax.Precision.HIGHEST)
        return dtp_t, p_t, p_t - da_t

    def columns(p_t, e_t):
        tot = p_t[:, ch - 1:ch]
        fac_t = jnp.exp2(jnp.where(row_id < hpg, p_t, tot - e_t))
        pad = jnp.zeros((ch - 3 * nr, ch), F32)
        return jnp.concatenate([p_t, e_t, fac_t, pad], axis=0).T

    def pair_sel(a, b):
        return jnp.where(head0, a, b)

    def block_diag_rhs(x_pair):
        zero = jnp.zeros_like(x_pair)
        return jnp.concatenate([jnp.where(head0, x_pair, zero), jnp.where(head0, zero, x_pair)], axis=0)

    def state_update(st_ref, bt, x_ref, r, w_rows, tot):
        for q in range(npair):
            j0, j1 = 2 * q, 2 * q + 1
            lhs = jnp.concatenate([(bt * w_rows[j0]).astype(BF16), (bt * w_rows[j1]).astype(BF16)], axis=1)
            rhs = block_diag_rhs(x_ref[pl.ds(r, ch), q * LANES:(q + 1) * LANES])
            new = jnp.dot(lhs, rhs, preferred_element_type=F32)
            dec = pair_sel(jnp.exp2(tot[j0]), jnp.exp2(tot[j1]))
            cols = slice(q * LANES, (q + 1) * LANES)
            st_ref[:, cols] = st_ref[:, cols] * dec + new

    def fwd_state_terms(dtp_t, p_t):
        w_rows, tot = [], []
        for j in range(hpg):
            t = p_t[j:j + 1, ch - 1:ch]
            tot.append(t)
            w_rows.append(jnp.exp2(t - p_t[j:j + 1, :]) * dtp_t[j:j + 1, :])
        return w_rows, tot

    def bwd_state_terms(dtp_t, p_t, e_t):
        w_rows, tot = [], []
        for j in range(hpg, 2 * hpg):
            tot.append(p_t[j:j + 1, ch - 1:ch])
            w_rows.append(jnp.exp2(e_t[j:j + 1, :]) * dtp_t[j:j + 1, :])
        return w_rows, tot

    stf[...] = jnp.zeros(stf.shape, F32)
    stb[...] = jnp.zeros(stb.shape, F32)
    n_cc = ctx_len // ch

    def load_terms(c):
        terms = terms_s[c]
        return terms[:nr], terms[nr:2 * nr], terms[2 * nr:]

    def ctx_fwd(c, carry):
        r = pl.multiple_of(c * ch, ch)
        dtp_t, p_t, e_t = chunk_terms(dtc_ref, r)
        terms_s[c] = jnp.concatenate([dtp_t, p_t, e_t], axis=0)
        bt = bcc_s[pl.ds(r, ch), :].astype(F32).T
        bt_s[c] = bt
        w_rows, tot = fwd_state_terms(dtp_t, p_t)
        state_update(stf, bt, xsc_s, r, w_rows, tot)
        return carry

    def ctx_bwd(i, carry):
        c = n_cc - 1 - i
        r = pl.multiple_of(c * ch, ch)
        dtp_t, p_t, e_t = load_terms(c)
        w_rows, tot = bwd_state_terms(dtp_t, p_t, e_t)
        state_update(stb, bt_s[c], xsc_s, r, w_rows, tot)
        return carry

    lax.fori_loop(0, n_cc, ctx_fwd, 0, unroll=2)
    lax.fori_loop(0, n_cc, ctx_bwd, 0, unroll=2)

    n_lc = seq // ch

    def lat_fwd(c, carry):
        r = pl.multiple_of(c * ch, ch)
        dtp_t, p_t, e_t = chunk_terms(dtl_ref, r)
        terms_s[c] = jnp.concatenate([dtp_t, p_t, e_t], axis=0)
        pe = columns(p_t, e_t)
        bm = b_s[pl.ds(r, ch), :]
        cm = c_s[pl.ds(r, ch), :]
        bt = bm.astype(F32).T
        bt_s[c] = bt
        cb = lax.dot_general(cm, bm, (((1,), (1,)), ((), ())), preferred_element_type=F32)
        y_off = jnp.dot(cm, stf[...].astype(BF16), preferred_element_type=F32)
        pe16 = pe.astype(BF16)
        y_off = y_off * jnp.dot(pe16, spread_f, preferred_element_type=F32)
        eb_s[pl.ds(r, ch), :] = jnp.dot(pe16, spread_b, preferred_element_type=F32)
        lg_t = jnp.log2(dtp_t)
        row_f = p_t - lg_t
        row_b = e_t + lg_t
        lg_sum = jnp.log2(dtp_t[:hpg] + dtp_t[hpg:2 * hpg])
        for q in range(npair):
            ms = []
            for j in (2 * q, 2 * q + 1):
                jb = hpg + j
                col = jnp.where(upper, jnp.broadcast_to(p_t[j:j + 1, :], (ch, ch)),
                                jnp.broadcast_to(e_t[jb:jb + 1, :], (ch, ch))).T
                t = jnp.where(below, col - row_f[j:j + 1, :],
                              jnp.where(diag, lg_sum[j:j + 1, :], row_b[jb:jb + 1, :] - col))
                ms.append((cb * jnp.exp2(t)).astype(BF16))
            cols = slice(q * LANES, (q + 1) * LANES)
            rhs = block_diag_rhs(xs_s[pl.ds(r, ch), cols])
            y = jnp.dot(jnp.concatenate(ms, axis=1), rhs, preferred_element_type=F32)
            y_s[pl.ds(r, ch), cols] = y + y_off[:, cols]
        w_rows, tot = fwd_state_terms(dtp_t, p_t)
        state_update(stf, bt, xs_s, r, w_rows, tot)
        return carry

    lax.fori_loop(0, n_lc, lat_fwd, 0, unroll=8)

    dskip = dsk_ref[...]
    nrm = nrm_ref[...]

    def lat_bwd(i, carry):
        c = n_lc - 1 - i
        r = pl.multiple_of(c * ch, ch)
        dtp_t, p_t, e_t = load_terms(c)
        y_off = jnp.dot(c_s[pl.ds(r, ch), :], stb[...].astype(BF16), preferred_element_type=F32)
        y = y_s[pl.ds(r, ch), :] + y_off * eb_s[pl.ds(r, ch), :]
        w_rows, tot = bwd_state_terms(dtp_t, p_t, e_t)
        state_update(stb, bt_s[c], xs_s, r, w_rows, tot)
        y = (y + xs_s[pl.ds(r, ch), :].astype(F32) * dskip) * _silu(zl_ref[pl.ds(r, ch), :].astype(F32))
        y = y * lax.rsqrt(jnp.mean(y * y, axis=-1, keepdims=True) + RMS_EPS) * nrm
        o_ref[pl.ds(r, ch), :] = y.astype(o_ref.dtype)
        return carry

    lax.fori_loop(0, n_lc, lat_bwd, 0, unroll=4)


def _ssd(proj, dt, projc, dtc, conv_w, conv_b, alog_g, dtb_g, dskip_c, ssd_norm, *,
         batch, seq, ctx_len, attn_width):
    ssd_width = _ssd_width(attn_width)
    n_kv = attn_width // HEAD_DIM // Q_PER_KV
    kv_width = n_kv * HEAD_DIM
    gw = ssd_width // N_SSD_GROUPS
    hpg = gw // SSD_HEAD_DIM
    xs_off = 3 * attn_width + 2 * kv_width
    xs_off_c = 2 * kv_width
    assert xs_off % gw == 0 and xs_off_c % gw == 0 and D_STATE == LANES
    nb = N_SSD_GROUPS
    b_cb = (xs_off + ssd_width) // LANES
    c_cb = b_cb + nb
    b_cb_c = (xs_off_c + ssd_width) // LANES
    cw_b = ssd_width // LANES
    nr = alog_g.shape[1]
    in_specs = [
        pl.BlockSpec((seq, gw), lambda b, g: (b, xs_off // gw + g)),
        pl.BlockSpec((seq, LANES), lambda b, g: (b, b_cb + g)),
        pl.BlockSpec((seq, LANES), lambda b, g: (b, c_cb + g)),
        pl.BlockSpec((seq, gw), lambda b, g: (b, 2 * attn_width // gw + g)),
        pl.BlockSpec((seq, LANES), lambda b, g: (b, g)),
        pl.BlockSpec((ctx_len, gw), lambda b, g: (b, xs_off_c // gw + g)),
        pl.BlockSpec((ctx_len, LANES), lambda b, g: (b, b_cb_c + g)),
        pl.BlockSpec((ctx_len, LANES), lambda b, g: (b, g)),
        pl.BlockSpec((D_CONV, gw), lambda b, g: (0, g)),
        pl.BlockSpec((D_CONV, LANES), lambda b, g: (0, cw_b + g)),
        pl.BlockSpec((D_CONV, LANES), lambda b, g: (0, cw_b + nb + g)),
        pl.BlockSpec((1, gw), lambda b, g: (0, g)),
        pl.BlockSpec((1, LANES), lambda b, g: (0, cw_b + g)),
        pl.BlockSpec((1, LANES), lambda b, g: (0, cw_b + nb + g)),
        pl.BlockSpec((None, nr, LANES), lambda b, g: (g, 0, 0)),
        pl.BlockSpec((None, nr, LANES), lambda b, g: (g, 0, 0)),
        pl.BlockSpec((1, gw), lambda b, g: (0, g)),
        pl.BlockSpec((1, gw), lambda b, g: (0, g)),
    ]
    pad_rows = max(seq, ctx_len) + 2 * CONV_HALO
    n_chunks = max(seq, ctx_len) // SSD_CHUNK
    return pl.pallas_call(
        functools.partial(_ssd_kernel, hpg=hpg),
        grid=(batch, N_SSD_GROUPS),
        in_specs=in_specs,
        out_specs=pl.BlockSpec((seq, gw), lambda b, g: (b, g)),
        out_shape=jax.ShapeDtypeStruct((batch * seq, ssd_width), BF16),
        scratch_shapes=[pltpu.VMEM((pad_rows, gw + 2 * D_STATE), BF16),
                        pltpu.VMEM((seq, gw), BF16),
                        pltpu.VMEM((seq, LANES), BF16),
                        pltpu.VMEM((seq, LANES), BF16),
                        pltpu.VMEM((ctx_len, gw), BF16),
                        pltpu.VMEM((ctx_len, LANES), BF16),
                        pltpu.VMEM((seq, gw), F32),
                        pltpu.VMEM((D_STATE, gw), F32),
                        pltpu.VMEM((D_STATE, gw), F32),
                        pltpu.VMEM((n_chunks, 3 * nr, SSD_CHUNK), F32),
                        pltpu.VMEM((n_chunks, D_STATE, SSD_CHUNK), F32),
                        pltpu.VMEM((seq, gw), F32)],
        compiler_params=_cparams("parallel", "parallel"),
        name="ssd_bidir",
    )(proj, proj, proj, proj, dt, projc, projc, dtc,
      conv_w, conv_w, conv_w, conv_b, conv_b, conv_b, alog_g, dtb_g, dskip_c, ssd_norm)


def _outproj_kernel(a_ref, y_ref, w0_ref, w1_ref, x_ref, gate_ref, o_ref):
    acc = jnp.dot(a_ref[...], w0_ref[...], preferred_element_type=F32)
    acc = acc + jnp.dot(y_ref[...], w1_ref[...], preferred_element_type=F32)
    o_ref[...] = x_ref[...] + gate_ref[...] * acc


def _outproj(lhs0, cb0, lhs1, cb1, w, x2d, mod, *, rows_per_mod):
    m, d = x2d.shape
    kh = w.shape[0] // 2
    tm = _tile(rows_per_mod, ROW_TILE)
    tn = _tile(d, OUTPROJ_COL_TILE)
    tiles_per_mod = rows_per_mod // tm
    return pl.pallas_call(
        _outproj_kernel,
        grid=(m // tm, d // tn),
        in_specs=[pl.BlockSpec((tm, kh), lambda i, j: (i, cb0)),
                  pl.BlockSpec((tm, kh), lambda i, j: (i, cb1)),
                  pl.BlockSpec((kh, tn), lambda i, j: (0, j)),
                  pl.BlockSpec((kh, tn), lambda i, j: (1, j)),
                  pl.BlockSpec((tm, tn), lambda i, j: (i, j)),
                  pl.BlockSpec((None, 1, tn), lambda i, j: (3 * (i // tiles_per_mod) + 2, 0, j))],
        out_specs=pl.BlockSpec((tm, tn), lambda i, j: (i, j)),
        out_shape=jax.ShapeDtypeStruct((m, d), F32),
        compiler_params=_cparams("parallel", "arbitrary"),
        name="outproj_residual",
    )(lhs0, lhs1, w, w, x2d, mod)


FFT_RADIX = 8


def _cmul_root(z, k, n):
    re, im = z
    eighth = (8 * k) // n % 8
    assert (8 * k) % n == 0
    r = math.sqrt(0.5)
    if eighth == 0:
        return re, im
    if eighth == 1:
        return (re + im) * r, (im - re) * r
    if eighth == 2:
        return im, -re
    if eighth == 3:
        return (im - re) * r, -(re + im) * r
    nre, nim = _cmul_root(z, k - n // 2, n)
    return -nre, -nim


def _fft_small(v):
    n = len(v)
    if n == 1:
        return v
    ev, od = _fft_small(v[0::2]), _fft_small(v[1::2])
    out = [None] * n
    for k in range(n // 2):
        tr, ti = _cmul_root(od[k], k, n)
        out[k] = (ev[k][0] + tr, ev[k][1] + ti)
        out[k + n // 2] = (ev[k][0] - tr, ev[k][1] - ti)
    return out


def _fourier_kernel(u_ref, z_ref, w_ref, wc_ref, o_ref, uf_scr, y_scr, ab_scr, *, rows_per_step):
    seq, tn = u_ref.shape
    n2 = seq // FFT_RADIX
    rs = rows_per_step

    nslab = tn // LANES

    def widen(i, carry):
        r = pl.multiple_of(i * LANES, LANES)
        for c in range(nslab):
            uf_scr[c, pl.ds(r, LANES), :] = u_ref[pl.ds(r, LANES), c * LANES:(c + 1) * LANES].astype(F32)
        return carry

    lax.fori_loop(0, seq // LANES, widen, 0, unroll=4)
    for a in range(FFT_RADIX):
        ua = jnp.concatenate([uf_scr[c, pl.ds(a, n2, stride=FFT_RADIX), :] for c in range(nslab)], axis=1)
        y_scr[a] = jnp.dot(w_ref[a], ua.astype(BF16), preferred_element_type=F32)

    per_pass = ab_scr.shape[0] // (FFT_RADIX * rs)

    def combine_and_mix(g, carry):
        base = pl.multiple_of(g * (per_pass * rs), per_pass * rs)
        for ii in range(per_pass):
            r = base + ii * rs
            z = [(y_scr[a, pl.ds(r, rs), :], y_scr[a, pl.ds(n2 + r, rs), :]) for a in range(FFT_RADIX)]
            x = _fft_small(z)
            for k1 in range(FFT_RADIX):
                rows = pl.ds((ii * FFT_RADIX + k1) * rs, rs)
                ab_scr[rows, 0:tn] = x[k1][0].astype(BF16)
                ab_scr[rows, tn:2 * tn] = x[k1][1].astype(BF16)
        f = jnp.dot(ab_scr[...], wc_ref[...], preferred_element_type=F32)
        for ii in range(per_pass):
            r = base + ii * rs
            for k1 in range(FFT_RADIX):
                s0 = (ii * FFT_RADIX + k1) * rs
                gate = _silu(z_ref[pl.ds(k1 * n2 + r, rs), :].astype(F32))
                o_ref[pl.ds(k1 * n2 + r, rs), :] = (f[s0:s0 + rs] * gate).astype(o_ref.dtype)
        return carry

    lax.fori_loop(0, n2 // (per_pass * rs), combine_and_mix, 0)


def _fourier_gate(uz, *, batch, seq, width):
    n2 = seq // FFT_RADIX
    gw = width // N_FOURIER_GROUPS
    k2 = np.arange(n2, dtype=np.int64)[None, :, None]
    pos = FFT_RADIX * np.arange(n2, dtype=np.int64)[None, None, :] + np.arange(FFT_RADIX, dtype=np.int64)[:, None, None]
    ang = 2.0 * np.pi * ((k2 * pos) % seq).astype(np.float64) / seq
    w = jnp.asarray((np.concatenate([np.cos(ang), -np.sin(ang)], axis=1) / np.sqrt(seq)).astype(np.float32))
    kc = np.arange(gw, dtype=np.int64)
    angc = 2.0 * np.pi * ((kc[:, None] * kc[None, :]) % gw).astype(np.float64) / gw
    wc = jnp.asarray((np.concatenate([np.cos(angc), np.sin(angc)], axis=0) / np.sqrt(gw)).astype(np.float32))
    rs = 2 * SUBLANES
    per_pass = min(4, n2 // rs)
    return pl.pallas_call(
        functools.partial(_fourier_kernel, rows_per_step=rs),
        grid=(batch, N_FOURIER_GROUPS),
        in_specs=[pl.BlockSpec((seq, gw), lambda b, g: (b, g)),
                  pl.BlockSpec((seq, gw), lambda b, g: (b, N_FOURIER_GROUPS + g)),
                  pl.BlockSpec((FFT_RADIX, 2 * n2, n2), lambda b, g: (0, 0, 0)),
                  pl.BlockSpec((2 * gw, gw), lambda b, g: (0, 0))],
        out_specs=pl.BlockSpec((seq, gw), lambda b, g: (b, g)),
        out_shape=jax.ShapeDtypeStruct((batch * seq, width), BF16),
        scratch_shapes=[pltpu.VMEM((gw // LANES, seq, LANES), F32),
                        pltpu.VMEM((FFT_RADIX, 2 * n2, gw), F32),
                        pltpu.VMEM((per_pass * FFT_RADIX * rs, 2 * gw), BF16)],
        compiler_params=_cparams("parallel", "parallel"),
        name="fourier_mix_gate",
    )(uz, uz, w.astype(BF16), wc.astype(BF16))


def _rope_tables(seq):
    pos = jnp.arange(seq)
    row = (pos // GRID_W).astype(F32)
    col = (pos % GRID_W).astype(F32)
    inv_freq = ROPE_THETA ** (-jnp.arange(0, ROPE_AXIS_DIM, 2, dtype=F32) / ROPE_AXIS_DIM)
    ar, ac = row[:, None] * inv_freq, col[:, None] * inv_freq
    cos_t = jnp.concatenate([jnp.cos(ar), jnp.cos(ar), jnp.cos(ac), jnp.cos(ac)], axis=-1)
    sin_t = jnp.concatenate([-jnp.sin(ar), jnp.sin(ar), -jnp.sin(ac), jnp.sin(ac)], axis=-1)
    return cos_t, sin_t


def _layer(a, j):
    return a.reshape(a.shape[1:]) if a.shape[0] == 1 else a[j]


def _per_group_rows(v, hpg):
    g = v.shape[1] // hpg
    rows = -(-2 * hpg // SUBLANES) * SUBLANES
    t = v.reshape(2, g, hpg).transpose(1, 0, 2).reshape(g, 2 * hpg)
    t = jnp.pad(t, ((0, 0), (0, rows - 2 * hpg)))
    return jnp.broadcast_to(t[:, :, None], (g, rows, LANES))


def kernel(x, c, ctx, c_ctx, ev_norm_w, ev_ada_w, ev_ada_b, ev_w_in, ev_q_norm, ev_k_norm, ev_conv_w,
           ev_conv_b, ev_a_log, ev_dt_bias, ev_d_skip, ev_ssd_norm, ev_w_out, od_norm_w, od_ada_w,
           od_ada_b, od_w_in, od_w_out):
    batch, seq, d = x.shape
    ctx_len = ctx.shape[1]
    mix = ev_w_out.shape[1]
    attn_width = mix // 2
    ssd_width = mix - attn_width
    n_ssd_heads = ssd_width // SSD_HEAD_DIM
    hpg = n_ssd_heads // N_SSD_GROUPS
    n_kv = attn_width // HEAD_DIM // Q_PER_KV
    kv_width = n_kv * HEAD_DIM
    conv_ch = ssd_width + 2 * N_SSD_GROUPS * D_STATE
    ctx_col0 = 2 * attn_width + ssd_width
    n_main = ctx_col0 + 2 * kv_width + conv_ch
    m = batch * seq
    x2d = x.reshape(m, d)
    ctx2d = ctx.reshape(batch * ctx_len, d)
    n_mod_rows = 2 * SUBLANES
    assert batch + 1 <= n_mod_rows

    w_in = _layer(ev_w_in, 0)
    w_main = w_in.astype(BF16)
    qscale = LOG2E * HEAD_DIM ** -0.5
    q_gain = (_layer(ev_q_norm, 0) * qscale).reshape(1, HEAD_DIM)
    k_gain = _layer(ev_k_norm, 0).reshape(1, HEAD_DIM)
    cos_t, sin_t = _rope_tables(seq)
    dt_cols = w_main[:, n_main:]
    lane_pad = jnp.zeros((d, LANES - 2 * hpg), BF16)
    w_dt = jnp.concatenate(
        [blk for g in range(N_SSD_GROUPS)
         for blk in (dt_cols[:, g * hpg:(g + 1) * hpg],
                     dt_cols[:, n_ssd_heads + g * hpg:n_ssd_heads + (g + 1) * hpg], lane_pad)], axis=1)
    cond = jnp.zeros((n_mod_rows, d), F32).at[:batch].set(c).at[batch].set(c_ctx)
    mod0 = _ada(cond, _layer(ev_ada_w, 0), _layer(ev_ada_b, 0)).reshape(3 * n_mod_rows, 1, d)
    norm_w0 = _layer(ev_norm_w, 0)

    proj, dt = _inproj(x2d, mod0, norm_w0, w_main, w_dt,
                       rows_per_mod=seq, mod_row0=0, col0=0, ncols=n_main)
    projc, dtc = _inproj(ctx2d, mod0, norm_w0, w_main, w_dt,
                         rows_per_mod=batch * ctx_len, mod_row0=batch, col0=ctx_col0, ncols=n_main - ctx_col0)

    attn = _attention(proj, projc, q_gain, k_gain, cos_t, sin_t,
                      batch=batch, seq=seq, ctx_len=ctx_len, attn_width=attn_width)
    yssd = _ssd(proj, dt, projc, dtc, _layer(ev_conv_w, 0), _layer(ev_conv_b, 0).reshape(1, conv_ch),
                _per_group_rows(_layer(ev_a_log, 0), hpg), _per_group_rows(_layer(ev_dt_bias, 0), hpg),
                jnp.repeat(_layer(ev_d_skip, 0), SSD_HEAD_DIM).reshape(1, ssd_width),
                _layer(ev_ssd_norm, 0).reshape(1, ssd_width),
                batch=batch, seq=seq, ctx_len=ctx_len, attn_width=attn_width)
    x1 = _outproj(attn, 0, yssd, 0, _layer(ev_w_out, 0).astype(BF16), x2d, mod0, rows_per_mod=seq)

    fw = od_w_out.shape[1]
    cond1 = jnp.zeros((n_mod_rows, d), F32).at[:batch].set(c)
    mod1 = _ada(cond1, _layer(od_ada_w, 0), _layer(od_ada_b, 0)).reshape(3 * n_mod_rows, 1, d)
    uz, _ = _inproj(x1, mod1, _layer(od_norm_w, 0), _layer(od_w_in, 0).astype(BF16), None,
                    rows_per_mod=seq, mod_row0=0, col0=0, ncols=2 * fw)
    fg = _fourier_gate(uz, batch=batch, seq=seq, width=fw)
    x2 = _outproj(fg, 0, fg, 1, _layer(od_w_out, 0).astype(BF16), x1, mod1, rows_per_mod=seq)
    return x2.reshape(batch, seq, d)
```

```python
import functools
import math

import numpy as np
import jax
import jax.numpy as jnp
from jax import lax
from jax.experimental import pallas as pl
from jax.experimental.pallas import tpu as pltpu

F32 = jnp.float32
BF16 = jnp.bfloat16

GRID_W = 64
HEAD_DIM = 128
Q_PER_KV = 4
ROPE_THETA = 10000.0
ROPE_AXIS_DIM = HEAD_DIM // 2
SSD_HEAD_DIM = 64
N_SSD_GROUPS = 4
D_STATE = 128
D_CONV = 5
SSD_CHUNK = 128
N_FOURIER_GROUPS = 8
RMS_EPS = 1e-6

LANES = 128
SUBLANES = 8
VMEM_LIMIT_BYTES = 56 * 1024 * 1024

CONV_HALO = SSD_CHUNK // 2

ROW_TILE = 1024
INPROJ_COL_TILE = 2048
OUTPROJ_COL_TILE = 1024
ADA_COL_TILE = 1024
ATTN_Q_TILE = 512
ATTN_KEY_TILE = 4 * LANES
LOG2E = 1.4426950408889634


def _cparams(*sem):
    return pltpu.CompilerParams(dimension_semantics=sem, vmem_limit_bytes=VMEM_LIMIT_BYTES)


def _tile(n, pref):
    t = min(n, pref)
    while n % t:
        t //= 2
    return t


def _silu(v):
    h = 0.5 * v
    return h + h * jnp.tanh(h)


def _ada_kernel(c_ref, w_ref, b_ref, o_ref):
    c = c_ref[...]
    s = _silu(c).astype(BF16)
    o_ref[...] = jnp.dot(s, w_ref[...].astype(BF16), preferred_element_type=F32) + b_ref[...]


def _ada(cond, w, b):
    r, d = cond.shape
    n = w.shape[1]
    tn = _tile(n, ADA_COL_TILE)
    return pl.pallas_call(
        _ada_kernel,
        grid=(n // tn,),
        in_specs=[pl.BlockSpec((r, d), lambda j: (0, 0)),
                  pl.BlockSpec((d, tn), lambda j: (0, j)),
                  pl.BlockSpec((1, tn), lambda j: (0, j))],
        out_specs=pl.BlockSpec((r, tn), lambda j: (0, j)),
        out_shape=jax.ShapeDtypeStruct((r, n), F32),
        compiler_params=_cparams("parallel"),
        name="ada_mod",
    )(cond, w, b.reshape(1, n))


def _inproj_kernel(x_ref, sh_ref, sc_ref, nw_ref, w_ref, *rest, has_dt, sub):
    if has_dt:
        wdt_ref, o_ref, dt_ref, h_ref = rest
    else:
        o_ref, h_ref = rest
    tm = x_ref.shape[0]

    @pl.when(pl.program_id(1) == 0)
    def _():
        gain = nw_ref[...] * (1.0 + sc_ref[...])
        sh = sh_ref[...]

        def body(s, carry):
            r = pl.multiple_of(s * sub, sub)
            x = x_ref[pl.ds(r, sub), :]
            y = x * lax.rsqrt(jnp.mean(x * x, axis=-1, keepdims=True) + RMS_EPS)
            h_ref[pl.ds(r, sub), :] = (y * gain + sh).astype(BF16)
            return carry

        lax.fori_loop(0, tm // sub, body, 0, unroll=8)
        if has_dt:
            dt_ref[...] = jnp.dot(h_ref[...], wdt_ref[...], preferred_element_type=F32)

    o_ref[...] = jnp.dot(h_ref[...], w_ref[...], preferred_element_type=F32).astype(o_ref.dtype)


def _inproj(x2d, mod, norm_w, w, wdt, *, rows_per_mod, mod_row0, col0, ncols):
    m, d = x2d.shape
    tm = _tile(rows_per_mod, ROW_TILE)
    has_dt = wdt is not None
    tn = _tile(math.gcd(ncols, col0), INPROJ_COL_TILE)
    cb0 = col0 // tn
    tiles_per_mod = rows_per_mod // tm

    def modrow(i):
        return mod_row0 + i // tiles_per_mod

    in_specs = [pl.BlockSpec((tm, d), lambda i, j: (i, 0)),
                pl.BlockSpec((None, 1, d), lambda i, j: (3 * modrow(i), 0, 0)),
                pl.BlockSpec((None, 1, d), lambda i, j: (3 * modrow(i) + 1, 0, 0)),
                pl.BlockSpec((1, d), lambda i, j: (0, 0)),
                pl.BlockSpec((d, tn), lambda i, j: (0, cb0 + j))]
    args = [x2d, mod, mod, norm_w.reshape(1, d), w]
    out_specs = [pl.BlockSpec((tm, tn), lambda i, j: (i, j))]
    out_shape = [jax.ShapeDtypeStruct((m, ncols), BF16)]
    if has_dt:
        ndt = wdt.shape[1]
        in_specs.append(pl.BlockSpec((d, ndt), lambda i, j: (0, 0)))
        args.append(wdt)
        out_specs.append(pl.BlockSpec((tm, ndt), lambda i, j: (i, 0)))
        out_shape.append(jax.ShapeDtypeStruct((m, ndt), F32))
    res = pl.pallas_call(
        functools.partial(_inproj_kernel, has_dt=has_dt, sub=min(32, tm)),
        grid=(m // tm, ncols // tn),
        in_specs=in_specs,
        out_specs=out_specs,
        out_shape=out_shape,
        scratch_shapes=[pltpu.VMEM((tm, d), BF16)],
        compiler_params=_cparams("parallel", "arbitrary"),
        name="mod_inproj",
    )(*args)
    return (res[0], res[1]) if has_dt else (res[0], None)


def _norm_rope(v, gain, cos, sin):
    ones = jnp.ones((HEAD_DIM, HEAD_DIM), BF16)
    ss = jnp.dot((v * v).astype(BF16), ones, preferred_element_type=F32)
    y = v * lax.rsqrt(ss * (1.0 / HEAD_DIM) + RMS_EPS) * gain
    if cos is not None:
        quarter = ROPE_AXIS_DIM // 2
        lane = lax.broadcasted_iota(jnp.int32, (1, HEAD_DIM), 1)
        partner = jnp.where(lane % ROPE_AXIS_DIM < quarter,
                            pltpu.roll(y, HEAD_DIM - quarter, 1), pltpu.roll(y, quarter, 1))
        y = y * cos + partner * sin
    return y.astype(BF16)


def _attn_kernel(q_ref, g_ref, kl_ref, vl_ref, kc_ref, vc_ref, qg_ref, kg_ref, cq_ref, sq_ref, ck_ref, sk_ref,
                 o_ref, k_scr, s_scr, *, nrep, tk):
    tq = q_ref.shape[0]
    rows = nrep * tq
    blocks, col = [], 0
    for v_ref in (vc_ref, vl_ref):
        n = v_ref.shape[0]
        for r0 in range(0, n, tk):
            blocks.append((v_ref, r0, min(tk, n - r0), col))
            col += min(tk, n - r0)

    @pl.when(pl.program_id(2) == 0)
    def _prep_keys():
        kg = kg_ref[...]
        for v_ref, r0, size, c0 in blocks:
            if v_ref is vc_ref:
                k_scr[pl.ds(c0, size), :] = _norm_rope(kc_ref[pl.ds(r0, size), :].astype(F32), kg, None, None)
            else:
                k_scr[pl.ds(c0, size), :] = _norm_rope(kl_ref[pl.ds(r0, size), :].astype(F32), kg,
                                                       ck_ref[pl.ds(r0, size), :], sk_ref[pl.ds(r0, size), :])

    qg = qg_ref[...]
    qs = jnp.concatenate([_norm_rope(q_ref[:, r * HEAD_DIM:(r + 1) * HEAD_DIM].astype(F32), qg,
                                     cq_ref[...], sq_ref[...]) for r in range(nrep)], axis=0)

    n_parts = 2
    half = rows // n_parts
    halves = tuple(slice(i * half, (i + 1) * half) for i in range(n_parts))
    m = [None] * n_parts
    mb = [None] * n_parts
    acc = [jnp.zeros((half, HEAD_DIM + LANES), F32) for _ in halves]

    def pass1(i, blk):
        _, r0, size, c0 = blk
        s = lax.dot_general(qs[halves[i]], k_scr[pl.ds(c0, size), :], (((1,), (1,)), ((), ())),
                            preferred_element_type=F32)
        s_scr[halves[i], c0:c0 + size] = s
        for c in range(size // LANES):
            sc = s[:, c * LANES:(c + 1) * LANES]
            m[i] = sc if m[i] is None else jnp.maximum(m[i], sc)

    def pass2(i, blk):
        v_ref, r0, size, c0 = blk
        v_aug = jnp.concatenate([v_ref[pl.ds(r0, size), :], jnp.ones((size, LANES), BF16)], axis=1)
        p = jnp.concatenate([jnp.exp2(s_scr[halves[i], c0 + c * LANES:c0 + (c + 1) * LANES] - mb[i])
                             for c in range(size // LANES)], axis=1).astype(BF16)
        acc[i] = acc[i] + jnp.dot(p, v_aug, preferred_element_type=F32)

    def row_max(i):
        mb[i] = jnp.broadcast_to(jnp.max(m[i], axis=-1, keepdims=True), (half, LANES))

    for phase in range(n_parts + 1):
        for blk in blocks:
            if phase < n_parts:
                pass1(phase, blk)
            if phase > 0:
                pass2(phase - 1, blk)
        if phase < n_parts:
            row_max(phase)
    acc = jnp.concatenate(acc, axis=0)
    o = acc[:, :HEAD_DIM] / acc[:, HEAD_DIM:]
    for r in range(nrep):
        g = g_ref[:, r * HEAD_DIM:(r + 1) * HEAD_DIM].astype(F32)
        o_ref[:, r * HEAD_DIM:(r + 1) * HEAD_DIM] = (o[r * tq:(r + 1) * tq] * _silu(g)).astype(o_ref.dtype)


def _attention(proj, projc, q_gain, k_gain, cos_t, sin_t, *, batch, seq, ctx_len, attn_width):
    n_kv = attn_width // HEAD_DIM // Q_PER_KV
    gw = Q_PER_KV * HEAD_DIM
    tq = _tile(seq, ATTN_Q_TILE)
    tk = ATTN_KEY_TILE
    assert ctx_len % LANES == 0 and seq % LANES == 0
    tpb = seq // tq
    k_cb = (2 * attn_width + _ssd_width(attn_width)) // HEAD_DIM
    v_cb = k_cb + n_kv
    return pl.pallas_call(
        functools.partial(_attn_kernel, nrep=Q_PER_KV, tk=tk),
        grid=(batch, n_kv, tpb),
        in_specs=[pl.BlockSpec((tq, gw), lambda b, h, t: (b * tpb + t, h)),
                  pl.BlockSpec((tq, gw), lambda b, h, t: (b * tpb + t, attn_width // gw + h)),
                  pl.BlockSpec((seq, HEAD_DIM), lambda b, h, t: (b, k_cb + h)),
                  pl.BlockSpec((seq, HEAD_DIM), lambda b, h, t: (b, v_cb + h)),
                  pl.BlockSpec((ctx_len, HEAD_DIM), lambda b, h, t: (b, h)),
                  pl.BlockSpec((ctx_len, HEAD_DIM), lambda b, h, t: (b, n_kv + h)),
                  pl.BlockSpec((1, HEAD_DIM), lambda b, h, t: (0, 0)),
                  pl.BlockSpec((1, HEAD_DIM), lambda b, h, t: (0, 0)),
                  pl.BlockSpec((tq, HEAD_DIM), lambda b, h, t: (t, 0)),
                  pl.BlockSpec((tq, HEAD_DIM), lambda b, h, t: (t, 0)),
                  pl.BlockSpec((seq, HEAD_DIM), lambda b, h, t: (0, 0)),
                  pl.BlockSpec((seq, HEAD_DIM), lambda b, h, t: (0, 0))],
        out_specs=pl.BlockSpec((tq, gw), lambda b, h, t: (b * tpb + t, h)),
        out_shape=jax.ShapeDtypeStruct((batch * seq, attn_width), BF16),
        scratch_shapes=[pltpu.VMEM((ctx_len + seq, HEAD_DIM), BF16),
                        pltpu.VMEM((Q_PER_KV * tq, ctx_len + seq), F32)],
        compiler_params=_cparams("parallel", "parallel", "arbitrary"),
        name="gqa_attention",
    )(proj, proj, proj, proj, projc, projc, q_gain, k_gain, cos_t, sin_t, cos_t, sin_t)


def _ssd_width(attn_width):
    return attn_width


def _conv_silu(groups, pad_ref, n_rows):
    ch = SSD_CHUNK
    halo = CONV_HALO
    win_rows = ch + 2 * halo
    widths = [g[0].shape[1] for g in groups]
    offs = [sum(widths[:i]) for i in range(len(groups))]
    total = sum(widths)
    pad_ref[pl.ds(0, halo), 0:total] = jnp.zeros((halo, total), BF16)
    pad_ref[pl.ds(halo + n_rows, halo), 0:total] = jnp.zeros((halo, total), BF16)

    def copy(i, carry):
        r = pl.multiple_of(i * ch, ch)
        for (src_ref, _, _, _), o, cw in zip(groups, offs, widths):
            pad_ref[pl.ds(r + halo, ch), o:o + cw] = src_ref[pl.ds(r, ch), :]
        return carry

    lax.fori_loop(0, n_rows // ch, copy, 0)
    w = jnp.concatenate([g[1][...] for g in groups], axis=1)
    bias = jnp.concatenate([g[2][...] for g in groups], axis=1)
    out_row = lax.broadcasted_iota(jnp.int32, (ch, win_rows), 0)
    win_row = lax.broadcasted_iota(jnp.int32, (ch, win_rows), 1)
    taps = [k for k in range(D_CONV) if k != D_CONV // 2]
    selects = [jnp.where(win_row == out_row + (halo + k - D_CONV // 2), 1.0, 0.0).astype(BF16) for k in taps]

    blk = 2 * LANES

    def body(i, carry):
        r = pl.multiple_of(i * ch, ch)
        ys = []
        for c0 in range(0, total, blk):
            cs = slice(c0, min(c0 + blk, total))
            win = pad_ref[pl.ds(r, win_rows), cs]
            acc = bias[:, cs] + win[halo:halo + ch].astype(F32) * w[D_CONV // 2:D_CONV // 2 + 1, cs]
            for k, select in zip(taps, selects):
                acc = acc + jnp.dot(select, win, preferred_element_type=F32) * w[k:k + 1, cs]
            ys.append(_silu(acc))
        y = jnp.concatenate(ys, axis=1)
        for (_, _, _, dst_ref), o, cw in zip(groups, offs, widths):
            dst_ref[pl.ds(r, ch), :] = y[:, o:o + cw].astype(dst_ref.dtype)
        return carry

    lax.fori_loop(0, n_rows // ch, body, 0, unroll=2)


def _ssd_kernel(xl_ref, bl_ref, cl_ref, zl_ref, dtl_ref, xc_ref, bc_ref, dtc_ref,
                wx_ref, wb_ref, wc_ref, bx_ref, bb_ref, bcv_ref, alog_ref, dtb_ref, dsk_ref, nrm_ref,
                o_ref,
                pad_s, xs_s, b_s, c_s, xsc_s, bcc_s, y_s, stf, stb, terms_s, bt_s, eb_s, *, hpg):
    ch = SSD_CHUNK
    seq = xl_ref.shape[0]
    ctx_len = xc_ref.shape[0]
    gw = xl_ref.shape[1]
    npair = gw // LANES
    assert 2 * SSD_HEAD_DIM == LANES and hpg == 2 * npair

    _conv_silu([(xl_ref, wx_ref, bx_ref, xs_s), (bl_ref, wb_ref, bb_ref, b_s), (cl_ref, wc_ref, bcv_ref, c_s)],
               pad_s, seq)
    _conv_silu([(xc_ref, wx_ref, bx_ref, xsc_s), (bc_ref, wb_ref, bb_ref, bcc_s)], pad_s, ctx_len)

    li = lax.broadcasted_iota(jnp.int32, (ch, ch), 0)
    si = lax.broadcasted_iota(jnp.int32, (ch, ch), 1)
    tri_t = (li <= si).astype(F32)
    nr = alog_ref.shape[0]
    below = li > si
    upper = si >= li
    diag = li == si
    xr = lax.broadcasted_iota(jnp.int32, (ch, gw), 0)
    xh = lax.broadcasted_iota(jnp.int32, (ch, gw), 1) // SSD_HEAD_DIM
    spread_f = jnp.where(xr == 2 * nr + xh, 1.0, 0.0).astype(BF16)
    spread_b = jnp.where(xr == 2 * nr + hpg + xh, 1.0, 0.0).astype(BF16)
    row_id = lax.broadcasted_iota(jnp.int32, (nr, ch), 0)
    lane = lax.broadcasted_iota(jnp.int32, (1, LANES), 1)
    head0 = lane < SSD_HEAD_DIM
    neg_a_t = -jnp.exp(alog_ref[...]) * LOG2E
    dt_bias_t = dtb_ref[...]

    def chunk_terms(dt_ref, r):
        x_t = dt_ref[pl.ds(r, ch), :].T[:nr] + dt_bias_t
        dtp_t = jnp.maximum(x_t, 0.0) + jnp.log1p(jnp.exp(-jnp.abs(x_t)))
        da_t = dtp_t * neg_a_t
        p_t = jnp.dot(da_t, tri_t, preferred_element_type=F32, precision=lax.Precision.HIGHEST)
        return dtp_t, p_t, p_t - da_t

    def columns(p_t, e_t):
        tot = p_t[:, ch - 1:ch]
        fac_t = jnp.exp2(jnp.where(row_id < hpg, p_t, tot - e_t))
        pad = jnp.zeros((ch - 3 * nr, ch), F32)
        return jnp.concatenate([p_t, e_t, fac_t, pad], axis=0).T

    def pair_sel(a, b):
        return jnp.where(head0, a, b)

    def block_diag_rhs(x_pair):
        zero = jnp.zeros_like(x_pair)
        return jnp.concatenate([jnp.where(head0, x_pair, zero), jnp.where(head0, zero, x_pair)], axis=0)

    def state_update(st_ref, bt, x_ref, r, w_rows, tot):
        for q in range(npair):
            j0, j1 = 2 * q, 2 * q + 1
            lhs = jnp.concatenate([(bt * w_rows[j0]).astype(BF16), (bt * w_rows[j1]).astype(BF16)], axis=1)
            rhs = block_diag_rhs(x_ref[pl.ds(r, ch), q * LANES:(q + 1) * LANES])
            new = jnp.dot(lhs, rhs, preferred_element_type=F32)
            dec = pair_sel(jnp.exp2(tot[j0]), jnp.exp2(tot[j1]))
            cols = slice(q * LANES, (q + 1) * LANES)
            st_ref[:, cols] = st_ref[:, cols] * dec + new

    def fwd_state_terms(dtp_t, p_t):
        w_rows, tot = [], []
        for j in range(hpg):
            t = p_t[j:j + 1, ch - 1:ch]
            tot.append(t)
            w_rows.append(jnp.exp2(t - p_t[j:j + 1, :]) * dtp_t[j:j + 1, :])
        return w_rows, tot

    def bwd_state_terms(dtp_t, p_t, e_t):
        w_rows, tot = [], []
        for j in range(hpg, 2 * hpg):
            tot.append(p_t[j:j + 1, ch - 1:ch])
            w_rows.append(jnp.exp2(e_t[j:j + 1, :]) * dtp_t[j:j + 1, :])
        return w_rows, tot

    stf[...] = jnp.zeros(stf.shape, F32)
    stb[...] = jnp.zeros(stb.shape, F32)
    n_cc = ctx_len // ch

    def load_terms(c):
        terms = terms_s[c]
        return terms[:nr], terms[nr:2 * nr], terms[2 * nr:]

    def ctx_fwd(c, carry):
        r = pl.multiple_of(c * ch, ch)
        dtp_t, p_t, e_t = chunk_terms(dtc_ref, r)
        terms_s[c] = jnp.concatenate([dtp_t, p_t, e_t], axis=0)
        bt = bcc_s[pl.ds(r, ch), :].astype(F32).T
        bt_s[c] = bt
        w_rows, tot = fwd_state_terms(dtp_t, p_t)
        state_update(stf, bt, xsc_s, r, w_rows, tot)
        return carry

    def ctx_bwd(i, carry):
        c = n_cc - 1 - i
        r = pl.multiple_of(c * ch, ch)
        dtp_t, p_t, e_t = load_terms(c)
        w_rows, tot = bwd_state_terms(dtp_t, p_t, e_t)
        state_update(stb, bt_s[c], xsc_s, r, w_rows, tot)
        return carry

    lax.fori_loop(0, n_cc, ctx_fwd, 0, unroll=2)
    lax.fori_loop(0, n_cc, ctx_bwd, 0, unroll=2)

    n_lc = seq // ch

    def lat_fwd(c, carry):
        r = pl.multiple_of(c * ch, ch)
        dtp_t, p_t, e_t = chunk_terms(dtl_ref, r)
        terms_s[c] = jnp.concatenate([dtp_t, p_t, e_t], axis=0)
        pe = columns(p_t, e_t)
        bm = b_s[pl.ds(r, ch), :]
        cm = c_s[pl.ds(r, ch), :]
        bt = bm.astype(F32).T
        bt_s[c] = bt
        cb = lax.dot_general(cm, bm, (((1,), (1,)), ((), ())), preferred_element_type=F32)
        y_off = jnp.dot(cm, stf[...].astype(BF16), preferred_element_type=F32)
        pe16 = pe.astype(BF16)
        y_off = y_off * jnp.dot(pe16, spread_f, preferred_element_type=F32)
        eb_s[pl.ds(r, ch), :] = jnp.dot(pe16, spread_b, preferred_element_type=F32)
        lg_t = jnp.log2(dtp_t)
        row_f = p_t - lg_t
        row_b = e_t + lg_t
        lg_sum = jnp.log2(dtp_t[:hpg] + dtp_t[hpg:2 * hpg])
        for q in range(npair):
            ms = []
            for j in (2 * q, 2 * q + 1):
                jb = hpg + j
                col = jnp.where(upper, jnp.broadcast_to(p_t[j:j + 1, :], (ch, ch)),
                                jnp.broadcast_to(e_t[jb:jb + 1, :], (ch, ch))).T
                t = jnp.where(below, col - row_f[j:j + 1, :],
                              jnp.where(diag, lg_sum[j:j + 1, :], row_b[jb:jb + 1, :] - col))
                ms.append((cb * jnp.exp2(t)).astype(BF16))
            cols = slice(q * LANES, (q + 1) * LANES)
            rhs = block_diag_rhs(xs_s[pl.ds(r, ch), cols])
            y = jnp.dot(jnp.concatenate(ms, axis=1), rhs, preferred_element_type=F32)
            y_s[pl.ds(r, ch), cols] = y + y_off[:, cols]
        w_rows, tot = fwd_state_terms(dtp_t, p_t)
        state_update(stf, bt, xs_s, r, w_rows, tot)
        return carry

    lax.fori_loop(0, n_lc, lat_fwd, 0, unroll=16)

    dskip = dsk_ref[...]
    nrm = nrm_ref[...]

    def lat_bwd(i, carry):
        c = n_lc - 1 - i
        r = pl.multiple_of(c * ch, ch)
        dtp_t, p_t, e_t = load_terms(c)
        y_off = jnp.dot(c_s[pl.ds(r, ch), :], stb[...].astype(BF16), preferred_element_type=F32)
        y = y_s[pl.ds(r, ch), :] + y_off * eb_s[pl.ds(r, ch), :]
        w_rows, tot = bwd_state_terms(dtp_t, p_t, e_t)
        state_update(stb, bt_s[c], xs_s, r, w_rows, tot)
        y = (y + xs_s[pl.ds(r, ch), :].astype(F32) * dskip) * _silu(zl_ref[pl.ds(r, ch), :].astype(F32))
        y = y * lax.rsqrt(jnp.mean(y * y, axis=-1, keepdims=True) + RMS_EPS) * nrm
        o_ref[pl.ds(r, ch), :] = y.astype(o_ref.dtype)
        return carry

    lax.fori_loop(0, n_lc, lat_bwd, 0, unroll=8)


def _ssd(proj, dt, projc, dtc, conv_w, conv_b, alog_g, dtb_g, dskip_c, ssd_norm, *,
         batch, seq, ctx_len, attn_width):
    ssd_width = _ssd_width(attn_width)
    n_kv = attn_width // HEAD_DIM // Q_PER_KV
    kv_width = n_kv * HEAD_DIM
    gw = ssd_width // N_SSD_GROUPS
    hpg = gw // SSD_HEAD_DIM
    xs_off = 3 * attn_width + 2 * kv_width
    xs_off_c = 2 * kv_width
    assert xs_off % gw == 0 and xs_off_c % gw == 0 and D_STATE == LANES
    nb = N_SSD_GROUPS
    b_cb = (xs_off + ssd_width) // LANES
    c_cb = b_cb + nb
    b_cb_c = (xs_off_c + ssd_width) // LANES
    cw_b = ssd_width // LANES
    nr = alog_g.shape[1]
    in_specs = [
        pl.BlockSpec((seq, gw), lambda b, g: (b, xs_off // gw + g)),
        pl.BlockSpec((seq, LANES), lambda b, g: (b, b_cb + g)),
        pl.BlockSpec((seq, LANES), lambda b, g: (b, c_cb + g)),
        pl.BlockSpec((seq, gw), lambda b, g: (b, 2 * attn_width // gw + g)),
        pl.BlockSpec((seq, LANES), lambda b, g: (b, g)),
        pl.BlockSpec((ctx_len, gw), lambda b, g: (b, xs_off_c // gw + g)),
        pl.BlockSpec((ctx_len, LANES), lambda b, g: (b, b_cb_c + g)),
        pl.BlockSpec((ctx_len, LANES), lambda b, g: (b, g)),
        pl.BlockSpec((D_CONV, gw), lambda b, g: (0, g)),
        pl.BlockSpec((D_CONV, LANES), lambda b, g: (0, cw_b + g)),
        pl.BlockSpec((D_CONV, LANES), lambda b, g: (0, cw_b + nb + g)),
        pl.BlockSpec((1, gw), lambda b, g: (0, g)),
        pl.BlockSpec((1, LANES), lambda b, g: (0, cw_b + g)),
        pl.BlockSpec((1, LANES), lambda b, g: (0, cw_b + nb + g)),
        pl.BlockSpec((None, nr, LANES), lambda b, g: (g, 0, 0)),
        pl.BlockSpec((None, nr, LANES), lambda b, g: (g, 0, 0)),
        pl.BlockSpec((1, gw), lambda b, g: (0, g)),
        pl.BlockSpec((1, gw), lambda b, g: (0, g)),
    ]
    pad_rows = max(seq, ctx_len) + 2 * CONV_HALO
    n_chunks = max(seq, ctx_len) // SSD_CHUNK
    return pl.pallas_call(
        functools.partial(_ssd_kernel, hpg=hpg),
        grid=(batch, N_SSD_GROUPS),
        in_specs=in_specs,
        out_specs=pl.BlockSpec((seq, gw), lambda b, g: (b, g)),
        out_shape=jax.ShapeDtypeStruct((batch * seq, ssd_width), BF16),
        scratch_shapes=[pltpu.VMEM((pad_rows, gw + 2 * D_STATE), BF16),
                        pltpu.VMEM((seq, gw), BF16),
                        pltpu.VMEM((seq, LANES), BF16),
                        pltpu.VMEM((seq, LANES), BF16),
                        pltpu.VMEM((ctx_len, gw), BF16),
                        pltpu.VMEM((ctx_len, LANES), BF16),
                        pltpu.VMEM((seq, gw), F32),
                        pltpu.VMEM((D_STATE, gw), F32),
                        pltpu.VMEM((D_STATE, gw), F32),
                        pltpu.VMEM((n_chunks, 3 * nr, SSD_CHUNK), F32),
                        pltpu.VMEM((n_chunks, D_STATE, SSD_CHUNK), F32),
                        pltpu.VMEM((seq, gw), F32)],
        compiler_params=_cparams("parallel", "parallel"),
        name="ssd_bidir",
    )(proj, proj, proj, proj, dt, projc, projc, dtc,
      conv_w, conv_w, conv_w, conv_b, conv_b, conv_b, alog_g, dtb_g, dskip_c, ssd_norm)


def _outproj_kernel(a_ref, y_ref, w0_ref, w1_ref, x_ref, gate_ref, o_ref):
    acc = jnp.dot(a_ref[...], w0_ref[...], preferred_element_type=F32)
    acc = acc + jnp.dot(y_ref[...], w1_ref[...], preferred_element_type=F32)
    o_ref[...] = x_ref[...] + gate_ref[...] * acc


def _outproj(lhs0, cb0, lhs1, cb1, w, x2d, mod, *, rows_per_mod):
    m, d = x2d.shape
    kh = w.shape[0] // 2
    tm = _tile(rows_per_mod, ROW_TILE)
    tn = _tile(d, OUTPROJ_COL_TILE)
    tiles_per_mod = rows_per_mod // tm
    return pl.pallas_call(
        _outproj_kernel,
        grid=(m // tm, d // tn),
        in_specs=[pl.BlockSpec((tm, kh), lambda i, j: (i, cb0)),
                  pl.BlockSpec((tm, kh), lambda i, j: (i, cb1)),
                  pl.BlockSpec((kh, tn), lambda i, j: (0, j)),
                  pl.BlockSpec((kh, tn), lambda i, j: (1, j)),
                  pl.BlockSpec((tm, tn), lambda i, j: (i, j)),
                  pl.BlockSpec((None, 1, tn), lambda i, j: (3 * (i // tiles_per_mod) + 2, 0, j))],
        out_specs=pl.BlockSpec((tm, tn), lambda i, j: (i, j)),
        out_shape=jax.ShapeDtypeStruct((m, d), F32),
        compiler_params=_cparams("parallel", "arbitrary"),
        name="outproj_residual",
    )(lhs0, lhs1, w, w, x2d, mod)


FFT_RADIX = 8


def _cmul_root(z, k, n):
    re, im = z
    eighth = (8 * k) // n % 8
    assert (8 * k) % n == 0
    r = math.sqrt(0.5)
    if eighth == 0:
        return re, im
    if eighth == 1:
        return (re + im) * r, (im - re) * r
    if eighth == 2:
        return im, -re
    if eighth == 3:
        return (im - re) * r, -(re + im) * r
    nre, nim = _cmul_root(z, k - n // 2, n)
    return -nre, -nim


def _fft_small(v):
    n = len(v)
    if n == 1:
        return v
    ev, od = _fft_small(v[0::2]), _fft_small(v[1::2])
    out = [None] * n
    for k in range(n // 2):
        tr, ti = _cmul_root(od[k], k, n)
        out[k] = (ev[k][0] + tr, ev[k][1] + ti)
        out[k + n // 2] = (ev[k][0] - tr, ev[k][1] - ti)
    return out


def _fourier_kernel(u_ref, z_ref, w_ref, wc_ref, o_ref, uf_scr, y_scr, ab_scr, *, rows_per_step):
    seq, tn = u_ref.shape
    n2 = seq // FFT_RADIX
    rs = rows_per_step

    nslab = tn // LANES

    def widen(i, carry):
        r = pl.multiple_of(i * LANES, LANES)
        for c in range(nslab):
            uf_scr[c, pl.ds(r, LANES), :] = u_ref[pl.ds(r, LANES), c * LANES:(c + 1) * LANES].astype(F32)
        return carry

    lax.fori_loop(0, seq // LANES, widen, 0, unroll=4)
    for a in range(FFT_RADIX):
        ua = jnp.concatenate([uf_scr[c, pl.ds(a, n2, stride=FFT_RADIX), :] for c in range(nslab)], axis=1)
        y_scr[a] = jnp.dot(w_ref[a], ua.astype(BF16), preferred_element_type=F32)

    per_pass = ab_scr.shape[0] // (FFT_RADIX * rs)

    def combine_and_mix(g, carry):
        base = pl.multiple_of(g * (per_pass * rs), per_pass * rs)
        for ii in range(per_pass):
            r = base + ii * rs
            z = [(y_scr[a, pl.ds(r, rs), :], y_scr[a, pl.ds(n2 + r, rs), :]) for a in range(FFT_RADIX)]
            x = _fft_small(z)
            for k1 in range(FFT_RADIX):
                rows = pl.ds((ii * FFT_RADIX + k1) * rs, rs)
                ab_scr[rows, 0:tn] = x[k1][0].astype(BF16)
                ab_scr[rows, tn:2 * tn] = x[k1][1].astype(BF16)
        f = jnp.dot(ab_scr[...], wc_ref[...], preferred_element_type=F32)
        for ii in range(per_pass):
            r = base + ii * rs
            for k1 in range(FFT_RADIX):
                s0 = (ii * FFT_RADIX + k1) * rs
                gate = _silu(z_ref[pl.ds(k1 * n2 + r, rs), :].astype(F32))
                o_ref[pl.ds(k1 * n2 + r, rs), :] = (f[s0:s0 + rs] * gate).astype(o_ref.dtype)
        return carry

    lax.fori_loop(0, n2 // (per_pass * rs), combine_and_mix, 0)


def _fourier_gate(uz, *, batch, seq, width):
    n2 = seq // FFT_RADIX
    gw = width // N_FOURIER_GROUPS
    k2 = np.arange(n2, dtype=np.int64)[None, :, None]
    pos = FFT_RADIX * np.arange(n2, dtype=np.int64)[None, None, :] + np.arange(FFT_RADIX, dtype=np.int64)[:, None, None]
    ang = 2.0 * np.pi * ((k2 * pos) % seq).astype(np.float64) / seq
    w = jnp.asarray((np.concatenate([np.cos(ang), -np.sin(ang)], axis=1) / np.sqrt(seq)).astype(np.float32))
    kc = np.arange(gw, dtype=np.int64)
    angc = 2.0 * np.pi * ((kc[:, None] * kc[None, :]) % gw).astype(np.float64) / gw
    wc = jnp.asarray((np.concatenate([np.cos(angc), np.sin(angc)], axis=0) / np.sqrt(gw)).astype(np.float32))
    rs = 2 * SUBLANES
    per_pass = min(4, n2 // rs)
    return pl.pallas_call(
        functools.partial(_fourier_kernel, rows_per_step=rs),
        grid=(batch, N_FOURIER_GROUPS),
        in_specs=[pl.BlockSpec((seq, gw), lambda b, g: (b, g)),
                  pl.BlockSpec((seq, gw), lambda b, g: (b, N_FOURIER_GROUPS + g)),
                  pl.BlockSpec((FFT_RADIX, 2 * n2, n2), lambda b, g: (0, 0, 0)),
                  pl.BlockSpec((2 * gw, gw), lambda b, g: (0, 0))],
        out_specs=pl.BlockSpec((seq, gw), lambda b, g: (b, g)),
        out_shape=jax.ShapeDtypeStruct((batch * seq, width), BF16),
        scratch_shapes=[pltpu.VMEM((gw // LANES, seq, LANES), F32),
                        pltpu.VMEM((FFT_RADIX, 2 * n2, gw), F32),
                        pltpu.VMEM((per_pass * FFT_RADIX * rs, 2 * gw), BF16)],
        compiler_params=_cparams("parallel", "parallel"),
        name="fourier_mix_gate",
    )(uz, uz, w.astype(BF16), wc.astype(BF16))


def _rope_tables(seq):
    pos = jnp.arange(seq)
    row = (pos // GRID_W).astype(F32)
    col = (pos % GRID_W).astype(F32)
    inv_freq = ROPE_THETA ** (-jnp.arange(0, ROPE_AXIS_DIM, 2, dtype=F32) / ROPE_AXIS_DIM)
    ar, ac = row[:, None] * inv_freq, col[:, None] * inv_freq
    cos_t = jnp.concatenate([jnp.cos(ar), jnp.cos(ar), jnp.cos(ac), jnp.cos(ac)], axis=-1)
    sin_t = jnp.concatenate([-jnp.sin(ar), jnp.sin(ar), -jnp.sin(ac), jnp.sin(ac)], axis=-1)
    return cos_t, sin_t


def _layer(a, j):
    return a.reshape(a.shape[1:]) if a.shape[0] == 1 else a[j]


def _per_group_rows(v, hpg):
    g = v.shape[1] // hpg
    rows = -(-2 * hpg // SUBLANES) * SUBLANES
    t = v.reshape(2, g, hpg).transpose(1, 0, 2).reshape(g, 2 * hpg)
    t = jnp.pad(t, ((0, 0), (0, rows - 2 * hpg)))
    return jnp.broadcast_to(t[:, :, None], (g, rows, LANES))


def kernel(x, c, ctx, c_ctx, ev_norm_w, ev_ada_w, ev_ada_b, ev_w_in, ev_q_norm, ev_k_norm, ev_conv_w,
           ev_conv_b, ev_a_log, ev_dt_bias, ev_d_skip, ev_ssd_norm, ev_w_out, od_norm_w, od_ada_w,
           od_ada_b, od_w_in, od_w_out):
    batch, seq, d = x.shape
    ctx_len = ctx.shape[1]
    mix = ev_w_out.shape[1]
    attn_width = mix // 2
    ssd_width = mix - attn_width
    n_ssd_heads = ssd_width // SSD_HEAD_DIM
    hpg = n_ssd_heads // N_SSD_GROUPS
    n_kv = attn_width // HEAD_DIM // Q_PER_KV
    kv_width = n_kv * HEAD_DIM
    conv_ch = ssd_width + 2 * N_SSD_GROUPS * D_STATE
    ctx_col0 = 2 * attn_width + ssd_width
    n_main = ctx_col0 + 2 * kv_width + conv_ch
    m = batch * seq
    x2d = x.reshape(m, d)
    ctx2d = ctx.reshape(batch * ctx_len, d)
    n_mod_rows = 2 * SUBLANES
    assert batch + 1 <= n_mod_rows

    w_in = _layer(ev_w_in, 0)
    w_main = w_in.astype(BF16)
    qscale = LOG2E * HEAD_DIM ** -0.5
    q_gain = (_layer(ev_q_norm, 0) * qscale).reshape(1, HEAD_DIM)
    k_gain = _layer(ev_k_norm, 0).reshape(1, HEAD_DIM)
    cos_t, sin_t = _rope_tables(seq)
    dt_cols = w_main[:, n_main:]
    lane_pad = jnp.zeros((d, LANES - 2 * hpg), BF16)
    w_dt = jnp.concatenate(
        [blk for g in range(N_SSD_GROUPS)
         for blk in (dt_cols[:, g * hpg:(g + 1) * hpg],
                     dt_cols[:, n_ssd_heads + g * hpg:n_ssd_heads + (g + 1) * hpg], lane_pad)], axis=1)
    cond = jnp.zeros((n_mod_rows, d), F32).at[:batch].set(c).at[batch].set(c_ctx)
    mod0 = _ada(cond, _layer(ev_ada_w, 0), _layer(ev_ada_b, 0)).reshape(3 * n_mod_rows, 1, d)
    norm_w0 = _layer(ev_norm_w, 0)

    proj, dt = _inproj(x2d, mod0, norm_w0, w_main, w_dt,
                       rows_per_mod=seq, mod_row0=0, col0=0, ncols=n_main)
    projc, dtc = _inproj(ctx2d, mod0, norm_w0, w_main, w_dt,
                         rows_per_mod=batch * ctx_len, mod_row0=batch, col0=ctx_col0, ncols=n_main - ctx_col0)

    attn = _attention(proj, projc, q_gain, k_gain, cos_t, sin_t,
                      batch=batch, seq=seq, ctx_len=ctx_len, attn_width=attn_width)
    yssd = _ssd(proj, dt, projc, dtc, _layer(ev_conv_w, 0), _layer(ev_conv_b, 0).reshape(1, conv_ch),
                _per_group_rows(_layer(ev_a_log, 0), hpg), _per_group_rows(_layer(ev_dt_bias, 0), hpg),
                jnp.repeat(_layer(ev_d_skip, 0), SSD_HEAD_DIM).reshape(1, ssd_width),
                _layer(ev_ssd_norm, 0).reshape(1, ssd_width),
                batch=batch, seq=seq, ctx_len=ctx_len, attn_width=attn_width)
    x1 = _outproj(attn, 0, yssd, 0, _layer(ev_w_out, 0).astype(BF16), x2d, mod0, rows_per_mod=seq)

    fw = od_w_out.shape[1]
    cond1 = jnp.zeros((n_mod_rows, d), F32).at[:batch].set(c)
    mod1 = _ada(cond1, _layer(od_ada_w, 0), _layer(od_ada_b, 0)).reshape(3 * n_mod_rows, 1, d)
    uz, _ = _inproj(x1, mod1, _layer(od_norm_w, 0), _layer(od_w_in, 0).astype(BF16), None,
                    rows_per_mod=seq, mod_row0=0, col0=0, ncols=2 * fw)
    fg = _fourier_gate(uz, batch=batch, seq=seq, width=fw)
    x2 = _outproj(fg, 0, fg, 1, _layer(od_w_out, 0).astype(BF16), x1, mod1, rows_per_mod=seq)
    return x2.reshape(batch, seq, d)
```

```python
import functools
import math

import numpy as np
import jax
import jax.numpy as jnp
from jax import lax
from jax.experimental import pallas as pl
from jax.experimental.pallas import tpu as pltpu

F32 = jnp.float32
BF16 = jnp.bfloat16

GRID_W = 64
HEAD_DIM = 128
Q_PER_KV = 4
ROPE_THETA = 10000.0
ROPE_AXIS_DIM = HEAD_DIM // 2
SSD_HEAD_DIM = 64
N_SSD_GROUPS = 4
D_STATE = 128
D_CONV = 5
SSD_CHUNK = 128
N_FOURIER_GROUPS = 8
RMS_EPS = 1e-6

LANES = 128
SUBLANES = 8
VMEM_LIMIT_BYTES = 56 * 1024 * 1024

CONV_HALO = SSD_CHUNK // 2

ROW_TILE = 1024
INPROJ_COL_TILE = 2048
OUTPROJ_COL_TILE = 1024
ADA_COL_TILE = 1024
ATTN_Q_TILE = 512
ATTN_KEY_TILE = 4 * LANES
LOG2E = 1.4426950408889634


def _cparams(*sem):
    return pltpu.CompilerParams(dimension_semantics=sem, vmem_limit_bytes=VMEM_LIMIT_BYTES)


def _tile(n, pref):
    t = min(n, pref)
    while n % t:
        t //= 2
    return t


def _silu(v):
    h = 0.5 * v
    return h + h * jnp.tanh(h)


def _ada_kernel(c_ref, w_ref, b_ref, o_ref):
    c = c_ref[...]
    s = _silu(c).astype(BF16)
    o_ref[...] = jnp.dot(s, w_ref[...].astype(BF16), preferred_element_type=F32) + b_ref[...]


def _ada(cond, w, b):
    r, d = cond.shape
    n = w.shape[1]
    tn = _tile(n, ADA_COL_TILE)
    return pl.pallas_call(
        _ada_kernel,
        grid=(n // tn,),
        in_specs=[pl.BlockSpec((r, d), lambda j: (0, 0)),
                  pl.BlockSpec((d, tn), lambda j: (0, j)),
                  pl.BlockSpec((1, tn), lambda j: (0, j))],
        out_specs=pl.BlockSpec((r, tn), lambda j: (0, j)),
        out_shape=jax.ShapeDtypeStruct((r, n), F32),
        compiler_params=_cparams("parallel"),
        name="ada_mod",
    )(cond, w, b.reshape(1, n))


def _inproj_kernel(x_ref, sh_ref, sc_ref, nw_ref, w_ref, *rest, has_dt, sub):
    if has_dt:
        wdt_ref, o_ref, dt_ref, h_ref = rest
    else:
        o_ref, h_ref = rest
    tm = x_ref.shape[0]

    @pl.when(pl.program_id(1) == 0)
    def _():
        gain = nw_ref[...] * (1.0 + sc_ref[...])
        sh = sh_ref[...]

        def body(s, carry):
            r = pl.multiple_of(s * sub, sub)
            x = x_ref[pl.ds(r, sub), :]
            y = x * lax.rsqrt(jnp.mean(x * x, axis=-1, keepdims=True) + RMS_EPS)
            h_ref[pl.ds(r, sub), :] = (y * gain + sh).astype(BF16)
            return carry

        lax.fori_loop(0, tm // sub, body, 0, unroll=8)
        if has_dt:
            dt_ref[...] = jnp.dot(h_ref[...], wdt_ref[...], preferred_element_type=F32)

    o_ref[...] = jnp.dot(h_ref[...], w_ref[...], preferred_element_type=F32).astype(o_ref.dtype)


def _inproj(x2d, mod, norm_w, w, wdt, *, rows_per_mod, mod_row0, col0, ncols):
    m, d = x2d.shape
    tm = _tile(rows_per_mod, ROW_TILE)
    has_dt = wdt is not None
    tn = _tile(math.gcd(ncols, col0), INPROJ_COL_TILE)
    cb0 = col0 // tn
    tiles_per_mod = rows_per_mod // tm

    def modrow(i):
        return mod_row0 + i // tiles_per_mod

    in_specs = [pl.BlockSpec((tm, d), lambda i, j: (i, 0)),
                pl.BlockSpec((None, 1, d), lambda i, j: (3 * modrow(i), 0, 0)),
                pl.BlockSpec((None, 1, d), lambda i, j: (3 * modrow(i) + 1, 0, 0)),
                pl.BlockSpec((1, d), lambda i, j: (0, 0)),
                pl.BlockSpec((d, tn), lambda i, j: (0, cb0 + j))]
    args = [x2d, mod, mod, norm_w.reshape(1, d), w]
    out_specs = [pl.BlockSpec((tm, tn), lambda i, j: (i, j))]
    out_shape = [jax.ShapeDtypeStruct((m, ncols), BF16)]
    if has_dt:
        ndt = wdt.shape[1]
        in_specs.append(pl.BlockSpec((d, ndt), lambda i, j: (0, 0)))
        args.append(wdt)
        out_specs.append(pl.BlockSpec((tm, ndt), lambda i, j: (i, 0)))
        out_shape.append(jax.ShapeDtypeStruct((m, ndt), F32))
    res = pl.pallas_call(
        functools.partial(_inproj_kernel, has_dt=has_dt, sub=min(32, tm)),
        grid=(m // tm, ncols // tn),
        in_specs=in_specs,
        out_specs=out_specs,
        out_shape=out_shape,
        scratch_shapes=[pltpu.VMEM((tm, d), BF16)],
        compiler_params=_cparams("parallel", "arbitrary"),
        name="mod_inproj",
    )(*args)
    return (res[0], res[1]) if has_dt else (res[0], None)


def _norm_rope(v, gain, cos, sin):
    ones = jnp.ones((HEAD_DIM, HEAD_DIM), BF16)
    ss = jnp.dot((v * v).astype(BF16), ones, preferred_element_type=F32)
    y = v * lax.rsqrt(ss * (1.0 / HEAD_DIM) + RMS_EPS) * gain
    if cos is not None:
        quarter = ROPE_AXIS_DIM // 2
        lane = lax.broadcasted_iota(jnp.int32, (1, HEAD_DIM), 1)
        partner = jnp.where(lane % ROPE_AXIS_DIM < quarter,
                            pltpu.roll(y, HEAD_DIM - quarter, 1), pltpu.roll(y, quarter, 1))
        y = y * cos + partner * sin
    return y.astype(BF16)


def _attn_kernel(q_ref, g_ref, kl_ref, vl_ref, kc_ref, vc_ref, qg_ref, kg_ref, cq_ref, sq_ref, ck_ref, sk_ref,
                 o_ref, k_scr, s_scr, *, nrep, tk):
    tq = q_ref.shape[0]
    rows = nrep * tq
    blocks, col = [], 0
    for v_ref in (vc_ref, vl_ref):
        n = v_ref.shape[0]
        for r0 in range(0, n, tk):
            blocks.append((v_ref, r0, min(tk, n - r0), col))
            col += min(tk, n - r0)

    @pl.when(pl.program_id(2) == 0)
    def _prep_keys():
        kg = kg_ref[...]
        for v_ref, r0, size, c0 in blocks:
            if v_ref is vc_ref:
                k_scr[pl.ds(c0, size), :] = _norm_rope(kc_ref[pl.ds(r0, size), :].astype(F32), kg, None, None)
            else:
                k_scr[pl.ds(c0, size), :] = _norm_rope(kl_ref[pl.ds(r0, size), :].astype(F32), kg,
                                                       ck_ref[pl.ds(r0, size), :], sk_ref[pl.ds(r0, size), :])

    qg = qg_ref[...]
    qs = jnp.concatenate([_norm_rope(q_ref[:, r * HEAD_DIM:(r + 1) * HEAD_DIM].astype(F32), qg,
                                     cq_ref[...], sq_ref[...]) for r in range(nrep)], axis=0)

    n_parts = 2
    half = rows // n_parts
    halves = tuple(slice(i * half, (i + 1) * half) for i in range(n_parts))
    m = [None] * n_parts
    mb = [None] * n_parts
    acc = [jnp.zeros((half, HEAD_DIM + LANES), F32) for _ in halves]

    def pass1(i, blk):
        _, r0, size, c0 = blk
        s = lax.dot_general(qs[halves[i]], k_scr[pl.ds(c0, size), :], (((1,), (1,)), ((), ())),
                            preferred_element_type=F32)
        s_scr[halves[i], c0:c0 + size] = s
        for c in range(size // LANES):
            sc = s[:, c * LANES:(c + 1) * LANES]
            m[i] = sc if m[i] is None else jnp.maximum(m[i], sc)

    def pass2(i, blk):
        v_ref, r0, size, c0 = blk
        v_aug = jnp.concatenate([v_ref[pl.ds(r0, size), :], jnp.ones((size, LANES), BF16)], axis=1)
        p = jnp.concatenate([jnp.exp2(s_scr[halves[i], c0 + c * LANES:c0 + (c + 1) * LANES] - mb[i])
                             for c in range(size // LANES)], axis=1).astype(BF16)
        acc[i] = acc[i] + jnp.dot(p, v_aug, preferred_element_type=F32)

    def row_max(i):
        mb[i] = jnp.broadcast_to(jnp.max(m[i], axis=-1, keepdims=True), (half, LANES))

    for phase in range(n_parts + 1):
        for blk in blocks:
            if phase < n_parts:
                pass1(phase, blk)
            if phase > 0:
                pass2(phase - 1, blk)
        if phase < n_parts:
            row_max(phase)
    acc = jnp.concatenate(acc, axis=0)
    o = acc[:, :HEAD_DIM] / acc[:, HEAD_DIM:]
    for r in range(nrep):
        g = g_ref[:, r * HEAD_DIM:(r + 1) * HEAD_DIM].astype(F32)
        o_ref[:, r * HEAD_DIM:(r + 1) * HEAD_DIM] = (o[r * tq:(r + 1) * tq] * _silu(g)).astype(o_ref.dtype)


def _attention(proj, projc, q_gain, k_gain, cos_t, sin_t, *, batch, seq, ctx_len, attn_width):
    n_kv = attn_width // HEAD_DIM // Q_PER_KV
    gw = Q_PER_KV * HEAD_DIM
    tq = _tile(seq, ATTN_Q_TILE)
    tk = ATTN_KEY_TILE
    assert ctx_len % LANES == 0 and seq % LANES == 0
    tpb = seq // tq
    k_cb = (2 * attn_width + _ssd_width(attn_width)) // HEAD_DIM
    v_cb = k_cb + n_kv
    return pl.pallas_call(
        functools.partial(_attn_kernel, nrep=Q_PER_KV, tk=tk),
        grid=(batch, n_kv, tpb),
        in_specs=[pl.BlockSpec((tq, gw), lambda b, h, t: (b * tpb + t, h)),
                  pl.BlockSpec((tq, gw), lambda b, h, t: (b * tpb + t, attn_width // gw + h)),
                  pl.BlockSpec((seq, HEAD_DIM), lambda b, h, t: (b, k_cb + h)),
                  pl.BlockSpec((seq, HEAD_DIM), lambda b, h, t: (b, v_cb + h)),
                  pl.BlockSpec((ctx_len, HEAD_DIM), lambda b, h, t: (b, h)),
                  pl.BlockSpec((ctx_len, HEAD_DIM), lambda b, h, t: (b, n_kv + h)),
                  pl.BlockSpec((1, HEAD_DIM), lambda b, h, t: (0, 0)),
                  pl.BlockSpec((1, HEAD_DIM), lambda b, h, t: (0, 0)),
                  pl.BlockSpec((tq, HEAD_DIM), lambda b, h, t: (t, 0)),
                  pl.BlockSpec((tq, HEAD_DIM), lambda b, h, t: (t, 0)),
                  pl.BlockSpec((seq, HEAD_DIM), lambda b, h, t: (0, 0)),
                  pl.BlockSpec((seq, HEAD_DIM), lambda b, h, t: (0, 0))],
        out_specs=pl.BlockSpec((tq, gw), lambda b, h, t: (b * tpb + t, h)),
        out_shape=jax.ShapeDtypeStruct((batch * seq, attn_width), BF16),
        scratch_shapes=[pltpu.VMEM((ctx_len + seq, HEAD_DIM), BF16),
                        pltpu.VMEM((Q_PER_KV * tq, ctx_len + seq), F32)],
        compiler_params=_cparams("parallel", "parallel", "arbitrary"),
        name="gqa_attention",
    )(proj, proj, proj, proj, projc, projc, q_gain, k_gain, cos_t, sin_t, cos_t, sin_t)


def _ssd_width(attn_width):
    return attn_width


def _conv_silu(groups, pad_ref, n_rows):
    ch = SSD_CHUNK
    halo = CONV_HALO
    win_rows = ch + 2 * halo
    widths = [g[0].shape[1] for g in groups]
    offs = [sum(widths[:i]) for i in range(len(groups))]
    total = sum(widths)
    pad_ref[pl.ds(0, halo), 0:total] = jnp.zeros((halo, total), BF16)
    pad_ref[pl.ds(halo + n_rows, halo), 0:total] = jnp.zeros((halo, total), BF16)

    def copy(i, carry):
        r = pl.multiple_of(i * ch, ch)
        for (src_ref, _, _, _), o, cw in zip(groups, offs, widths):
            pad_ref[pl.ds(r + halo, ch), o:o + cw] = src_ref[pl.ds(r, ch), :]
        return carry

    lax.fori_loop(0, n_rows // ch, copy, 0)
    w = jnp.concatenate([g[1][...] for g in groups], axis=1)
    bias = jnp.concatenate([g[2][...] for g in groups], axis=1)
    out_row = lax.broadcasted_iota(jnp.int32, (ch, win_rows), 0)
    win_row = lax.broadcasted_iota(jnp.int32, (ch, win_rows), 1)
    taps = [k for k in range(D_CONV) if k != D_CONV // 2]
    selects = [jnp.where(win_row == out_row + (halo + k - D_CONV // 2), 1.0, 0.0).astype(BF16) for k in taps]

    blk = 2 * LANES

    def body(i, carry):
        r = pl.multiple_of(i * ch, ch)
        ys = []
        for c0 in range(0, total, blk):
            cs = slice(c0, min(c0 + blk, total))
            win = pad_ref[pl.ds(r, win_rows), cs]
            acc = bias[:, cs] + win[halo:halo + ch].astype(F32) * w[D_CONV // 2:D_CONV // 2 + 1, cs]
            for k, select in zip(taps, selects):
                acc = acc + jnp.dot(select, win, preferred_element_type=F32) * w[k:k + 1, cs]
            ys.append(_silu(acc))
        y = jnp.concatenate(ys, axis=1)
        for (_, _, _, dst_ref), o, cw in zip(groups, offs, widths):
            dst_ref[pl.ds(r, ch), :] = y[:, o:o + cw].astype(dst_ref.dtype)
        return carry

    lax.fori_loop(0, n_rows // ch, body, 0, unroll=8)


def _ssd_kernel(xl_ref, bl_ref, cl_ref, zl_ref, dtl_ref, xc_ref, bc_ref, dtc_ref,
                wx_ref, wb_ref, wc_ref, bx_ref, bb_ref, bcv_ref, alog_ref, dtb_ref, dsk_ref, nrm_ref,
                o_ref,
                pad_s, xs_s, b_s, c_s, xsc_s, bcc_s, y_s, stf, stb, terms_s, bt_s, eb_s, *, hpg):
    ch = SSD_CHUNK
    seq = xl_ref.shape[0]
    ctx_len = xc_ref.shape[0]
    gw = xl_ref.shape[1]
    npair = gw // LANES
    assert 2 * SSD_HEAD_DIM == LANES and hpg == 2 * npair

    _conv_silu([(xl_ref, wx_ref, bx_ref, xs_s), (bl_ref, wb_ref, bb_ref, b_s), (cl_ref, wc_ref, bcv_ref, c_s)],
               pad_s, seq)
    _conv_silu([(xc_ref, wx_ref, bx_ref, xsc_s), (bc_ref, wb_ref, bb_ref, bcc_s)], pad_s, ctx_len)

    li = lax.broadcasted_iota(jnp.int32, (ch, ch), 0)
    si = lax.broadcasted_iota(jnp.int32, (ch, ch), 1)
    tri_t = (li <= si).astype(F32)
    nr = alog_ref.shape[0]
    below = li > si
    upper = si >= li
    diag = li == si
    xr = lax.broadcasted_iota(jnp.int32, (ch, gw), 0)
    xh = lax.broadcasted_iota(jnp.int32, (ch, gw), 1) // SSD_HEAD_DIM
    spread_f = jnp.where(xr == 2 * nr + xh, 1.0, 0.0).astype(BF16)
    spread_b = jnp.where(xr == 2 * nr + hpg + xh, 1.0, 0.0).astype(BF16)
    row_id = lax.broadcasted_iota(jnp.int32, (nr, ch), 0)
    lane = lax.broadcasted_iota(jnp.int32, (1, LANES), 1)
    head0 = lane < SSD_HEAD_DIM
    neg_a_t = -jnp.exp(alog_ref[...]) * LOG2E
    dt_bias_t = dtb_ref[...]

    def chunk_terms(dt_ref, r):
        x_t = dt_ref[pl.ds(r, ch), :].T[:nr] + dt_bias_t
        dtp_t = jnp.maximum(x_t, 0.0) + jnp.log1p(jnp.exp(-jnp.abs(x_t)))
        da_t = dtp_t * neg_a_t
        p_t = jnp.dot(da_t, tri_t, preferred_element_type=F32, precision=lax.Precision.HIGHEST)
        return dtp_t, p_t, p_t - da_t

    def columns(p_t, e_t):
        tot = p_t[:, ch - 1:ch]
        fac_t = jnp.exp2(jnp.where(row_id < hpg, p_t, tot - e_t))
        pad = jnp.zeros((ch - 3 * nr, ch), F32)
        return jnp.concatenate([p_t, e_t, fac_t, pad], axis=0).T

    def pair_sel(a, b):
        return jnp.where(head0, a, b)

    def block_diag_rhs(x_pair):
        zero = jnp.zeros_like(x_pair)
        return jnp.concatenate([jnp.where(head0, x_pair, zero), jnp.where(head0, zero, x_pair)], axis=0)

    def state_update(st_ref, bt, x_ref, r, w_rows, tot):
        for q in range(npair):
            j0, j1 = 2 * q, 2 * q + 1
            lhs = jnp.concatenate([(bt * w_rows[j0]).astype(BF16), (bt * w_rows[j1]).astype(BF16)], axis=1)
            rhs = block_diag_rhs(x_ref[pl.ds(r, ch), q * LANES:(q + 1) * LANES])
            new = jnp.dot(lhs, rhs, preferred_element_type=F32)
            dec = pair_sel(jnp.exp2(tot[j0]), jnp.exp2(tot[j1]))
            cols = slice(q * LANES, (q + 1) * LANES)
            st_ref[:, cols] = st_ref[:, cols] * dec + new

    def fwd_state_terms(dtp_t, p_t):
        w_rows, tot = [], []
        for j in range(hpg):
            t = p_t[j:j + 1, ch - 1:ch]
            tot.append(t)
            w_rows.append(jnp.exp2(t - p_t[j:j + 1, :]) * dtp_t[j:j + 1, :])
        return w_rows, tot

    def bwd_state_terms(dtp_t, p_t, e_t):
        w_rows, tot = [], []
        for j in range(hpg, 2 * hpg):
            tot.append(p_t[j:j + 1, ch - 1:ch])
            w_rows.append(jnp.exp2(e_t[j:j + 1, :]) * dtp_t[j:j + 1, :])
        return w_rows, tot

    stf[...] = jnp.zeros(stf.shape, F32)
    stb[...] = jnp.zeros(stb.shape, F32)
    n_cc = ctx_len // ch

    def load_terms(c):
        terms = terms_s[c]
        return terms[:nr], terms[nr:2 * nr], terms[2 * nr:]

    def ctx_fwd(c, carry):
        r = pl.multiple_of(c * ch, ch)
        dtp_t, p_t, e_t = chunk_terms(dtc_ref, r)
        terms_s[c] = jnp.concatenate([dtp_t, p_t, e_t], axis=0)
        bt = bcc_s[pl.ds(r, ch), :].astype(F32).T
        bt_s[c] = bt
        w_rows, tot = fwd_state_terms(dtp_t, p_t)
        state_update(stf, bt, xsc_s, r, w_rows, tot)
        return carry

    def ctx_bwd(i, carry):
        c = n_cc - 1 - i
        r = pl.multiple_of(c * ch, ch)
        dtp_t, p_t, e_t = load_terms(c)
        w_rows, tot = bwd_state_terms(dtp_t, p_t, e_t)
        state_update(stb, bt_s[c], xsc_s, r, w_rows, tot)
        return carry

    lax.fori_loop(0, n_cc, ctx_fwd, 0, unroll=2)
    lax.fori_loop(0, n_cc, ctx_bwd, 0, unroll=2)

    n_lc = seq // ch

    def lat_fwd(c, carry):
        r = pl.multiple_of(c * ch, ch)
        dtp_t, p_t, e_t = chunk_terms(dtl_ref, r)
        terms_s[c] = jnp.concatenate([dtp_t, p_t, e_t], axis=0)
        pe = columns(p_t, e_t)
        bm = b_s[pl.ds(r, ch), :]
        cm = c_s[pl.ds(r, ch), :]
        bt = bm.astype(F32).T
        bt_s[c] = bt
        cb = lax.dot_general(cm, bm, (((1,), (1,)), ((), ())), preferred_element_type=F32)
        y_off = jnp.dot(cm, stf[...].astype(BF16), preferred_element_type=F32)
        pe16 = pe.astype(BF16)
        y_off = y_off * jnp.dot(pe16, spread_f, preferred_element_type=F32)
        eb_s[pl.ds(r, ch), :] = jnp.dot(pe16, spread_b, preferred_element_type=F32)
        lg_t = jnp.log2(dtp_t)
        row_f = p_t - lg_t
        row_b = e_t + lg_t
        lg_sum = jnp.log2(dtp_t[:hpg] + dtp_t[hpg:2 * hpg])
        for q in range(npair):
            ms = []
            for j in (2 * q, 2 * q + 1):
                jb = hpg + j
                col = jnp.where(upper, jnp.broadcast_to(p_t[j:j + 1, :], (ch, ch)),
                                jnp.broadcast_to(e_t[jb:jb + 1, :], (ch, ch))).T
                t = jnp.where(below, col - row_f[j:j + 1, :],
                              jnp.where(diag, lg_sum[j:j + 1, :], row_b[jb:jb + 1, :] - col))
                ms.append((cb * jnp.exp2(t)).astype(BF16))
            cols = slice(q * LANES, (q + 1) * LANES)
            rhs = block_diag_rhs(xs_s[pl.ds(r, ch), cols])
            y = jnp.dot(jnp.concatenate(ms, axis=1), rhs, preferred_element_type=F32)
            y_s[pl.ds(r, ch), cols] = y + y_off[:, cols]
        w_rows, tot = fwd_state_terms(dtp_t, p_t)
        state_update(stf, bt, xs_s, r, w_rows, tot)
        return carry

    lax.fori_loop(0, n_lc, lat_fwd, 0, unroll=16)

    dskip = dsk_ref[...]
    nrm = nrm_ref[...]

    def lat_bwd(i, carry):
        c = n_lc - 1 - i
        r = pl.multiple_of(c * ch, ch)
        dtp_t, p_t, e_t = load_terms(c)
        y_off = jnp.dot(c_s[pl.ds(r, ch), :], stb[...].astype(BF16), preferred_element_type=F32)
        y = y_s[pl.ds(r, ch), :] + y_off * eb_s[pl.ds(r, ch), :]
        w_rows, tot = bwd_state_terms(dtp_t, p_t, e_t)
        state_update(stb, bt_s[c], xs_s, r, w_rows, tot)
        y = (y + xs_s[pl.ds(r, ch), :].astype(F32) * dskip) * _silu(zl_ref[pl.ds(r, ch), :].astype(F32))
        y = y * lax.rsqrt(jnp.mean(y * y, axis=-1, keepdims=True) + RMS_EPS) * nrm
        o_ref[pl.ds(r, ch), :] = y.astype(o_ref.dtype)
        return carry

    lax.fori_loop(0, n_lc, lat_bwd, 0, unroll=8)


def _ssd(proj, dt, projc, dtc, conv_w, conv_b, alog_g, dtb_g, dskip_c, ssd_norm, *,
         batch, seq, ctx_len, attn_width):
    ssd_width = _ssd_width(attn_width)
    n_kv = attn_width // HEAD_DIM // Q_PER_KV
    kv_width = n_kv * HEAD_DIM
    gw = ssd_width // N_SSD_GROUPS
    hpg = gw // SSD_HEAD_DIM
    xs_off = 3 * attn_width + 2 * kv_width
    xs_off_c = 2 * kv_width
    assert xs_off % gw == 0 and xs_off_c % gw == 0 and D_STATE == LANES
    nb = N_SSD_GROUPS
    b_cb = (xs_off + ssd_width) // LANES
    c_cb = b_cb + nb
    b_cb_c = (xs_off_c + ssd_width) // LANES
    cw_b = ssd_width // LANES
    nr = alog_g.shape[1]
    in_specs = [
        pl.BlockSpec((seq, gw), lambda b, g: (b, xs_off // gw + g)),
        pl.BlockSpec((seq, LANES), lambda b, g: (b, b_cb + g)),
        pl.BlockSpec((seq, LANES), lambda b, g: (b, c_cb + g)),
        pl.BlockSpec((seq, gw), lambda b, g: (b, 2 * attn_width // gw + g)),
        pl.BlockSpec((seq, LANES), lambda b, g: (b, g)),
        pl.BlockSpec((ctx_len, gw), lambda b, g: (b, xs_off_c // gw + g)),
        pl.BlockSpec((ctx_len, LANES), lambda b, g: (b, b_cb_c + g)),
        pl.BlockSpec((ctx_len, LANES), lambda b, g: (b, g)),
        pl.BlockSpec((D_CONV, gw), lambda b, g: (0, g)),
        pl.BlockSpec((D_CONV, LANES), lambda b, g: (0, cw_b + g)),
        pl.BlockSpec((D_CONV, LANES), lambda b, g: (0, cw_b + nb + g)),
        pl.BlockSpec((1, gw), lambda b, g: (0, g)),
        pl.BlockSpec((1, LANES), lambda b, g: (0, cw_b + g)),
        pl.BlockSpec((1, LANES), lambda b, g: (0, cw_b + nb + g)),
        pl.BlockSpec((None, nr, LANES), lambda b, g: (g, 0, 0)),
        pl.BlockSpec((None, nr, LANES), lambda b, g: (g, 0, 0)),
        pl.BlockSpec((1, gw), lambda b, g: (0, g)),
        pl.BlockSpec((1, gw), lambda b, g: (0, g)),
    ]
    pad_rows = max(seq, ctx_len) + 2 * CONV_HALO
    n_chunks = max(seq, ctx_len) // SSD_CHUNK
    return pl.pallas_call(
        functools.partial(_ssd_kernel, hpg=hpg),
        grid=(batch, N_SSD_GROUPS),
        in_specs=in_specs,
        out_specs=pl.BlockSpec((seq, gw), lambda b, g: (b, g)),
        out_shape=jax.ShapeDtypeStruct((batch * seq, ssd_width), BF16),
        scratch_shapes=[pltpu.VMEM((pad_rows, gw + 2 * D_STATE), BF16),
                        pltpu.VMEM((seq, gw), BF16),
                        pltpu.VMEM((seq, LANES), BF16),
                        pltpu.VMEM((seq, LANES), BF16),
                        pltpu.VMEM((ctx_len, gw), BF16),
                        pltpu.VMEM((ctx_len, LANES), BF16),
                        pltpu.VMEM((seq, gw), F32),
                        pltpu.VMEM((D_STATE, gw), F32),
                        pltpu.VMEM((D_STATE, gw), F32),
                        pltpu.VMEM((n_chunks, 3 * nr, SSD_CHUNK), F32),
                        pltpu.VMEM((n_chunks, D_STATE, SSD_CHUNK), F32),
                        pltpu.VMEM((seq, gw), F32)],
        compiler_params=_cparams("parallel", "parallel"),
        name="ssd_bidir",
    )(proj, proj, proj, proj, dt, projc, projc, dtc,
      conv_w, conv_w, conv_w, conv_b, conv_b, conv_b, alog_g, dtb_g, dskip_c, ssd_norm)


def _outproj_kernel(a_ref, y_ref, w0_ref, w1_ref, x_ref, gate_ref, o_ref):
    acc = jnp.dot(a_ref[...], w0_ref[...], preferred_element_type=F32)
    acc = acc + jnp.dot(y_ref[...], w1_ref[...], preferred_element_type=F32)
    o_ref[...] = x_ref[...] + gate_ref[...] * acc


def _outproj(lhs0, cb0, lhs1, cb1, w, x2d, mod, *, rows_per_mod):
    m, d = x2d.shape
    kh = w.shape[0] // 2
    tm = _tile(rows_per_mod, ROW_TILE)
    tn = _tile(d, OUTPROJ_COL_TILE)
    tiles_per_mod = rows_per_mod // tm
    return pl.pallas_call(
        _outproj_kernel,
        grid=(m // tm, d // tn),
        in_specs=[pl.BlockSpec((tm, kh), lambda i, j: (i, cb0)),
                  pl.BlockSpec((tm, kh), lambda i, j: (i, cb1)),
                  pl.BlockSpec((kh, tn), lambda i, j: (0, j)),
                  pl.BlockSpec((kh, tn), lambda i, j: (1, j)),
                  pl.BlockSpec((tm, tn), lambda i, j: (i, j)),
                  pl.BlockSpec((None, 1, tn), lambda i, j: (3 * (i // tiles_per_mod) + 2, 0, j))],
        out_specs=pl.BlockSpec((tm, tn), lambda i, j: (i, j)),
        out_shape=jax.ShapeDtypeStruct((m, d), F32),
        compiler_params=_cparams("parallel", "arbitrary"),
        name="outproj_residual",
    )(lhs0, lhs1, w, w, x2d, mod)


FFT_RADIX = 8


def _cmul_root(z, k, n):
    re, im = z
    eighth = (8 * k) // n % 8
    assert (8 * k) % n == 0
    r = math.sqrt(0.5)
    if eighth == 0:
        return re, im
    if eighth == 1:
        return (re + im) * r, (im - re) * r
    if eighth == 2:
        return im, -re
    if eighth == 3:
        return (im - re) * r, -(re + im) * r
    nre, nim = _cmul_root(z, k - n // 2, n)
    return -nre, -nim


def _fft_small(v):
    n = len(v)
    if n == 1:
        return v
    ev, od = _fft_small(v[0::2]), _fft_small(v[1::2])
    out = [None] * n
    for k in range(n // 2):
        tr, ti = _cmul_root(od[k], k, n)
        out[k] = (ev[k][0] + tr, ev[k][1] + ti)
        out[k + n // 2] = (ev[k][0] - tr, ev[k][1] - ti)
    return out


def _fourier_kernel(u_ref, z_ref, w_ref, wc_ref, o_ref, uf_scr, y_scr, ab_scr, *, rows_per_step):
    seq, tn = u_ref.shape
    n2 = seq // FFT_RADIX
    rs = rows_per_step

    nslab = tn // LANES

    def widen(i, carry):
        r = pl.multiple_of(i * LANES, LANES)
        for c in range(nslab):
            uf_scr[c, pl.ds(r, LANES), :] = u_ref[pl.ds(r, LANES), c * LANES:(c + 1) * LANES].astype(F32)
        return carry

    lax.fori_loop(0, seq // LANES, widen, 0, unroll=4)
    for a in range(FFT_RADIX):
        ua = jnp.concatenate([uf_scr[c, pl.ds(a, n2, stride=FFT_RADIX), :] for c in range(nslab)], axis=1)
        y_scr[a] = jnp.dot(w_ref[a], ua.astype(BF16), preferred_element_type=F32)

    per_pass = ab_scr.shape[0] // (FFT_RADIX * rs)

    def combine_and_mix(g, carry):
        base = pl.multiple_of(g * (per_pass * rs), per_pass * rs)
        for ii in range(per_pass):
            r = base + ii * rs
            z = [(y_scr[a, pl.ds(r, rs), :], y_scr[a, pl.ds(n2 + r, rs), :]) for a in range(FFT_RADIX)]
            x = _fft_small(z)
            for k1 in range(FFT_RADIX):
                rows = pl.ds((ii * FFT_RADIX + k1) * rs, rs)
                ab_scr[rows, 0:tn] = x[k1][0].astype(BF16)
                ab_scr[rows, tn:2 * tn] = x[k1][1].astype(BF16)
        f = jnp.dot(ab_scr[...], wc_ref[...], preferred_element_type=F32)
        for ii in range(per_pass):
            r = base + ii * rs
            for k1 in range(FFT_RADIX):
                s0 = (ii * FFT_RADIX + k1) * rs
                gate = _silu(z_ref[pl.ds(k1 * n2 + r, rs), :].astype(F32))
                o_ref[pl.ds(k1 * n2 + r, rs), :] = (f[s0:s0 + rs] * gate).astype(o_ref.dtype)
        return carry

    lax.fori_loop(0, n2 // (per_pass * rs), combine_and_mix, 0)


def _fourier_gate(uz, *, batch, seq, width):
    n2 = seq // FFT_RADIX
    gw = width // N_FOURIER_GROUPS
    k2 = np.arange(n2, dtype=np.int64)[None, :, None]
    pos = FFT_RADIX * np.arange(n2, dtype=np.int64)[None, None, :] + np.arange(FFT_RADIX, dtype=np.int64)[:, None, None]
    ang = 2.0 * np.pi * ((k2 * pos) % seq).astype(np.float64) / seq
    w = jnp.asarray((np.concatenate([np.cos(ang), -np.sin(ang)], axis=1) / np.sqrt(seq)).astype(np.float32))
    kc = np.arange(gw, dtype=np.int64)
    angc = 2.0 * np.pi * ((kc[:, None] * kc[None, :]) % gw).astype(np.float64) / gw
    wc = jnp.asarray((np.concatenate([np.cos(angc), np.sin(angc)], axis=0) / np.sqrt(gw)).astype(np.float32))
    rs = 2 * SUBLANES
    per_pass = min(4, n2 // rs)
    return pl.pallas_call(
        functools.partial(_fourier_kernel, rows_per_step=rs),
        grid=(batch, N_FOURIER_GROUPS),
        in_specs=[pl.BlockSpec((seq, gw), lambda b, g: (b, g)),
                  pl.BlockSpec((seq, gw), lambda b, g: (b, N_FOURIER_GROUPS + g)),
                  pl.BlockSpec((FFT_RADIX, 2 * n2, n2), lambda b, g: (0, 0, 0)),
                  pl.BlockSpec((2 * gw, gw), lambda b, g: (0, 0))],
        out_specs=pl.BlockSpec((seq, gw), lambda b, g: (b, g)),
        out_shape=jax.ShapeDtypeStruct((batch * seq, width), BF16),
        scratch_shapes=[pltpu.VMEM((gw // LANES, seq, LANES), F32),
                        pltpu.VMEM((FFT_RADIX, 2 * n2, gw), F32),
                        pltpu.VMEM((per_pass * FFT_RADIX * rs, 2 * gw), BF16)],
        compiler_params=_cparams("parallel", "parallel"),
        name="fourier_mix_gate",
    )(uz, uz, w.astype(BF16), wc.astype(BF16))


def _rope_tables(seq):
    pos = jnp.arange(seq)
    row = (pos // GRID_W).astype(F32)
    col = (pos % GRID_W).astype(F32)
    inv_freq = ROPE_THETA ** (-jnp.arange(0, ROPE_AXIS_DIM, 2, dtype=F32) / ROPE_AXIS_DIM)
    ar, ac = row[:, None] * inv_freq, col[:, None] * inv_freq
    cos_t = jnp.concatenate([jnp.cos(ar), jnp.cos(ar), jnp.cos(ac), jnp.cos(ac)], axis=-1)
    sin_t = jnp.concatenate([-jnp.sin(ar), jnp.sin(ar), -jnp.sin(ac), jnp.sin(ac)], axis=-1)
    return cos_t, sin_t


def _layer(a, j):
    return a.reshape(a.shape[1:]) if a.shape[0] == 1 else a[j]


def _per_group_rows(v, hpg):
    g = v.shape[1] // hpg
    rows = -(-2 * hpg // SUBLANES) * SUBLANES
    t = v.reshape(2, g, hpg).transpose(1, 0, 2).reshape(g, 2 * hpg)
    t = jnp.pad(t, ((0, 0), (0, rows - 2 * hpg)))
    return jnp.broadcast_to(t[:, :, None], (g, rows, LANES))


def kernel(x, c, ctx, c_ctx, ev_norm_w, ev_ada_w, ev_ada_b, ev_w_in, ev_q_norm, ev_k_norm, ev_conv_w,
           ev_conv_b, ev_a_log, ev_dt_bias, ev_d_skip, ev_ssd_norm, ev_w_out, od_norm_w, od_ada_w,
           od_ada_b, od_w_in, od_w_out):
    batch, seq, d = x.shape
    ctx_len = ctx.shape[1]
    mix = ev_w_out.shape[1]
    attn_width = mix // 2
    ssd_width = mix - attn_width
    n_ssd_heads = ssd_width // SSD_HEAD_DIM
    hpg = n_ssd_heads // N_SSD_GROUPS
    n_kv = attn_width // HEAD_DIM // Q_PER_KV
    kv_width = n_kv * HEAD_DIM
    conv_ch = ssd_width + 2 * N_SSD_GROUPS * D_STATE
    ctx_col0 = 2 * attn_width + ssd_width
    n_main = ctx_col0 + 2 * kv_width + conv_ch
    m = batch * seq
    x2d = x.reshape(m, d)
    ctx2d = ctx.reshape(batch * ctx_len, d)
    n_mod_rows = 2 * SUBLANES
    assert batch + 1 <= n_mod_rows

    w_in = _layer(ev_w_in, 0)
    w_main = w_in.astype(BF16)
    qscale = LOG2E * HEAD_DIM ** -0.5
    q_gain = (_layer(ev_q_norm, 0) * qscale).reshape(1, HEAD_DIM)
    k_gain = _layer(ev_k_norm, 0).reshape(1, HEAD_DIM)
    cos_t, sin_t = _rope_tables(seq)
    dt_cols = w_main[:, n_main:]
    lane_pad = jnp.zeros((d, LANES - 2 * hpg), BF16)
    w_dt = jnp.concatenate(
        [blk for g in range(N_SSD_GROUPS)
         for blk in (dt_cols[:, g * hpg:(g + 1) * hpg],
                     dt_cols[:, n_ssd_heads + g * hpg:n_ssd_heads + (g + 1) * hpg], lane_pad)], axis=1)
    cond = jnp.zeros((n_mod_rows, d), F32).at[:batch].set(c).at[batch].set(c_ctx)
    mod0 = _ada(cond, _layer(ev_ada_w, 0), _layer(ev_ada_b, 0)).reshape(3 * n_mod_rows, 1, d)
    norm_w0 = _layer(ev_norm_w, 0)

    proj, dt = _inproj(x2d, mod0, norm_w0, w_main, w_dt,
                       rows_per_mod=seq, mod_row0=0, col0=0, ncols=n_main)
    projc, dtc = _inproj(ctx2d, mod0, norm_w0, w_main, w_dt,
                         rows_per_mod=batch * ctx_len, mod_row0=batch, col0=ctx_col0, ncols=n_main - ctx_col0)

    attn = _attention(proj, projc, q_gain, k_gain, cos_t, sin_t,
                      batch=batch, seq=seq, ctx_len=ctx_len, attn_width=attn_width)
    yssd = _ssd(proj, dt, projc, dtc, _layer(ev_conv_w, 0), _layer(ev_conv_b, 0).reshape(1, conv_ch),
                _per_group_rows(_layer(ev_a_log, 0), hpg), _per_group_rows(_layer(ev_dt_bias, 0), hpg),
                jnp.repeat(_layer(ev_d_skip, 0), SSD_HEAD_DIM).reshape(1, ssd_width),
                _layer(ev_ssd_norm, 0).reshape(1, ssd_width),
                batch=batch, seq=seq, ctx_len=ctx_len, attn_width=attn_width)
    x1 = _outproj(attn, 0, yssd, 0, _layer(ev_w_out, 0).astype(BF16), x2d, mod0, rows_per_mod=seq)

    fw = od_w_out.shape[1]
    cond1 = jnp.zeros((n_mod_rows, d), F32).at[:batch].set(c)
    mod1 = _ada(cond1, _layer(od_ada_w, 0), _layer(od_ada_b, 0)).reshape(3 * n_mod_rows, 1, d)
    uz, _ = _inproj(x1, mod1, _layer(od_norm_w, 0), _layer(od_w_in, 0).astype(BF16), None,
                    rows_per_mod=seq, mod_row0=0, col0=0, ncols=2 * fw)
    fg = _fourier_gate(uz, batch=batch, seq=seq, width=fw)
    x2 = _outproj(fg, 0, fg, 1, _layer(od_w_out, 0).astype(BF16), x1, mod1, rows_per_mod=seq)
    return x2.reshape(batch, seq, d)
```

```python
import functools
import math

import numpy as np
import jax
import jax.numpy as jnp
from jax import lax
from jax.experimental import pallas as pl
from jax.experimental.pallas import tpu as pltpu

F32 = jnp.float32
BF16 = jnp.bfloat16

GRID_W = 64
HEAD_DIM = 128
Q_PER_KV = 4
ROPE_THETA = 10000.0
ROPE_AXIS_DIM = HEAD_DIM // 2
SSD_HEAD_DIM = 64
N_SSD_GROUPS = 4
D_STATE = 128
D_CONV = 5
SSD_CHUNK = 128
N_FOURIER_GROUPS = 8
RMS_EPS = 1e-6

LANES = 128
SUBLANES = 8
VMEM_LIMIT_BYTES = 56 * 1024 * 1024

CONV_HALO = SSD_CHUNK // 2

ROW_TILE = 1024
INPROJ_COL_TILE = 2048
OUTPROJ_COL_TILE = 1024
ADA_COL_TILE = 1024
ATTN_Q_TILE = 512
ATTN_KEY_TILE = 4 * LANES
LOG2E = 1.4426950408889634


def _cparams(*sem):
    return pltpu.CompilerParams(dimension_semantics=sem, vmem_limit_bytes=VMEM_LIMIT_BYTES)


def _tile(n, pref):
    t = min(n, pref)
    while n % t:
        t //= 2
    return t


def _silu(v):
    h = 0.5 * v
    return h + h * jnp.tanh(h)


def _ada_kernel(c_ref, w_ref, b_ref, o_ref):
    c = c_ref[...]
    s = _silu(c).astype(BF16)
    o_ref[...] = jnp.dot(s, w_ref[...].astype(BF16), preferred_element_type=F32) + b_ref[...]


def _ada(cond, w, b):
    r, d = cond.shape
    n = w.shape[1]
    tn = _tile(n, ADA_COL_TILE)
    return pl.pallas_call(
        _ada_kernel,
        grid=(n // tn,),
        in_specs=[pl.BlockSpec((r, d), lambda j: (0, 0)),
                  pl.BlockSpec((d, tn), lambda j: (0, j)),
                  pl.BlockSpec((1, tn), lambda j: (0, j))],
        out_specs=pl.BlockSpec((r, tn), lambda j: (0, j)),
        out_shape=jax.ShapeDtypeStruct((r, n), F32),
        compiler_params=_cparams("parallel"),
        name="ada_mod",
    )(cond, w, b.reshape(1, n))


def _inproj_kernel(x_ref, sh_ref, sc_ref, nw_ref, w_ref, *rest, has_dt, sub):
    if has_dt:
        wdt_ref, o_ref, dt_ref, h_ref = rest
    else:
        o_ref, h_ref = rest
    tm = x_ref.shape[0]

    @pl.when(pl.program_id(1) == 0)
    def _():
        gain = nw_ref[...] * (1.0 + sc_ref[...])
        sh = sh_ref[...]

        def body(s, carry):
            r = pl.multiple_of(s * sub, sub)
            x = x_ref[pl.ds(r, sub), :]
            y = x * lax.rsqrt(jnp.mean(x * x, axis=-1, keepdims=True) + RMS_EPS)
            h_ref[pl.ds(r, sub), :] = (y * gain + sh).astype(BF16)
            return carry

        lax.fori_loop(0, tm // sub, body, 0, unroll=8)
        if has_dt:
            dt_ref[...] = jnp.dot(h_ref[...], wdt_ref[...], preferred_element_type=F32)

    o_ref[...] = jnp.dot(h_ref[...], w_ref[...], preferred_element_type=F32).astype(o_ref.dtype)


def _inproj(x2d, mod, norm_w, w, wdt, *, rows_per_mod, mod_row0, col0, ncols):
    m, d = x2d.shape
    tm = _tile(rows_per_mod, ROW_TILE)
    has_dt = wdt is not None
    tn = _tile(math.gcd(ncols, col0), INPROJ_COL_TILE)
    cb0 = col0 // tn
    tiles_per_mod = rows_per_mod // tm

    def modrow(i):
        return mod_row0 + i // tiles_per_mod

    in_specs = [pl.BlockSpec((tm, d), lambda i, j: (i, 0)),
                pl.BlockSpec((None, 1, d), lambda i, j: (3 * modrow(i), 0, 0)),
                pl.BlockSpec((None, 1, d), lambda i, j: (3 * modrow(i) + 1, 0, 0)),
                pl.BlockSpec((1, d), lambda i, j: (0, 0)),
                pl.BlockSpec((d, tn), lambda i, j: (0, cb0 + j))]
    args = [x2d, mod, mod, norm_w.reshape(1, d), w]
    out_specs = [pl.BlockSpec((tm, tn), lambda i, j: (i, j))]
    out_shape = [jax.ShapeDtypeStruct((m, ncols), BF16)]
    if has_dt:
        ndt = wdt.shape[1]
        in_specs.append(pl.BlockSpec((d, ndt), lambda i, j: (0, 0)))
        args.append(wdt)
        out_specs.append(pl.BlockSpec((tm, ndt), lambda i, j: (i, 0)))
        out_shape.append(jax.ShapeDtypeStruct((m, ndt), F32))
    res = pl.pallas_call(
        functools.partial(_inproj_kernel, has_dt=has_dt, sub=min(32, tm)),
        grid=(m // tm, ncols // tn),
        in_specs=in_specs,
        out_specs=out_specs,
        out_shape=out_shape,
        scratch_shapes=[pltpu.VMEM((tm, d), BF16)],
        compiler_params=_cparams("parallel", "arbitrary"),
        name="mod_inproj",
    )(*args)
    return (res[0], res[1]) if has_dt else (res[0], None)


def _norm_rope(v, gain, cos, sin):
    ones = jnp.ones((HEAD_DIM, HEAD_DIM), BF16)
    ss = jnp.dot((v * v).astype(BF16), ones, preferred_element_type=F32)
    y = v * lax.rsqrt(ss * (1.0 / HEAD_DIM) + RMS_EPS) * gain
    if cos is not None:
        quarter = ROPE_AXIS_DIM // 2
        lane = lax.broadcasted_iota(jnp.int32, (1, HEAD_DIM), 1)
        partner = jnp.where(lane % ROPE_AXIS_DIM < quarter,
                            pltpu.roll(y, HEAD_DIM - quarter, 1), pltpu.roll(y, quarter, 1))
        y = y * cos + partner * sin
    return y.astype(BF16)


def _attn_kernel(q_ref, g_ref, kl_ref, vl_ref, kc_ref, vc_ref, qg_ref, kg_ref, cq_ref, sq_ref, ck_ref, sk_ref,
                 o_ref, k_scr, s_scr, *, nrep, tk):
    tq = q_ref.shape[0]
    rows = nrep * tq
    blocks, col = [], 0
    for v_ref in (vc_ref, vl_ref):
        n = v_ref.shape[0]
        for r0 in range(0, n, tk):
            blocks.append((v_ref, r0, min(tk, n - r0), col))
            col += min(tk, n - r0)

    @pl.when(pl.program_id(2) == 0)
    def _prep_keys():
        kg = kg_ref[...]
        for v_ref, r0, size, c0 in blocks:
            if v_ref is vc_ref:
                k_scr[pl.ds(c0, size), :] = _norm_rope(kc_ref[pl.ds(r0, size), :].astype(F32), kg, None, None)
            else:
                k_scr[pl.ds(c0, size), :] = _norm_rope(kl_ref[pl.ds(r0, size), :].astype(F32), kg,
                                                       ck_ref[pl.ds(r0, size), :], sk_ref[pl.ds(r0, size), :])

    qg = qg_ref[...]
    qs = jnp.concatenate([_norm_rope(q_ref[:, r * HEAD_DIM:(r + 1) * HEAD_DIM].astype(F32), qg,
                                     cq_ref[...], sq_ref[...]) for r in range(nrep)], axis=0)

    n_parts = 2
    half = rows // n_parts
    halves = tuple(slice(i * half, (i + 1) * half) for i in range(n_parts))
    m = [None] * n_parts
    mb = [None] * n_parts
    acc = [jnp.zeros((half, HEAD_DIM + LANES), F32) for _ in halves]

    def pass1(i, blk):
        _, r0, size, c0 = blk
        s = lax.dot_general(qs[halves[i]], k_scr[pl.ds(c0, size), :], (((1,), (1,)), ((), ())),
                            preferred_element_type=F32)
        s_scr[halves[i], c0:c0 + size] = s
        for c in range(size // LANES):
            sc = s[:, c * LANES:(c + 1) * LANES]
            m[i] = sc if m[i] is None else jnp.maximum(m[i], sc)

    def pass2(i, blk):
        v_ref, r0, size, c0 = blk
        v_aug = jnp.concatenate([v_ref[pl.ds(r0, size), :], jnp.ones((size, LANES), BF16)], axis=1)
        p = jnp.concatenate([jnp.exp2(s_scr[halves[i], c0 + c * LANES:c0 + (c + 1) * LANES] - mb[i])
                             for c in range(size // LANES)], axis=1).astype(BF16)
        acc[i] = acc[i] + jnp.dot(p, v_aug, preferred_element_type=F32)

    def row_max(i):
        mb[i] = jnp.broadcast_to(jnp.max(m[i], axis=-1, keepdims=True), (half, LANES))

    for phase in range(n_parts + 1):
        for blk in blocks:
            if phase < n_parts:
                pass1(phase, blk)
            if phase > 0:
                pass2(phase - 1, blk)
        if phase < n_parts:
            row_max(phase)
    acc = jnp.concatenate(acc, axis=0)
    o = acc[:, :HEAD_DIM] / acc[:, HEAD_DIM:]
    for r in range(nrep):
        g = g_ref[:, r * HEAD_DIM:(r + 1) * HEAD_DIM].astype(F32)
        o_ref[:, r * HEAD_DIM:(r + 1) * HEAD_DIM] = (o[r * tq:(r + 1) * tq] * _silu(g)).astype(o_ref.dtype)


def _attention(proj, projc, q_gain, k_gain, cos_t, sin_t, *, batch, seq, ctx_len, attn_width):
    n_kv = attn_width // HEAD_DIM // Q_PER_KV
    gw = Q_PER_KV * HEAD_DIM
    tq = _tile(seq, ATTN_Q_TILE)
    tk = ATTN_KEY_TILE
    assert ctx_len % LANES == 0 and seq % LANES == 0
    tpb = seq // tq
    k_cb = (2 * attn_width + _ssd_width(attn_width)) // HEAD_DIM
    v_cb = k_cb + n_kv
    return pl.pallas_call(
        functools.partial(_attn_kernel, nrep=Q_PER_KV, tk=tk),
        grid=(batch, n_kv, tpb),
        in_specs=[pl.BlockSpec((tq, gw), lambda b, h, t: (b * tpb + t, h)),
                  pl.BlockSpec((tq, gw), lambda b, h, t: (b * tpb + t, attn_width // gw + h)),
                  pl.BlockSpec((seq, HEAD_DIM), lambda b, h, t: (b, k_cb + h)),
                  pl.BlockSpec((seq, HEAD_DIM), lambda b, h, t: (b, v_cb + h)),
                  pl.BlockSpec((ctx_len, HEAD_DIM), lambda b, h, t: (b, h)),
                  pl.BlockSpec((ctx_len, HEAD_DIM), lambda b, h, t: (b, n_kv + h)),
                  pl.BlockSpec((1, HEAD_DIM), lambda b, h, t: (0, 0)),
                  pl.BlockSpec((1, HEAD_DIM), lambda b, h, t: (0, 0)),
                  pl.BlockSpec((tq, HEAD_DIM), lambda b, h, t: (t, 0)),
                  pl.BlockSpec((tq, HEAD_DIM), lambda b, h, t: (t, 0)),
                  pl.BlockSpec((seq, HEAD_DIM), lambda b, h, t: (0, 0)),
                  pl.BlockSpec((seq, HEAD_DIM), lambda b, h, t: (0, 0))],
        out_specs=pl.BlockSpec((tq, gw), lambda b, h, t: (b * tpb + t, h)),
        out_shape=jax.ShapeDtypeStruct((batch * seq, attn_width), BF16),
        scratch_shapes=[pltpu.VMEM((ctx_len + seq, HEAD_DIM), BF16),
                        pltpu.VMEM((Q_PER_KV * tq, ctx_len + seq), F32)],
        compiler_params=_cparams("parallel", "parallel", "arbitrary"),
        name="gqa_attention",
    )(proj, proj, proj, proj, projc, projc, q_gain, k_gain, cos_t, sin_t, cos_t, sin_t)


def _ssd_width(attn_width):
    return attn_width


def _conv_silu(groups, pad_ref, n_rows):
    ch = SSD_CHUNK
    halo = CONV_HALO
    win_rows = ch + 2 * halo
    widths = [g[0].shape[1] for g in groups]
    offs = [sum(widths[:i]) for i in range(len(groups))]
    total = sum(widths)
    pad_ref[pl.ds(0, halo), 0:total] = jnp.zeros((halo, total), BF16)
    pad_ref[pl.ds(halo + n_rows, halo), 0:total] = jnp.zeros((halo, total), BF16)

    def copy(i, carry):
        r = pl.multiple_of(i * ch, ch)
        for (src_ref, _, _, _), o, cw in zip(groups, offs, widths):
            pad_ref[pl.ds(r + halo, ch), o:o + cw] = src_ref[pl.ds(r, ch), :]
        return carry

    lax.fori_loop(0, n_rows // ch, copy, 0)
    w = jnp.concatenate([g[1][...] for g in groups], axis=1)
    bias = jnp.concatenate([g[2][...] for g in groups], axis=1)
    out_row = lax.broadcasted_iota(jnp.int32, (ch, win_rows), 0)
    win_row = lax.broadcasted_iota(jnp.int32, (ch, win_rows), 1)
    taps = [k for k in range(D_CONV) if k != D_CONV // 2]
    selects = [jnp.where(win_row == out_row + (halo + k - D_CONV // 2), 1.0, 0.0).astype(BF16) for k in taps]

    blk = 2 * LANES

    def body(i, carry):
        r = pl.multiple_of(i * ch, ch)
        ys = []
        for c0 in range(0, total, blk):
            cs = slice(c0, min(c0 + blk, total))
            win = pad_ref[pl.ds(r, win_rows), cs]
            acc = bias[:, cs] + win[halo:halo + ch].astype(F32) * w[D_CONV // 2:D_CONV // 2 + 1, cs]
            for k, select in zip(taps, selects):
                acc = acc + jnp.dot(select, win, preferred_element_type=F32) * w[k:k + 1, cs]
            ys.append(_silu(acc))
        y = jnp.concatenate(ys, axis=1)
        for (_, _, _, dst_ref), o, cw in zip(groups, offs, widths):
            dst_ref[pl.ds(r, ch), :] = y[:, o:o + cw].astype(dst_ref.dtype)
        return carry

    lax.fori_loop(0, n_rows // ch, body, 0, unroll=8)


def _ssd_kernel(xl_ref, bl_ref, cl_ref, zl_ref, dtl_ref, xc_ref, bc_ref, dtc_ref,
                wx_ref, wb_ref, wc_ref, bx_ref, bb_ref, bcv_ref, alog_ref, dtb_ref, dsk_ref, nrm_ref,
                o_ref,
                pad_s, xs_s, b_s, c_s, xsc_s, bcc_s, y_s, stf, stb, terms_s, bt_s, eb_s, *, hpg):
    ch = SSD_CHUNK
    seq = xl_ref.shape[0]
    ctx_len = xc_ref.shape[0]
    gw = xl_ref.shape[1]
    npair = gw // LANES
    assert 2 * SSD_HEAD_DIM == LANES and hpg == 2 * npair

    _conv_silu([(xl_ref, wx_ref, bx_ref, xs_s), (bl_ref, wb_ref, bb_ref, b_s), (cl_ref, wc_ref, bcv_ref, c_s)],
               pad_s, seq)
    _conv_silu([(xc_ref, wx_ref, bx_ref, xsc_s), (bc_ref, wb_ref, bb_ref, bcc_s)], pad_s, ctx_len)

    li = lax.broadcasted_iota(jnp.int32, (ch, ch), 0)
    si = lax.broadcasted_iota(jnp.int32, (ch, ch), 1)
    tri_t = (li <= si).astype(F32)
    nr = alog_ref.shape[0]
    below = li > si
    upper = si >= li
    diag = li == si
    xr = lax.broadcasted_iota(jnp.int32, (ch, gw), 0)
    xh = lax.broadcasted_iota(jnp.int32, (ch, gw), 1) // SSD_HEAD_DIM
    spread_f = jnp.where(xr == 2 * nr + xh, 1.0, 0.0).astype(BF16)
    spread_b = jnp.where(xr == 2 * nr + hpg + xh, 1.0, 0.0).astype(BF16)
    row_id = lax.broadcasted_iota(jnp.int32, (nr, ch), 0)
    lane = lax.broadcasted_iota(jnp.int32, (1, LANES), 1)
    head0 = lane < SSD_HEAD_DIM
    neg_a_t = -jnp.exp(alog_ref[...]) * LOG2E
    dt_bias_t = dtb_ref[...]

    def chunk_terms(dt_ref, r):
        x_t = dt_ref[pl.ds(r, ch), :].T[:nr] + dt_bias_t
        dtp_t = jnp.maximum(x_t, 0.0) + jnp.log1p(jnp.exp(-jnp.abs(x_t)))
        da_t = dtp_t * neg_a_t
        p_t = jnp.dot(da_t, tri_t, preferred_element_type=F32, precision=lax.Precision.HIGHEST)
        return dtp_t, p_t, p_t - da_t

    def columns(p_t, e_t):
        tot = p_t[:, ch - 1:ch]
        fac_t = jnp.exp2(jnp.where(row_id < hpg, p_t, tot - e_t))
        pad = jnp.zeros((ch - 3 * nr, ch), F32)
        return jnp.concatenate([p_t, e_t, fac_t, pad], axis=0).T

    def pair_sel(a, b):
        return jnp.where(head0, a, b)

    def block_diag_rhs(x_pair):
        zero = jnp.zeros_like(x_pair)
        return jnp.concatenate([jnp.where(head0, x_pair, zero), jnp.where(head0, zero, x_pair)], axis=0)

    def state_update(st_ref, bt, x_ref, r, w_rows, tot):
        for q in range(npair):
            j0, j1 = 2 * q, 2 * q + 1
            lhs = jnp.concatenate([(bt * w_rows[j0]).astype(BF16), (bt * w_rows[j1]).astype(BF16)], axis=1)
            rhs = block_diag_rhs(x_ref[pl.ds(r, ch), q * LANES:(q + 1) * LANES])
            new = jnp.dot(lhs, rhs, preferred_element_type=F32)
            dec = pair_sel(jnp.exp2(tot[j0]), jnp.exp2(tot[j1]))
            cols = slice(q * LANES, (q + 1) * LANES)
            st_ref[:, cols] = st_ref[:, cols] * dec + new

    def fwd_state_terms(dtp_t, p_t):
        w_rows, tot = [], []
        for j in range(hpg):
            t = p_t[j:j + 1, ch - 1:ch]
            tot.append(t)
            w_rows.append(jnp.exp2(t - p_t[j:j + 1, :]) * dtp_t[j:j + 1, :])
        return w_rows, tot

    def bwd_state_terms(dtp_t, p_t, e_t):
        w_rows, tot = [], []
        for j in range(hpg, 2 * hpg):
            tot.append(p_t[j:j + 1, ch - 1:ch])
            w_rows.append(jnp.exp2(e_t[j:j + 1, :]) * dtp_t[j:j + 1, :])
        return w_rows, tot

    stf[...] = jnp.zeros(stf.shape, F32)
    stb[...] = jnp.zeros(stb.shape, F32)
    n_cc = ctx_len // ch

    def load_terms(c):
        terms = terms_s[c]
        return terms[:nr], terms[nr:2 * nr], terms[2 * nr:]

    def ctx_fwd(c, carry):
        r = pl.multiple_of(c * ch, ch)
        dtp_t, p_t, e_t = chunk_terms(dtc_ref, r)
        terms_s[c] = jnp.concatenate([dtp_t, p_t, e_t], axis=0)
        bt = bcc_s[pl.ds(r, ch), :].astype(F32).T
        bt_s[c] = bt
        w_rows, tot = fwd_state_terms(dtp_t, p_t)
        state_update(stf, bt, xsc_s, r, w_rows, tot)
        return carry

    def ctx_bwd(i, carry):
        c = n_cc - 1 - i
        r = pl.multiple_of(c * ch, ch)
        dtp_t, p_t, e_t = load_terms(c)
        w_rows, tot = bwd_state_terms(dtp_t, p_t, e_t)
        state_update(stb, bt_s[c], xsc_s, r, w_rows, tot)
        return carry

    lax.fori_loop(0, n_cc, ctx_fwd, 0, unroll=2)
    lax.fori_loop(0, n_cc, ctx_bwd, 0, unroll=2)

    n_lc = seq // ch

    def lat_fwd(c, carry):
        r = pl.multiple_of(c * ch, ch)
        dtp_t, p_t, e_t = chunk_terms(dtl_ref, r)
        terms_s[c] = jnp.concatenate([dtp_t, p_t, e_t], axis=0)
        pe = columns(p_t, e_t)
        bm = b_s[pl.ds(r, ch), :]
        cm = c_s[pl.ds(r, ch), :]
        bt = bm.astype(F32).T
        bt_s[c] = bt
        cb = lax.dot_general(cm, bm, (((1,), (1,)), ((), ())), preferred_element_type=F32)
        y_off = jnp.dot(cm, stf[...].astype(BF16), preferred_element_type=F32)
        pe16 = pe.astype(BF16)
        y_off = y_off * jnp.dot(pe16, spread_f, preferred_element_type=F32)
        eb_s[pl.ds(r, ch), :] = jnp.dot(pe16, spread_b, preferred_element_type=F32)
        lg_t = jnp.log2(dtp_t)
        row_f = p_t - lg_t
        row_b = e_t + lg_t
        lg_sum = jnp.log2(dtp_t[:hpg] + dtp_t[hpg:2 * hpg])
        for q in range(npair):
            ms = []
            for j in (2 * q, 2 * q + 1):
                jb = hpg + j
                col = jnp.where(upper, jnp.broadcast_to(p_t[j:j + 1, :], (ch, ch)),
                                jnp.broadcast_to(e_t[jb:jb + 1, :], (ch, ch))).T
                t = jnp.where(below, col - row_f[j:j + 1, :],
                              jnp.where(diag, lg_sum[j:j + 1, :], row_b[jb:jb + 1, :] - col))
                ms.append((cb * jnp.exp2(t)).astype(BF16))
            cols = slice(q * LANES, (q + 1) * LANES)
            rhs = block_diag_rhs(xs_s[pl.ds(r, ch), cols])
            y = jnp.dot(jnp.concatenate(ms, axis=1), rhs, preferred_element_type=F32)
            y_s[pl.ds(r, ch), cols] = y + y_off[:, cols]
        w_rows, tot = fwd_state_terms(dtp_t, p_t)
        state_update(stf, bt, xs_s, r, w_rows, tot)
        return carry

    lax.fori_loop(0, n_lc, lat_fwd, 0, unroll=16)

    dskip = dsk_ref[...]
    nrm = nrm_ref[...]

    def lat_bwd(i, carry):
        c = n_lc - 1 - i
        r = pl.multiple_of(c * ch, ch)
        dtp_t, p_t, e_t = load_terms(c)
        y_off = jnp.dot(c_s[pl.ds(r, ch), :], stb[...].astype(BF16), preferred_element_type=F32)
        y = y_s[pl.ds(r, ch), :] + y_off * eb_s[pl.ds(r, ch), :]
        w_rows, tot = bwd_state_terms(dtp_t, p_t, e_t)
        state_update(stb, bt_s[c], xs_s, r, w_rows, tot)
        y = (y + xs_s[pl.ds(r, ch), :].astype(F32) * dskip) * _silu(zl_ref[pl.ds(r, ch), :].astype(F32))
        y = y * lax.rsqrt(jnp.mean(y * y, axis=-1, keepdims=True) + RMS_EPS) * nrm
        o_ref[pl.ds(r, ch), :] = y.astype(o_ref.dtype)
        return carry

    lax.fori_loop(0, n_lc, lat_bwd, 0, unroll=16)


def _ssd(proj, dt, projc, dtc, conv_w, conv_b, alog_g, dtb_g, dskip_c, ssd_norm, *,
         batch, seq, ctx_len, attn_width):
    ssd_width = _ssd_width(attn_width)
    n_kv = attn_width // HEAD_DIM // Q_PER_KV
    kv_width = n_kv * HEAD_DIM
    gw = ssd_width // N_SSD_GROUPS
    hpg = gw // SSD_HEAD_DIM
    xs_off = 3 * attn_width + 2 * kv_width
    xs_off_c = 2 * kv_width
    assert xs_off % gw == 0 and xs_off_c % gw == 0 and D_STATE == LANES
    nb = N_SSD_GROUPS
    b_cb = (xs_off + ssd_width) // LANES
    c_cb = b_cb + nb
    b_cb_c = (xs_off_c + ssd_width) // LANES
    cw_b = ssd_width // LANES
    nr = alog_g.shape[1]
    in_specs = [
        pl.BlockSpec((seq, gw), lambda b, g: (b, xs_off // gw + g)),
        pl.BlockSpec((seq, LANES), lambda b, g: (b, b_cb + g)),
        pl.BlockSpec((seq, LANES), lambda b, g: (b, c_cb + g)),
        pl.BlockSpec((seq, gw), lambda b, g: (b, 2 * attn_width // gw + g)),
        pl.BlockSpec((seq, LANES), lambda b, g: (b, g)),
        pl.BlockSpec((ctx_len, gw), lambda b, g: (b, xs_off_c // gw + g)),
        pl.BlockSpec((ctx_len, LANES), lambda b, g: (b, b_cb_c + g)),
        pl.BlockSpec((ctx_len, LANES), lambda b, g: (b, g)),
        pl.BlockSpec((D_CONV, gw), lambda b, g: (0, g)),
        pl.BlockSpec((D_CONV, LANES), lambda b, g: (0, cw_b + g)),
        pl.BlockSpec((D_CONV, LANES), lambda b, g: (0, cw_b + nb + g)),
        pl.BlockSpec((1, gw), lambda b, g: (0, g)),
        pl.BlockSpec((1, LANES), lambda b, g: (0, cw_b + g)),
        pl.BlockSpec((1, LANES), lambda b, g: (0, cw_b + nb + g)),
        pl.BlockSpec((None, nr, LANES), lambda b, g: (g, 0, 0)),
        pl.BlockSpec((None, nr, LANES), lambda b, g: (g, 0, 0)),
        pl.BlockSpec((1, gw), lambda b, g: (0, g)),
        pl.BlockSpec((1, gw), lambda b, g: (0, g)),
    ]
    pad_rows = max(seq, ctx_len) + 2 * CONV_HALO
    n_chunks = max(seq, ctx_len) // SSD_CHUNK
    return pl.pallas_call(
        functools.partial(_ssd_kernel, hpg=hpg),
        grid=(batch, N_SSD_GROUPS),
        in_specs=in_specs,
        out_specs=pl.BlockSpec((seq, gw), lambda b, g: (b, g)),
        out_shape=jax.ShapeDtypeStruct((batch * seq, ssd_width), BF16),
        scratch_shapes=[pltpu.VMEM((pad_rows, gw + 2 * D_STATE), BF16),
                        pltpu.VMEM((seq, gw), BF16),
                        pltpu.VMEM((seq, LANES), BF16),
                        pltpu.VMEM((seq, LANES), BF16),
                        pltpu.VMEM((ctx_len, gw), BF16),
                        pltpu.VMEM((ctx_len, LANES), BF16),
                        pltpu.VMEM((seq, gw), F32),
                        pltpu.VMEM((D_STATE, gw), F32),
                        pltpu.VMEM((D_STATE, gw), F32),
                        pltpu.VMEM((n_chunks, 3 * nr, SSD_CHUNK), F32),
                        pltpu.VMEM((n_chunks, D_STATE, SSD_CHUNK), F32),
                        pltpu.VMEM((seq, gw), F32)],
        compiler_params=_cparams("parallel", "parallel"),
        name="ssd_bidir",
    )(proj, proj, proj, proj, dt, projc, projc, dtc,
      conv_w, conv_w, conv_w, conv_b, conv_b, conv_b, alog_g, dtb_g, dskip_c, ssd_norm)


def _outproj_kernel(a_ref, y_ref, w0_ref, w1_ref, x_ref, gate_ref, o_ref):
    acc = jnp.dot(a_ref[...], w0_ref[...], preferred_element_type=F32)
    acc = acc + jnp.dot(y_ref[...], w1_ref[...], preferred_element_type=F32)
    o_ref[...] = x_ref[...] + gate_ref[...] * acc


def _outproj(lhs0, cb0, lhs1, cb1, w, x2d, mod, *, rows_per_mod):
    m, d = x2d.shape
    kh = w.shape[0] // 2
    tm = _tile(rows_per_mod, ROW_TILE)
    tn = _tile(d, OUTPROJ_COL_TILE)
    tiles_per_mod = rows_per_mod // tm
    return pl.pallas_call(
        _outproj_kernel,
        grid=(m // tm, d // tn),
        in_specs=[pl.BlockSpec((tm, kh), lambda i, j: (i, cb0)),
                  pl.BlockSpec((tm, kh), lambda i, j: (i, cb1)),
                  pl.BlockSpec((kh, tn), lambda i, j: (0, j)),
                  pl.BlockSpec((kh, tn), lambda i, j: (1, j)),
                  pl.BlockSpec((tm, tn), lambda i, j: (i, j)),
                  pl.BlockSpec((None, 1, tn), lambda i, j: (3 * (i // tiles_per_mod) + 2, 0, j))],
        out_specs=pl.BlockSpec((tm, tn), lambda i, j: (i, j)),
        out_shape=jax.ShapeDtypeStruct((m, d), F32),
        compiler_params=_cparams("parallel", "arbitrary"),
        name="outproj_residual",
    )(lhs0, lhs1, w, w, x2d, mod)


FFT_RADIX = 8


def _cmul_root(z, k, n):
    re, im = z
    eighth = (8 * k) // n % 8
    assert (8 * k) % n == 0
    r = math.sqrt(0.5)
    if eighth == 0:
        return re, im
    if eighth == 1:
        return (re + im) * r, (im - re) * r
    if eighth == 2:
        return im, -re
    if eighth == 3:
        return (im - re) * r, -(re + im) * r
    nre, nim = _cmul_root(z, k - n // 2, n)
    return -nre, -nim


def _fft_small(v):
    n = len(v)
    if n == 1:
        return v
    ev, od = _fft_small(v[0::2]), _fft_small(v[1::2])
    out = [None] * n
    for k in range(n // 2):
        tr, ti = _cmul_root(od[k], k, n)
        out[k] = (ev[k][0] + tr, ev[k][1] + ti)
        out[k + n // 2] = (ev[k][0] - tr, ev[k][1] - ti)
    return out


def _fourier_kernel(u_ref, z_ref, w_ref, wc_ref, o_ref, uf_scr, y_scr, ab_scr, *, rows_per_step):
    seq, tn = u_ref.shape
    n2 = seq // FFT_RADIX
    rs = rows_per_step

    nslab = tn // LANES

    def widen(i, carry):
        r = pl.multiple_of(i * LANES, LANES)
        for c in range(nslab):
            uf_scr[c, pl.ds(r, LANES), :] = u_ref[pl.ds(r, LANES), c * LANES:(c + 1) * LANES].astype(F32)
        return carry

    lax.fori_loop(0, seq // LANES, widen, 0, unroll=4)
    for a in range(FFT_RADIX):
        ua = jnp.concatenate([uf_scr[c, pl.ds(a, n2, stride=FFT_RADIX), :] for c in range(nslab)], axis=1)
        y_scr[a] = jnp.dot(w_ref[a], ua.astype(BF16), preferred_element_type=F32)

    per_pass = ab_scr.shape[0] // (FFT_RADIX * rs)

    def combine_and_mix(g, carry):
        base = pl.multiple_of(g * (per_pass * rs), per_pass * rs)
        for ii in range(per_pass):
            r = base + ii * rs
            z = [(y_scr[a, pl.ds(r, rs), :], y_scr[a, pl.ds(n2 + r, rs), :]) for a in range(FFT_RADIX)]
            x = _fft_small(z)
            for k1 in range(FFT_RADIX):
                rows = pl.ds((ii * FFT_RADIX + k1) * rs, rs)
                ab_scr[rows, 0:tn] = x[k1][0].astype(BF16)
                ab_scr[rows, tn:2 * tn] = x[k1][1].astype(BF16)
        f = jnp.dot(ab_scr[...], wc_ref[...], preferred_element_type=F32)
        for ii in range(per_pass):
            r = base + ii * rs
            for k1 in range(FFT_RADIX):
                s0 = (ii * FFT_RADIX + k1) * rs
                gate = _silu(z_ref[pl.ds(k1 * n2 + r, rs), :].astype(F32))
                o_ref[pl.ds(k1 * n2 + r, rs), :] = (f[s0:s0 + rs] * gate).astype(o_ref.dtype)
        return carry

    lax.fori_loop(0, n2 // (per_pass * rs), combine_and_mix, 0, unroll=2)


def _fourier_gate(uz, *, batch, seq, width):
    n2 = seq // FFT_RADIX
    gw = width // N_FOURIER_GROUPS
    k2 = np.arange(n2, dtype=np.int64)[None, :, None]
    pos = FFT_RADIX * np.arange(n2, dtype=np.int64)[None, None, :] + np.arange(FFT_RADIX, dtype=np.int64)[:, None, None]
    ang = 2.0 * np.pi * ((k2 * pos) % seq).astype(np.float64) / seq
    w = jnp.asarray((np.concatenate([np.cos(ang), -np.sin(ang)], axis=1) / np.sqrt(seq)).astype(np.float32))
    kc = np.arange(gw, dtype=np.int64)
    angc = 2.0 * np.pi * ((kc[:, None] * kc[None, :]) % gw).astype(np.float64) / gw
    wc = jnp.asarray((np.concatenate([np.cos(angc), np.sin(angc)], axis=0) / np.sqrt(gw)).astype(np.float32))
    rs = 2 * SUBLANES
    per_pass = min(4, n2 // rs)
    return pl.pallas_call(
        functools.partial(_fourier_kernel, rows_per_step=rs),
        grid=(batch, N_FOURIER_GROUPS),
        in_specs=[pl.BlockSpec((seq, gw), lambda b, g: (b, g)),
                  pl.BlockSpec((seq, gw), lambda b, g: (b, N_FOURIER_GROUPS + g)),
                  pl.BlockSpec((FFT_RADIX, 2 * n2, n2), lambda b, g: (0, 0, 0)),
                  pl.BlockSpec((2 * gw, gw), lambda b, g: (0, 0))],
        out_specs=pl.BlockSpec((seq, gw), lambda b, g: (b, g)),
        out_shape=jax.ShapeDtypeStruct((batch * seq, width), BF16),
        scratch_shapes=[pltpu.VMEM((gw // LANES, seq, LANES), F32),
                        pltpu.VMEM((FFT_RADIX, 2 * n2, gw), F32),
                        pltpu.VMEM((per_pass * FFT_RADIX * rs, 2 * gw), BF16)],
        compiler_params=_cparams("parallel", "parallel"),
        name="fourier_mix_gate",
    )(uz, uz, w.astype(BF16), wc.astype(BF16))


def _rope_tables(seq):
    pos = jnp.arange(seq)
    row = (pos // GRID_W).astype(F32)
    col = (pos % GRID_W).astype(F32)
    inv_freq = ROPE_THETA ** (-jnp.arange(0, ROPE_AXIS_DIM, 2, dtype=F32) / ROPE_AXIS_DIM)
    ar, ac = row[:, None] * inv_freq, col[:, None] * inv_freq
    cos_t = jnp.concatenate([jnp.cos(ar), jnp.cos(ar), jnp.cos(ac), jnp.cos(ac)], axis=-1)
    sin_t = jnp.concatenate([-jnp.sin(ar), jnp.sin(ar), -jnp.sin(ac), jnp.sin(ac)], axis=-1)
    return cos_t, sin_t


def _layer(a, j):
    return a.reshape(a.shape[1:]) if a.shape[0] == 1 else a[j]


def _per_group_rows(v, hpg):
    g = v.shape[1] // hpg
    rows = -(-2 * hpg // SUBLANES) * SUBLANES
    t = v.reshape(2, g, hpg).transpose(1, 0, 2).reshape(g, 2 * hpg)
    t = jnp.pad(t, ((0, 0), (0, rows - 2 * hpg)))
    return jnp.broadcast_to(t[:, :, None], (g, rows, LANES))


def kernel(x, c, ctx, c_ctx, ev_norm_w, ev_ada_w, ev_ada_b, ev_w_in, ev_q_norm, ev_k_norm, ev_conv_w,
           ev_conv_b, ev_a_log, ev_dt_bias, ev_d_skip, ev_ssd_norm, ev_w_out, od_norm_w, od_ada_w,
           od_ada_b, od_w_in, od_w_out):
    batch, seq, d = x.shape
    ctx_len = ctx.shape[1]
    mix = ev_w_out.shape[1]
    attn_width = mix // 2
    ssd_width = mix - attn_width
    n_ssd_heads = ssd_width // SSD_HEAD_DIM
    hpg = n_ssd_heads // N_SSD_GROUPS
    n_kv = attn_width // HEAD_DIM // Q_PER_KV
    kv_width = n_kv * HEAD_DIM
    conv_ch = ssd_width + 2 * N_SSD_GROUPS * D_STATE
    ctx_col0 = 2 * attn_width + ssd_width
    n_main = ctx_col0 + 2 * kv_width + conv_ch
    m = batch * seq
    x2d = x.reshape(m, d)
    ctx2d = ctx.reshape(batch * ctx_len, d)
    n_mod_rows = 2 * SUBLANES
    assert batch + 1 <= n_mod_rows

    w_in = _layer(ev_w_in, 0)
    w_main = w_in.astype(BF16)
    qscale = LOG2E * HEAD_DIM ** -0.5
    q_gain = (_layer(ev_q_norm, 0) * qscale).reshape(1, HEAD_DIM)
    k_gain = _layer(ev_k_norm, 0).reshape(1, HEAD_DIM)
    cos_t, sin_t = _rope_tables(seq)
    dt_cols = w_main[:, n_main:]
    lane_pad = jnp.zeros((d, LANES - 2 * hpg), BF16)
    w_dt = jnp.concatenate(
        [blk for g in range(N_SSD_GROUPS)
         for blk in (dt_cols[:, g * hpg:(g + 1) * hpg],
                     dt_cols[:, n_ssd_heads + g * hpg:n_ssd_heads + (g + 1) * hpg], lane_pad)], axis=1)
    cond = jnp.zeros((n_mod_rows, d), F32).at[:batch].set(c).at[batch].set(c_ctx)
    mod0 = _ada(cond, _layer(ev_ada_w, 0), _layer(ev_ada_b, 0)).reshape(3 * n_mod_rows, 1, d)
    norm_w0 = _layer(ev_norm_w, 0)

    proj, dt = _inproj(x2d, mod0, norm_w0, w_main, w_dt,
                       rows_per_mod=seq, mod_row0=0, col0=0, ncols=n_main)
    projc, dtc = _inproj(ctx2d, mod0, norm_w0, w_main, w_dt,
                         rows_per_mod=batch * ctx_len, mod_row0=batch, col0=ctx_col0, ncols=n_main - ctx_col0)

    attn = _attention(proj, projc, q_gain, k_gain, cos_t, sin_t,
                      batch=batch, seq=seq, ctx_len=ctx_len, attn_width=attn_width)
    yssd = _ssd(proj, dt, projc, dtc, _layer(ev_conv_w, 0), _layer(ev_conv_b, 0).reshape(1, conv_ch),
                _per_group_rows(_layer(ev_a_log, 0), hpg), _per_group_rows(_layer(ev_dt_bias, 0), hpg),
                jnp.repeat(_layer(ev_d_skip, 0), SSD_HEAD_DIM).reshape(1, ssd_width),
                _layer(ev_ssd_norm, 0).reshape(1, ssd_width),
                batch=batch, seq=seq, ctx_len=ctx_len, attn_width=attn_width)
    x1 = _outproj(attn, 0, yssd, 0, _layer(ev_w_out, 0).astype(BF16), x2d, mod0, rows_per_mod=seq)

    fw = od_w_out.shape[1]
    cond1 = jnp.zeros((n_mod_rows, d), F32).at[:batch].set(c)
    mod1 = _ada(cond1, _layer(od_ada_w, 0), _layer(od_ada_b, 0)).reshape(3 * n_mod_rows, 1, d)
    uz, _ = _inproj(x1, mod1, _layer(od_norm_w, 0), _layer(od_w_in, 0).astype(BF16), None,
                    rows_per_mod=seq, mod_row0=0, col0=0, ncols=2 * fw)
    fg = _fourier_gate(uz, batch=batch, seq=seq, width=fw)
    x2 = _outproj(fg, 0, fg, 1, _layer(od_w_out, 0).astype(BF16), x1, mod1, rows_per_mod=seq)
    return x2.reshape(batch, seq, d)
```
